```python
import jax
import jax.numpy as jnp
from jax import lax
import numpy as np

D_MODEL = 2048
BATCH = 4
SEQ = 4096
DEPTH = 2

HEAD_DIM = 128
CONV_CH = 1024
CONV_WIDTH = 31
SB_HEADS = 8
FOX_HEADS = 8
SC_CH = 1024
SC_WIDTH = 3
N_GROUPS = 4
EXPERTS_PER_GROUP = 8
N_EXPERTS = N_GROUPS * EXPERTS_PER_GROUP
TOP_K = 2
EXPERT_FF = 1024
Q_BLOCK = 128
MOE_BLOCK = 256
RMS_EPS = 1e-6
LN_EPS = 1e-5
N_EVEN = (DEPTH + 1) // 2
N_ODD = DEPTH // 2
SB_WIDTH = SB_HEADS * HEAD_DIM
FOX_WIDTH = FOX_HEADS * HEAD_DIM
EVEN_IN = 2 * CONV_CH + 3 * SB_WIDTH
EVEN_MIX = CONV_CH + SB_WIDTH
ODD_IN = 3 * FOX_WIDTH + FOX_HEADS + 3 * SC_CH
ODD_MIX = FOX_WIDTH + SC_CH

kernel_name = "hybrid_conformer_stickbreak_fox_shortconv_hmoe"


def rms_norm(x, g):
    xf = x.astype(jnp.float32)
    y = xf * lax.rsqrt(jnp.mean(xf * xf, axis=-1, keepdims=True) + RMS_EPS)
    return (y * g.astype(jnp.float32)).astype(x.dtype)


def layer_norm(x, g, b):
    xf = x.astype(jnp.float32)
    mu = jnp.mean(xf, axis=-1, keepdims=True)
    xc = xf - mu
    var = jnp.mean(xc * xc, axis=-1, keepdims=True)
    y = xc * lax.rsqrt(var + LN_EPS) * g.astype(jnp.float32) + b.astype(jnp.float32)
    return y.astype(x.dtype)


def causal_depthwise_conv(h, w):
    width = w.shape[0]
    return lax.conv_general_dilated(
        h, w[:, None, :].astype(h.dtype), window_strides=(1,),
        padding=[(width - 1, 0)], dimension_numbers=("NWC", "WIO", "NWC"),
        feature_group_count=h.shape[-1])


def split_heads(t, n_heads):
    b, s, _ = t.shape
    return t.reshape(b, s, n_heads, HEAD_DIM).transpose(0, 2, 1, 3)


def merge_heads(t):
    b, h, s, dh = t.shape
    return t.transpose(0, 2, 1, 3).reshape(b, s, h * dh)


def to_query_blocks(t):
    b, h, s = t.shape[:3]
    t = t.reshape(b, h, s // Q_BLOCK, Q_BLOCK, *t.shape[3:])
    return jnp.moveaxis(t, 2, 0)


def from_query_blocks(t):
    nb, b, h, q, dh = t.shape
    return jnp.moveaxis(t, 0, 2).reshape(b, h, nb * q, dh)


def stick_breaking_attention(q, k, v):
    s_len = q.shape[2]
    scale = HEAD_DIM ** -0.5
    kpos = jnp.arange(s_len)

    def block(args):
        i, qi = args
        qpos = i * Q_BLOCK + jnp.arange(Q_BLOCK)
        z = jnp.einsum("bhqd,bhkd->bhqk", qi, k,
                       preferred_element_type=jnp.float32) * scale
        strict = kpos[None, :] < qpos[:, None]
        log_beta = jax.nn.log_sigmoid(z)
        log_one_minus = jnp.where(strict, jax.nn.log_sigmoid(-z), 0.0)
        suffix = lax.cumsum(log_one_minus, axis=3, reverse=True) - log_one_minus
        weights = jnp.where(strict, jnp.exp(log_beta + suffix), 0.0)
        return jnp.einsum("bhqk,bhkd->bhqd", weights.astype(v.dtype), v)

    nb = s_len // Q_BLOCK
    out = lax.map(block, (jnp.arange(nb), to_query_blocks(q)))
    return from_query_blocks(out)


def forgetting_attention(q, k, v, log_f):
    s_len = q.shape[2]
    scale = HEAD_DIM ** -0.5
    kpos = jnp.arange(s_len)
    c = lax.cumsum(log_f, axis=2)

    def block(args):
        i, qi, ci = args
        qpos = i * Q_BLOCK + jnp.arange(Q_BLOCK)
        z = jnp.einsum("bhqd,bhkd->bhqk", qi, k,
                       preferred_element_type=jnp.float32) * scale
        z = z + (ci[..., :, None] - c[..., None, :])
        causal = kpos[None, :] <= qpos[:, None]
        z = jnp.where(causal, z, -jnp.inf)
        p = jax.nn.softmax(z, axis=-1)
        return jnp.einsum("bhqk,bhkd->bhqd", p.astype(v.dtype), v)

    nb = s_len // Q_BLOCK
    out = lax.map(block, (jnp.arange(nb), to_query_blocks(q), to_query_blocks(c)))
    return from_query_blocks(out)


def even_mixer(h, w_in, conv_w, conv_b, conv_norm_g, conv_norm_b, q_norm, k_norm, w_out):
    u = h @ w_in
    a_val, a_gate, qkv = jnp.split(u, [CONV_CH, 2 * CONV_CH], axis=-1)
    a = a_val * jax.nn.sigmoid(a_gate)
    a = causal_depthwise_conv(a, conv_w) + conv_b.astype(a.dtype)
    a = jax.nn.silu(layer_norm(a, conv_norm_g, conv_norm_b))
    q, k, v = jnp.split(qkv, 3, axis=-1)
    q = rms_norm(split_heads(q, SB_HEADS), q_norm)
    k = rms_norm(split_heads(k, SB_HEADS), k_norm)
    v = split_heads(v, SB_HEADS)
    o = merge_heads(stick_breaking_attention(q, k, v))
    return jnp.concatenate([a, o], axis=-1) @ w_out


def odd_mixer(h, w_in, forget_b, q_norm, k_norm, sc_w, w_out):
    u = h @ w_in
    n_qkv = 3 * FOX_WIDTH
    qkv, f_logit, sc = jnp.split(u, [n_qkv, n_qkv + FOX_HEADS], axis=-1)
    q, k, v = jnp.split(qkv, 3, axis=-1)
    q = rms_norm(split_heads(q, FOX_HEADS), q_norm)
    k = rms_norm(split_heads(k, FOX_HEADS), k_norm)
    v = split_heads(v, FOX_HEADS)
    log_f = jax.nn.log_sigmoid(f_logit.astype(jnp.float32) + forget_b.astype(jnp.float32))
    o = merge_heads(forgetting_attention(q, k, v, log_f.transpose(0, 2, 1)))
    b_gate, c_gate, xv = jnp.split(sc, 3, axis=-1)
    y = b_gate * causal_depthwise_conv(c_gate * xv, sc_w)
    return jnp.concatenate([o, y], axis=-1) @ w_out


def routed_experts(xt, expert_ids, gates, w_gate, w_up, w_down):
    n_tok, d = xt.shape
    m = n_tok * TOP_K
    flat_e = expert_ids.reshape(m)
    flat_g = gates.reshape(m)
    flat_tok = jnp.arange(m, dtype=jnp.int32) // TOP_K
    order = jnp.argsort(flat_e)
    sorted_e = flat_e[order]
    counts = jnp.bincount(flat_e, length=N_EXPERTS)
    padded = (counts + MOE_BLOCK - 1) // MOE_BLOCK * MOE_BLOCK
    starts = jnp.cumsum(counts) - counts
    pends = jnp.cumsum(padded)
    pstarts = pends - padded
    dest = pstarts[sorted_e] + jnp.arange(m) - starts[sorted_e]
    n_blocks = -(-m // MOE_BLOCK) + N_EXPERTS
    p = n_blocks * MOE_BLOCK
    slot_tok = jnp.full((p,), n_tok, dtype=jnp.int32).at[dest].set(flat_tok[order])
    slot_gate = jnp.zeros((p,), flat_g.dtype).at[dest].set(flat_g[order])
    block_start = jnp.arange(n_blocks) * MOE_BLOCK
    block_e = jnp.minimum(jnp.searchsorted(pends, block_start, side="right"), N_EXPERTS - 1)
    x_pad = jnp.concatenate([xt, jnp.zeros((1, d), xt.dtype)], axis=0)

    def run_block(args):
        e, toks, g = args
        xb = x_pad[toks]
        hb = jax.nn.silu(xb @ w_gate[e]) * (xb @ w_up[e])
        return (hb @ w_down[e]) * g[:, None].astype(xb.dtype)

    yb = lax.map(run_block, (block_e, slot_tok.reshape(n_blocks, MOE_BLOCK),
                             slot_gate.reshape(n_blocks, MOE_BLOCK)))
    out = jax.ops.segment_sum(yb.reshape(p, d), slot_tok, num_segments=n_tok + 1)
    return out[:n_tok]


def hierarchical_moe(h, group_w, group_b, expert_router_w, expert_router_b, w_gate, w_up, w_down):
    b, s, d = h.shape
    n_tok = b * s
    xt = h.reshape(n_tok, d)
    g_logits = (xt @ group_w).astype(jnp.float32) + group_b.astype(jnp.float32)
    g_prob = jax.nn.softmax(g_logits, axis=-1)
    g_top_p, g_idx = lax.top_k(g_prob, 1)
    e_logits = (xt @ expert_router_w).astype(jnp.float32) + expert_router_b.astype(jnp.float32)
    e_logits = e_logits.reshape(n_tok, N_GROUPS, EXPERTS_PER_GROUP)
    e_logits = jnp.take_along_axis(e_logits, g_idx[:, :, None], axis=1)[:, 0]
    e_prob = jax.nn.softmax(e_logits, axis=-1)
    e_top_p, e_idx = lax.top_k(e_prob, TOP_K)
    e_top_p = e_top_p / jnp.sum(e_top_p, axis=-1, keepdims=True)
    gates = g_top_p * e_top_p
    expert_ids = g_idx * EXPERTS_PER_GROUP + e_idx
    y = routed_experts(xt, expert_ids, gates, w_gate, w_up, w_down)
    return y.reshape(b, s, d)


def setup_inputs(seed: int = 0) -> dict:
    key = jax.random.key(seed)
    ks = jax.random.split(key, 26)
    f32 = jnp.float32
    d = D_MODEL

    def nrm(k, shape, scale):
        return jax.random.normal(k, shape, f32) * scale

    def gain(k, shape):
        return 1.0 + 0.05 * jax.random.normal(k, shape, f32)

    return {
        "x": nrm(ks[0], (BATCH, SEQ, d), 1.0),
        "even_norm": gain(ks[1], (N_EVEN, d)),
        "even_w_in": nrm(ks[2], (N_EVEN, d, EVEN_IN), d ** -0.5),
        "conv_w": nrm(ks[3], (N_EVEN, CONV_WIDTH, CONV_CH), CONV_WIDTH ** -0.5),
        "conv_b": nrm(ks[4], (N_EVEN, CONV_CH), 0.02),
        "conv_norm_g": gain(ks[5], (N_EVEN, CONV_CH)),
        "conv_norm_b": nrm(ks[6], (N_EVEN, CONV_CH), 0.02),
        "sb_q_norm": gain(ks[7], (N_EVEN, HEAD_DIM)),
        "sb_k_norm": gain(ks[8], (N_EVEN, HEAD_DIM)),
        "even_w_out": nrm(ks[9], (N_EVEN, EVEN_MIX, d), EVEN_MIX ** -0.5),
        "odd_norm": gain(ks[10], (N_ODD, d)),
        "odd_w_in": nrm(ks[11], (N_ODD, d, ODD_IN), d ** -0.5),
        "fox_forget_b": jax.random.uniform(ks[12], (N_ODD, FOX_HEADS), f32, 1.0, 5.0),
        "fox_q_norm": gain(ks[13], (N_ODD, HEAD_DIM)),
        "fox_k_norm": gain(ks[14], (N_ODD, HEAD_DIM)),
        "sc_w": nrm(ks[15], (N_ODD, SC_WIDTH, SC_CH), SC_WIDTH ** -0.5),
        "odd_w_out": nrm(ks[16], (N_ODD, ODD_MIX, d), ODD_MIX ** -0.5),
        "moe_norm": gain(ks[17], (DEPTH, d)),
        "router_group_w": nrm(ks[18], (DEPTH, d, N_GROUPS), d ** -0.5),
        "router_group_b": nrm(ks[19], (DEPTH, N_GROUPS), 0.01),
        "router_expert_w": nrm(ks[20], (DEPTH, d, N_EXPERTS), d ** -0.5),
        "router_expert_b": nrm(ks[21], (DEPTH, N_EXPERTS), 0.01),
        "expert_w_gate": nrm(ks[22], (DEPTH, N_EXPERTS, d, EXPERT_FF), d ** -0.5),
        "expert_w_up": nrm(ks[23], (DEPTH, N_EXPERTS, d, EXPERT_FF), d ** -0.5),
        "expert_w_down": nrm(ks[24], (DEPTH, N_EXPERTS, EXPERT_FF, d), EXPERT_FF ** -0.5),
    }


def reference(x, even_norm, even_w_in, conv_w, conv_b, conv_norm_g, conv_norm_b,
              sb_q_norm, sb_k_norm, even_w_out, odd_norm, odd_w_in, fox_forget_b,
              fox_q_norm, fox_k_norm, sc_w, odd_w_out, moe_norm, router_group_w,
              router_group_b, router_expert_w, router_expert_b, expert_w_gate,
              expert_w_up, expert_w_down):
    h = x
    for layer in range(DEPTH):
        i = layer // 2
        if layer % 2 == 0:
            h = h + even_mixer(rms_norm(h, even_norm[i]), even_w_in[i], conv_w[i], conv_b[i],
                               conv_norm_g[i], conv_norm_b[i], sb_q_norm[i], sb_k_norm[i],
                               even_w_out[i])
        else:
            h = h + odd_mixer(rms_norm(h, odd_norm[i]), odd_w_in[i], fox_forget_b[i],
                              fox_q_norm[i], fox_k_norm[i], sc_w[i], odd_w_out[i])
        h = h + hierarchical_moe(rms_norm(h, moe_norm[layer]), router_group_w[layer],
                                 router_group_b[layer], router_expert_w[layer],
                                 router_expert_b[layer], expert_w_gate[layer],
                                 expert_w_up[layer], expert_w_down[layer])
    return h
```

```python
import functools

import jax
import jax.numpy as jnp
from jax import lax
from jax.experimental import pallas as pl
from jax.experimental.pallas import tpu as pltpu

F32 = jnp.float32
BF16 = jnp.bfloat16

HEAD_DIM = 128
CONV_CH = 1024
CONV_WIDTH = 31
SB_HEADS = 8
FOX_HEADS = 8
SC_CH = 1024
SC_WIDTH = 3
N_GROUPS = 4
EXPERTS_PER_GROUP = 8
N_EXPERTS = N_GROUPS * EXPERTS_PER_GROUP
MOE_BLOCK = 256
RMS_EPS = 1e-6
LN_EPS = 1e-5

LANES = 128
SUBLANES = 8
VMEM_LIMIT = 56 * 1024 * 1024


def _cparams(*sem):
    return pltpu.CompilerParams(dimension_semantics=sem, vmem_limit_bytes=VMEM_LIMIT)


def _sigmoid(x):
    return 1.0 / (1.0 + jnp.exp(-x))


def _softplus_neg_abs(z):
    return jnp.log(1.0 + jnp.exp(-jnp.abs(z)))


def _rms(x, g):
    ms = jnp.mean(x * x, axis=-1, keepdims=True)
    return x * lax.rsqrt(ms + RMS_EPS) * g


def _split_bf16(x, parts):
    out = []
    r = x
    for _ in range(parts - 1):
        p = r.astype(BF16)
        out.append(p)
        r = r - p.astype(F32)
    out.append(r.astype(BF16))
    return out


def _norm_matmul_kernel(x_ref, g_ref, w_ref, o_ref, xn_ref):
    @pl.when(pl.program_id(1) == 0)
    def _():
        xn_ref[...] = _rms(x_ref[...], g_ref[...]).astype(BF16)

    o_ref[...] = jnp.dot(xn_ref[...], w_ref[...], preferred_element_type=F32)


def norm_matmul(x, gain, w, *, tm, tn):
    n, d = x.shape
    f = w.shape[1]
    return pl.pallas_call(
        _norm_matmul_kernel,
        grid=(n // tm, f // tn),
        in_specs=[pl.BlockSpec((tm, d), lambda i, j: (i, 0)),
                  pl.BlockSpec((1, d), lambda i, j: (0, 0)),
                  pl.BlockSpec((d, tn), lambda i, j: (0, j))],
        out_specs=pl.BlockSpec((tm, tn), lambda i, j: (i, j)),
        out_shape=jax.ShapeDtypeStruct((n, f), F32),
        scratch_shapes=[pltpu.VMEM((tm, d), BF16)],
        compiler_params=_cparams("parallel", "arbitrary"),
        name="norm_matmul",
    )(x, gain.reshape(1, d), w)


def _conv_tile(abuf, w_ref, bias_ref, cbuf, *, ts, halo, width, rows):
    n_ch = cbuf.shape[1]
    off = halo - (width - 1)

    def chan_body(c, carry):
        lanes = pl.ds(pl.multiple_of(c * LANES, LANES), LANES)
        wc = w_ref[:, lanes]
        for r0 in range(0, ts, rows):
            if bias_ref is None:
                acc = jnp.zeros((rows, LANES), F32)
            else:
                acc = jnp.broadcast_to(bias_ref[:, lanes], (rows, LANES))
            for k in range(width):
                acc = acc + wc[k:k + 1, :] * abuf[pl.ds(r0 + off + k, rows), lanes]
            cbuf[pl.ds(r0, rows), lanes] = acc
        return carry

    lax.fori_loop(0, n_ch // LANES, chan_body, 0)


def _carry_halo(abuf, *, ts, halo):
    s = pl.program_id(1)

    @pl.when(s == 0)
    def _():
        abuf[0:halo, :] = jnp.zeros((halo, abuf.shape[1]), F32)

    @pl.when(s > 0)
    def _():
        abuf[0:halo, :] = abuf[ts:ts + halo, :]


CONV_HALO = 32
SC_HALO = 8


def _conformer_kernel(av_ref, ag_ref, w_ref, cb_ref, lg_ref, lb_ref, o_ref, abuf, cbuf, *, ts):
    _carry_halo(abuf, ts=ts, halo=CONV_HALO)
    abuf[CONV_HALO:CONV_HALO + ts, :] = av_ref[...] * _sigmoid(ag_ref[...])
    _conv_tile(abuf, w_ref, cb_ref, cbuf, ts=ts, halo=CONV_HALO, width=CONV_WIDTH, rows=64)
    y = cbuf[...]
    mu = jnp.mean(y, axis=-1, keepdims=True)
    yc = y - mu
    var = jnp.mean(yc * yc, axis=-1, keepdims=True)
    yn = yc * lax.rsqrt(var + LN_EPS) * lg_ref[...] + lb_ref[...]
    o_ref[...] = (yn * _sigmoid(yn)).astype(BF16)


def conformer_conv(u3, conv_w, conv_b, ln_g, ln_b, *, ts):
    b, s, _ = u3.shape
    c = conv_w.shape[1]
    w_pad = jnp.zeros((CONV_HALO, c), F32).at[:CONV_WIDTH].set(conv_w)
    row = lambda a: a.reshape(1, c)
    full = lambda shape: pl.BlockSpec(shape, lambda bi, si: (0, 0))
    return pl.pallas_call(
        functools.partial(_conformer_kernel, ts=ts),
        grid=(b, s // ts),
        in_specs=[pl.BlockSpec((None, ts, c), lambda bi, si: (bi, si, 0)),
                  pl.BlockSpec((None, ts, c), lambda bi, si: (bi, si, 1)),
                  full((CONV_HALO, c)), full((1, c)), full((1, c)), full((1, c))],
        out_specs=pl.BlockSpec((None, ts, c), lambda bi, si: (bi, si, 0)),
        out_shape=jax.ShapeDtypeStruct((b, s, c), BF16),
        scratch_shapes=[pltpu.VMEM((CONV_HALO + ts, c), F32), pltpu.VMEM((ts, c), F32)],
        compiler_params=_cparams("parallel", "arbitrary"),
        name="conformer_conv",
    )(u3, u3, w_pad, row(conv_b), row(ln_g), row(ln_b))


def _short_conv_kernel(bg_ref, cg_ref, xv_ref, w_ref, o_ref, abuf, cbuf, *, ts):
    _carry_halo(abuf, ts=ts, halo=SC_HALO)
    abuf[SC_HALO:SC_HALO + ts, :] = cg_ref[...] * xv_ref[...]
    _conv_tile(abuf, w_ref, None, cbuf, ts=ts, halo=SC_HALO, width=SC_WIDTH, rows=64)
    o_ref[...] = (bg_ref[...] * cbuf[...]).astype(BF16)


def short_conv(u3, sc_w, *, col0, ts):
    b, s, _ = u3.shape
    c = sc_w.shape[1]
    j0 = col0 // c
    w_pad = jnp.zeros((SUBLANES, c), F32).at[:SC_WIDTH].set(sc_w)
    return pl.pallas_call(
        functools.partial(_short_conv_kernel, ts=ts),
        grid=(b, s // ts),
        in_specs=[pl.BlockSpec((None, ts, c), lambda bi, si: (bi, si, j0)),
                  pl.BlockSpec((None, ts, c), lambda bi, si: (bi, si, j0 + 1)),
                  pl.BlockSpec((None, ts, c), lambda bi, si: (bi, si, j0 + 2)),
                  pl.BlockSpec((SUBLANES, c), lambda bi, si: (0, 0))],
        out_specs=pl.BlockSpec((None, ts, c), lambda bi, si: (bi, si, 0)),
        out_shape=jax.ShapeDtypeStruct((b, s, c), BF16),
        scratch_shapes=[pltpu.VMEM((SC_HALO + ts, c), F32), pltpu.VMEM((ts, c), F32)],
        compiler_params=_cparams("parallel", "arbitrary"),
        name="short_conv",
    )(u3, u3, u3, w_pad)


def _prep_kv(k_ref, v_ref, kg_ref, kn_ref, vb_ref):
    @pl.when(pl.program_id(2) == 0)
    def _():
        kn_ref[...] = _rms(k_ref[...], kg_ref[...]).astype(BF16)
        vb_ref[...] = v_ref[...].astype(BF16)


def _qk(qn, kb):
    return lax.dot_general(qn, kb, (((1,), (1,)), ((), ())), preferred_element_type=F32)


SB_TQ = 256
SB_TK = 128


def _sb_kernel(q_ref, k_ref, v_ref, qg_ref, kg_ref, o_ref, kn_ref, vb_ref, acc_ref, r_ref):
    tq, tk = SB_TQ, SB_TK
    i = pl.program_id(2)
    _prep_kv(k_ref, v_ref, kg_ref, kn_ref, vb_ref)
    qn = (_rms(q_ref[...], qg_ref[...]) * (HEAD_DIM ** -0.5)).astype(BF16)
    acc_ref[...] = jnp.zeros((tq, HEAD_DIM), F32)
    r_ref[...] = jnp.zeros((tq, LANES), F32)

    jr = lax.broadcasted_iota(jnp.int32, (tk, 2 * tk), 0)
    sc = lax.broadcasted_iota(jnp.int32, (tk, 2 * tk), 1)
    u2 = jnp.where((jr > sc) | (sc >= tk), 1.0, 0.0).astype(BF16)
    qpos = i * tq + lax.broadcasted_iota(jnp.int32, (tq, tk), 0)
    kloc = lax.broadcasted_iota(jnp.int32, (tq, tk), 1)

    def step(jb, masked):
        ks = pl.multiple_of(jb * tk, tk)
        z = _qk(qn, kn_ref[pl.ds(ks, tk), :])
        log_beta = jnp.minimum(z, 0.0) - _softplus_neg_abs(z)
        log_om = log_beta - z
        if masked:
            strict = (ks + kloc) < qpos
            log_om = jnp.where(strict, log_om, 0.0)
        hi, lo = _split_bf16(log_om, 2)
        cs = (jnp.dot(hi, u2, preferred_element_type=F32)
              + jnp.dot(lo, u2, preferred_element_type=F32))
        w = jnp.exp(log_beta + cs[:, :tk] + r_ref[...])
        if masked:
            w = jnp.where(strict, w, 0.0)
        acc_ref[...] += jnp.dot(w.astype(BF16), vb_ref[pl.ds(ks, tk), :],
                                preferred_element_type=F32)
        r_ref[...] += cs[:, tk:]

    nd = tq // tk
    for d in range(nd):
        step(i * nd + (nd - 1 - d), True)

    def body(n, carry):
        step(i * nd - 1 - n, False)
        return carry

    lax.fori_loop(0, i * nd, body, 0)
    o_ref[...] = acc_ref[...].astype(BF16)


def stick_breaking_attention(u3, q_gain, k_gain, *, col0, heads):
    b, s, _ = u3.shape
    j0 = col0 // HEAD_DIM
    gain = lambda g: g.reshape(1, HEAD_DIM)
    full = lambda shape: pl.BlockSpec(shape, lambda bi, hi, qi: (0, 0))
    return pl.pallas_call(
        _sb_kernel,
        grid=(b, heads, s // SB_TQ),
        in_specs=[pl.BlockSpec((None, SB_TQ, HEAD_DIM), lambda bi, hi, qi: (bi, qi, j0 + hi)),
                  pl.BlockSpec((None, s, HEAD_DIM), lambda bi, hi, qi: (bi, 0, j0 + heads + hi)),
                  pl.BlockSpec((None, s, HEAD_DIM), lambda bi, hi, qi: (bi, 0, j0 + 2 * heads + hi)),
                  full((1, HEAD_DIM)), full((1, HEAD_DIM))],
        out_specs=pl.BlockSpec((None, SB_TQ, HEAD_DIM), lambda bi, hi, qi: (bi, qi, hi)),
        out_shape=jax.ShapeDtypeStruct((b, s, heads * HEAD_DIM), BF16),
        scratch_shapes=[pltpu.VMEM((s, HEAD_DIM), BF16), pltpu.VMEM((s, HEAD_DIM), BF16),
                        pltpu.VMEM((SB_TQ, HEAD_DIM), F32), pltpu.VMEM((SB_TQ, LANES), F32)],
        compiler_params=_cparams("parallel", "parallel", "arbitrary"),
        name="stick_breaking_attention",
    )(u3, u3, u3, gain(q_gain), gain(k_gain))


FOX_T = 256


def _fox_kernel(q_ref, k_ref, v_ref, cc_ref, cr_ref, qg_ref, kg_ref, o_ref,
                kn_ref, vb_ref, m_ref, l_ref, acc_ref):
    t = FOX_T
    h = pl.program_id(1)
    i = pl.program_id(2)
    _prep_kv(k_ref, v_ref, kg_ref, kn_ref, vb_ref)
    qn = (_rms(q_ref[...], qg_ref[...]) * (HEAD_DIM ** -0.5)).astype(BF16)
    lane = lax.broadcasted_iota(jnp.int32, (t, LANES), 1)
    cq = jnp.sum(jnp.where(lane == h, cc_ref[...], 0.0), axis=1, keepdims=True)
    m_ref[...] = jnp.full((t, LANES), -jnp.inf, F32)
    l_ref[...] = jnp.zeros((t, LANES), F32)
    acc_ref[...] = jnp.zeros((t, HEAD_DIM), F32)
    qpos = lax.broadcasted_iota(jnp.int32, (t, t), 0)
    kpos = lax.broadcasted_iota(jnp.int32, (t, t), 1)

    def step(j, masked):
        ks = pl.multiple_of(j * t, t)
        z = _qk(qn, kn_ref[pl.ds(ks, t), :]) + (cq - cr_ref[:, pl.ds(ks, t)])
        if masked:
            z = jnp.where(kpos <= qpos, z, -jnp.inf)
        m_prev = m_ref[...]
        m_new = jnp.maximum(m_prev, jnp.max(z, axis=1, keepdims=True))
        alpha = jnp.exp(m_prev - m_new)
        p = jnp.exp(z - m_new[:, 0:1])
        l_ref[...] = alpha * l_ref[...] + jnp.sum(p, axis=1, keepdims=True)
        acc_ref[...] = alpha * acc_ref[...] + jnp.dot(
            p.astype(BF16), vb_ref[pl.ds(ks, t), :], preferred_element_type=F32)
        m_ref[...] = m_new

    def body(j, carry):
        step(j, False)
        return carry

    lax.fori_loop(0, i, body, 0)
    step(i, True)
    o_ref[...] = (acc_ref[...] / l_ref[...]).astype(BF16)


def forgetting_attention(u3, c_col, c_row, q_gain, k_gain, *, heads):
    b, s, _ = u3.shape
    gain = lambda g: g.reshape(1, HEAD_DIM)
    full = lambda shape: pl.BlockSpec(shape, lambda bi, hi, qi: (0, 0))
    return pl.pallas_call(
        _fox_kernel,
        grid=(b, heads, s // FOX_T),
        in_specs=[pl.BlockSpec((None, FOX_T, HEAD_DIM), lambda bi, hi, qi: (bi, qi, hi)),
                  pl.BlockSpec((None, s, HEAD_DIM), lambda bi, hi, qi: (bi, 0, heads + hi)),
                  pl.BlockSpec((None, s, HEAD_DIM), lambda bi, hi, qi: (bi, 0, 2 * heads + hi)),
                  pl.BlockSpec((None, FOX_T, LANES), lambda bi, hi, qi: (bi, qi, 0)),
                  pl.BlockSpec((None, None, 1, s), lambda bi, hi, qi: (bi, hi, 0, 0)),
                  full((1, HEAD_DIM)), full((1, HEAD_DIM))],
        out_specs=pl.BlockSpec((None, FOX_T, HEAD_DIM), lambda bi, hi, qi: (bi, qi, hi)),
        out_shape=jax.ShapeDtypeStruct((b, s, heads * HEAD_DIM), BF16),
        scratch_shapes=[pltpu.VMEM((s, HEAD_DIM), BF16), pltpu.VMEM((s, HEAD_DIM), BF16),
                        pltpu.VMEM((FOX_T, LANES), F32), pltpu.VMEM((FOX_T, LANES), F32),
                        pltpu.VMEM((FOX_T, HEAD_DIM), F32)],
        compiler_params=_cparams("parallel", "parallel", "arbitrary"),
        name="forgetting_attention",
    )(u3, u3, u3, c_col, c_row, gain(q_gain), gain(k_gain))


def _forget_cumsum_kernel(f_ref, b_ref, o_ref, carry_ref, *, tc):
    @pl.when(pl.program_id(1) == 0)
    def _():
        carry_ref[...] = jnp.zeros((1, LANES), F32)

    x = f_ref[...] + b_ref[...]
    log_f = jnp.minimum(x, 0.0) - _softplus_neg_abs(x)
    r = lax.broadcasted_iota(jnp.int32, (tc, tc), 0)
    c = lax.broadcasted_iota(jnp.int32, (tc, tc), 1)
    tri = jnp.where(r >= c, 1.0, 0.0).astype(BF16)
    cs = carry_ref[...]
    for part in _split_bf16(log_f, 3):
        cs = cs + jnp.dot(tri, part, preferred_element_type=F32)
    o_ref[...] = cs
    carry_ref[...] = cs[tc - 1:tc, :]


def forget_cumsum(u3, forget_b, *, col0, tc):
    b, s, _ = u3.shape
    j0 = col0 // LANES
    b_pad = jnp.zeros((1, LANES), F32).at[0, :forget_b.shape[0]].set(forget_b)
    return pl.pallas_call(
        functools.partial(_forget_cumsum_kernel, tc=tc),
        grid=(b, s // tc),
        in_specs=[pl.BlockSpec((None, tc, LANES), lambda bi, si: (bi, si, j0)),
                  pl.BlockSpec((1, LANES), lambda bi, si: (0, 0))],
        out_specs=pl.BlockSpec((None, tc, LANES), lambda bi, si: (bi, si, 0)),
        out_shape=jax.ShapeDtypeStruct((b, s, LANES), F32),
        scratch_shapes=[pltpu.VMEM((1, LANES), F32)],
        compiler_params=_cparams("parallel", "arbitrary"),
        name="forget_cumsum",
    )(u3, b_pad)


def _out_proj_kernel(a_ref, b_ref, w1_ref, w2_ref, h_ref, o_ref):
    o_ref[...] = (h_ref[...]
                  + jnp.dot(a_ref[...], w1_ref[...], preferred_element_type=F32)
                  + jnp.dot(b_ref[...], w2_ref[...], preferred_element_type=F32))


def out_proj_residual(a, b, w, h, *, tm):
    n, ka = a.shape
    kb = b.shape[1]
    d = w.shape[1]
    return pl.pallas_call(
        _out_proj_kernel,
        grid=(n // tm,),
        in_specs=[pl.BlockSpec((tm, ka), lambda i: (i, 0)),
                  pl.BlockSpec((tm, kb), lambda i: (i, 0)),
                  pl.BlockSpec((ka, d), lambda i: (0, 0)),
                  pl.BlockSpec((kb, d), lambda i: (0, 0)),
                  pl.BlockSpec((tm, d), lambda i: (i, 0))],
        out_specs=pl.BlockSpec((tm, d), lambda i: (i, 0)),
        out_shape=jax.ShapeDtypeStruct((n, d), F32),
        compiler_params=_cparams("parallel"),
        name="out_proj_residual",
    )(a, b, w[:ka], w[ka:], h)


INFO_E, INFO_GATE, INFO_RANK = 0, 2, 4


def _to_row_tiles(dst_ref, x, n_rows):
    chunks = x.shape[1] // LANES
    for c in range(chunks):
        dst_ref[pl.ds(c, n_rows, stride=chunks), :] = x[:, c * LANES:(c + 1) * LANES]


def _from_row_tiles(src_ref, n_rows, chunks):
    return jnp.concatenate(
        [src_ref[pl.ds(c, n_rows, stride=chunks), :] for c in range(chunks)], axis=1)


def _lane_pick(x, lane, idx):
    return jnp.sum(jnp.where(lane == idx, x, 0.0), axis=1, keepdims=True)


def _router_kernel(h_ref, g_ref, wh_ref, wl_ref, b_ref, xn_ref, info_ref, cnt_ref, carry_ref, *, tm):
    @pl.when(pl.program_id(0) == 0)
    def _():
        carry_ref[...] = jnp.zeros((1, LANES), F32)

    xn = _rms(h_ref[...], g_ref[...])
    _to_row_tiles(xn_ref, xn, tm)

    xh, xl = _split_bf16(xn, 2)
    wh = wh_ref[...]
    logits = (jnp.dot(xh, wh, preferred_element_type=F32)
              + jnp.dot(xl, wh, preferred_element_type=F32)
              + jnp.dot(xh, wl_ref[...], preferred_element_type=F32)) + b_ref[...]

    lane = lax.broadcasted_iota(jnp.int32, (tm, LANES), 1).astype(F32)
    neg = -jnp.inf
    big = float(LANES)
    gl = jnp.where(lane < N_GROUPS, logits, neg)
    gmax = jnp.max(gl, axis=1, keepdims=True)
    g_top_p = 1.0 / jnp.sum(jnp.exp(gl - gmax), axis=1, keepdims=True)
    g_idx = jnp.min(jnp.where(gl == gmax, lane, big), axis=1, keepdims=True)

    lo = N_GROUPS + EXPERTS_PER_GROUP * g_idx
    el = jnp.where((lane >= lo) & (lane < lo + EXPERTS_PER_GROUP), logits, neg)
    m1 = jnp.max(el, axis=1, keepdims=True)
    i1 = jnp.min(jnp.where(el == m1, lane, big), axis=1, keepdims=True)
    el2 = jnp.where(lane == i1, neg, el)
    m2 = jnp.max(el2, axis=1, keepdims=True)
    i2 = jnp.min(jnp.where(el2 == m2, lane, big), axis=1, keepdims=True)
    ratio = jnp.exp(m2 - m1)
    p1 = 1.0 / (1.0 + ratio)
    p2 = ratio * p1
    e1 = i1 - N_GROUPS
    e2 = i2 - N_GROUPS

    onehot = jnp.where((lane == e1) | (lane == e2), 1.0, 0.0)
    r = lax.broadcasted_iota(jnp.int32, (tm, tm), 0)
    c = lax.broadcasted_iota(jnp.int32, (tm, tm), 1)
    before = jnp.where(r > c, 1.0, 0.0).astype(BF16)
    cnt = jnp.dot(before, onehot.astype(BF16), preferred_element_type=F32) + carry_ref[...]
    rank1 = _lane_pick(cnt, lane, e1)
    rank2 = _lane_pick(cnt, lane, e2)
    total = carry_ref[...] + jnp.sum(onehot, axis=0, keepdims=True)
    carry_ref[...] = total
    cnt_ref[...] = jnp.broadcast_to(total, (SUBLANES, LANES))

    info = jnp.zeros((tm, LANES), F32)
    for k, val in ((INFO_E, e1), (INFO_E + 1, e2), (INFO_GATE, g_top_p * p1),
                   (INFO_GATE + 1, g_top_p * p2), (INFO_RANK, rank1), (INFO_RANK + 1, rank2)):
        info = jnp.where(lane == k, val, info)
    info_ref[...] = info


def moe_router(h, gain, group_w, group_b, expert_w, expert_b, *, tm):
    n, d = h.shape
    chunks = d // LANES
    n_logits = N_GROUPS + N_EXPERTS
    w = jnp.zeros((d, LANES), F32).at[:, :N_GROUPS].set(group_w).at[:, N_GROUPS:n_logits].set(expert_w)
    bias = jnp.zeros((1, LANES), F32).at[0, :N_GROUPS].set(group_b).at[0, N_GROUPS:n_logits].set(expert_b)
    w_hi = w.astype(BF16)
    w_lo = (w - w_hi.astype(F32)).astype(BF16)
    full = lambda shape: pl.BlockSpec(shape, lambda i: (0, 0))
    return pl.pallas_call(
        functools.partial(_router_kernel, tm=tm),
        grid=(n // tm,),
        in_specs=[pl.BlockSpec((tm, d), lambda i: (i, 0)), full((1, d)),
                  full((d, LANES)), full((d, LANES)), full((1, LANES))],
        out_specs=[pl.BlockSpec((tm * chunks, LANES), lambda i: (i, 0)),
                   pl.BlockSpec((tm, LANES), lambda i: (i, 0)),
                   full((SUBLANES, LANES))],
        out_shape=[jax.ShapeDtypeStruct((n * chunks, LANES), F32),
                   jax.ShapeDtypeStruct((n, LANES), F32),
                   jax.ShapeDtypeStruct((SUBLANES, LANES), F32)],
        scratch_shapes=[pltpu.VMEM((1, LANES), F32)],
        compiler_params=_cparams("arbitrary"),
        name="moe_router",
    )(h, gain.reshape(1, d), w_hi, w_lo, bias)


def _row_copy(src, dst, sem):
    return pltpu.make_async_copy(src, dst, sem)


def _dispatch_kernel(zstart_ref, nu_ref, dest_ref, xn_ref, xs_ref, zero_ref, sem, *, tm, n_experts):
    tile = pl.program_id(0)
    n_blocks = xs_ref.shape[0] // MOE_BLOCK

    @pl.when(tile == 0)
    def _():
        zero_ref[...] = jnp.zeros(zero_ref.shape, F32)
        pad = lambda start: _row_copy(zero_ref, xs_ref.at[pl.ds(start, MOE_BLOCK)], sem)
        for e in range(n_experts):
            pad(zstart_ref[e]).start()
        for e in range(n_experts):
            pad(0).wait()

        def tail(i, carry):
            cp = pad(pl.multiple_of(i * MOE_BLOCK, MOE_BLOCK))
            cp.start()
            cp.wait()
            return carry

        lax.fori_loop(nu_ref[0], n_blocks, tail, 0)

    def issue(t, carry):
        src = xn_ref.at[tile * tm + t]
        _row_copy(src, xs_ref.at[dest_ref[0, 2 * t]], sem).start()
        _row_copy(src, xs_ref.at[dest_ref[0, 2 * t + 1]], sem).start()
        return carry

    lax.fori_loop(0, tm, issue, 0)

    def drain(t, carry):
        _row_copy(xn_ref.at[0], xs_ref.at[0], sem).wait()
        _row_copy(xn_ref.at[0], xs_ref.at[0], sem).wait()
        return carry

    lax.fori_loop(0, tm, drain, 0)


def moe_dispatch(xn_rows, dest, zstart, n_used, *, n_slots, tm):
    n, chunks, _ = xn_rows.shape
    dest3 = dest.reshape(n // tm, 1, 2 * tm)
    return pl.pallas_call(
        functools.partial(_dispatch_kernel, tm=tm, n_experts=zstart.shape[0]),
        grid_spec=pltpu.PrefetchScalarGridSpec(
            num_scalar_prefetch=2,
            grid=(n // tm,),
            in_specs=[pl.BlockSpec((None, 1, 2 * tm), lambda i, z, nu: (i, 0, 0),
                                   memory_space=pltpu.SMEM),
                      pl.BlockSpec(memory_space=pl.ANY)],
            out_specs=pl.BlockSpec(memory_space=pl.ANY),
            scratch_shapes=[pltpu.VMEM((MOE_BLOCK, chunks, LANES), F32),
                            pltpu.SemaphoreType.DMA(())]),
        out_shape=jax.ShapeDtypeStruct((n_slots, chunks, LANES), F32),
        compiler_params=_cparams("arbitrary"),
        name="moe_dispatch",
    )(zstart, n_used, dest3, xn_rows)


def _expert_kernel(be_ref, nu_ref, x_ref, wg_ref, wu_ref, wd_ref, y_ref, *, chunks):
    i = pl.program_id(0)

    @pl.when(i < nu_ref[0])
    def _():
        x = _from_row_tiles(x_ref, MOE_BLOCK, chunks).astype(BF16)
        gate = jnp.dot(x, wg_ref[...], preferred_element_type=F32)
        up = jnp.dot(x, wu_ref[...], preferred_element_type=F32)
        hidden = (gate * _sigmoid(gate) * up).astype(BF16)
        y = jnp.dot(hidden, wd_ref[...], preferred_element_type=F32)
        _to_row_tiles(y_ref, y, MOE_BLOCK)

    @pl.when(i >= nu_ref[0])
    def _():
        y_ref[...] = jnp.zeros(y_ref.shape, F32)


def moe_experts(xs_flat, block_e, n_used, w_gate, w_up, w_down, *, chunks):
    rows = xs_flat.shape[0]
    n_blocks = rows // (MOE_BLOCK * chunks)
    _, d, ff = w_gate.shape
    x_map = lambda i, be, nu: (jnp.minimum(i, nu[0] - 1), 0)
    return pl.pallas_call(
        functools.partial(_expert_kernel, chunks=chunks),
        grid_spec=pltpu.PrefetchScalarGridSpec(
            num_scalar_prefetch=2,
            grid=(n_blocks,),
            in_specs=[pl.BlockSpec((MOE_BLOCK * chunks, LANES), x_map),
                      pl.BlockSpec((None, d, ff), lambda i, be, nu: (be[i], 0, 0)),
                      pl.BlockSpec((None, d, ff), lambda i, be, nu: (be[i], 0, 0)),
                      pl.BlockSpec((None, ff, d), lambda i, be, nu: (be[i], 0, 0))],
            out_specs=pl.BlockSpec((MOE_BLOCK * chunks, LANES), lambda i, be, nu: (i, 0))),
        out_shape=jax.ShapeDtypeStruct((rows, LANES), F32),
        compiler_params=_cparams("arbitrary"),
        name="moe_experts",
    )(block_e, n_used, xs_flat, w_gate, w_up, w_down)


def _combine_kernel(dest_ref, h_ref, info_ref, y_ref, o_ref, buf1, buf2, sem, *, tm, chunks):
    def issue(t, carry):
        rows = pl.ds(pl.multiple_of(t * chunks, chunks), chunks)
        _row_copy(y_ref.at[dest_ref[0, 2 * t]], buf1.at[rows], sem).start()
        _row_copy(y_ref.at[dest_ref[0, 2 * t + 1]], buf2.at[rows], sem).start()
        return carry

    lax.fori_loop(0, tm, issue, 0)

    def drain(t, carry):
        rows = pl.ds(0, chunks)
        _row_copy(y_ref.at[0], buf1.at[rows], sem).wait()
        _row_copy(y_ref.at[0], buf2.at[rows], sem).wait()
        return carry

    lax.fori_loop(0, tm, drain, 0)

    info = info_ref[...]
    lane = lax.broadcasted_iota(jnp.int32, (tm, LANES), 1)
    g1 = jnp.sum(jnp.where(lane == INFO_GATE, info, 0.0), axis=1, keepdims=True)
    g2 = jnp.sum(jnp.where(lane == INFO_GATE + 1, info, 0.0), axis=1, keepdims=True)
    o_ref[...] = (h_ref[...] + g1 * _from_row_tiles(buf1, tm, chunks)
                  + g2 * _from_row_tiles(buf2, tm, chunks))


def moe_combine(h, info, y_rows, dest, *, tm):
    n, d = h.shape
    chunks = d // LANES
    dest3 = dest.reshape(n // tm, 1, 2 * tm)
    return pl.pallas_call(
        functools.partial(_combine_kernel, tm=tm, chunks=chunks),
        grid=(n // tm,),
        in_specs=[pl.BlockSpec((None, 1, 2 * tm), lambda i: (i, 0, 0), memory_space=pltpu.SMEM),
                  pl.BlockSpec((tm, d), lambda i: (i, 0)),
                  pl.BlockSpec((tm, LANES), lambda i: (i, 0)),
                  pl.BlockSpec(memory_space=pl.ANY)],
        out_specs=pl.BlockSpec((tm, d), lambda i: (i, 0)),
        out_shape=jax.ShapeDtypeStruct((n, d), F32),
        scratch_shapes=[pltpu.VMEM((tm * chunks, LANES), F32), pltpu.VMEM((tm * chunks, LANES), F32),
                        pltpu.SemaphoreType.DMA(())],
        compiler_params=_cparams("arbitrary"),
        name="moe_combine",
    )(dest3, h, info, y_rows)


def hierarchical_moe(h, gain, group_w, group_b, expert_w, expert_b, w_gate, w_up, w_down,
                     *, router_tm, dispatch_tm, combine_tm):
    n, d = h.shape
    chunks = d // LANES
    xn_flat, info, cnt = moe_router(h, gain, group_w, group_b, expert_w, expert_b, tm=router_tm)

    n_experts = w_gate.shape[0]
    n_blocks = -(-2 * n // MOE_BLOCK) + n_experts
    n_slots = n_blocks * MOE_BLOCK
    experts = info[:, INFO_E:INFO_E + 2].astype(jnp.int32)
    ranks = info[:, INFO_RANK:INFO_RANK + 2].astype(jnp.int32)
    counts = cnt[0, :n_experts].astype(jnp.int32)
    padded = (counts + MOE_BLOCK - 1) // MOE_BLOCK * MOE_BLOCK
    pends = jnp.cumsum(padded)
    pstarts = pends - padded
    dest = pstarts[experts] + ranks
    n_used = pends[-1] // MOE_BLOCK
    blk = jnp.arange(n_blocks, dtype=jnp.int32)
    block_e = jnp.minimum(jnp.searchsorted(pends, blk * MOE_BLOCK, side="right"), n_experts - 1)
    block_e = jnp.where(blk < n_used, block_e, block_e[n_used - 1]).astype(jnp.int32)
    zstart = jnp.minimum(pstarts + counts, n_slots - MOE_BLOCK).astype(jnp.int32)

    n_used = n_used.reshape(1).astype(jnp.int32)
    xs = moe_dispatch(xn_flat.reshape(n, chunks, LANES), dest, zstart, n_used,
                      n_slots=n_slots, tm=dispatch_tm)
    y = moe_experts(xs.reshape(n_slots * chunks, LANES), block_e, n_used,
                    w_gate, w_up, w_down, chunks=chunks)
    return moe_combine(h, info, y.reshape(n_slots, chunks, LANES), dest, tm=combine_tm)


def kernel(x, even_norm, even_w_in, conv_w, conv_b, conv_norm_g, conv_norm_b, sb_q_norm, sb_k_norm,
           even_w_out, odd_norm, odd_w_in, fox_forget_b, fox_q_norm, fox_k_norm, sc_w, odd_w_out,
           moe_norm, router_group_w, router_group_b, router_expert_w, router_expert_b,
           expert_w_gate, expert_w_up, expert_w_down):
    b, s, d = x.shape
    n = b * s
    h = x.reshape(n, d)
    sb_width = SB_HEADS * HEAD_DIM
    fox_width = FOX_HEADS * HEAD_DIM
    moe_tiles = dict(router_tm=512, dispatch_tm=256, combine_tm=256)

    def moe(h, layer):
        return hierarchical_moe(
            h, moe_norm[layer], router_group_w[layer], router_group_b[layer],
            router_expert_w[layer], router_expert_b[layer],
            expert_w_gate[layer].astype(BF16), expert_w_up[layer].astype(BF16),
            expert_w_down[layer].astype(BF16), **moe_tiles)

    u = norm_matmul(h, even_norm[0], even_w_in[0].astype(BF16), tm=512, tn=1024)
    u3 = u.reshape(b, s, -1)
    a = conformer_conv(u3, conv_w[0], conv_b[0], conv_norm_g[0], conv_norm_b[0], ts=256)
    o = stick_breaking_attention(u3, sb_q_norm[0], sb_k_norm[0], col0=2 * CONV_CH, heads=SB_HEADS)
    h = out_proj_residual(a.reshape(n, CONV_CH), o.reshape(n, sb_width),
                          even_w_out[0].astype(BF16), h, tm=256)
    h = moe(h, 0)

    n_qkv = 3 * fox_width
    w_in = odd_w_in[0]
    w_in = jnp.concatenate(
        [w_in[:, :n_qkv], w_in[:, n_qkv + FOX_HEADS:], w_in[:, n_qkv:n_qkv + FOX_HEADS],
         jnp.zeros((d, LANES - FOX_HEADS), F32)], axis=1).astype(BF16)
    u = norm_matmul(h, odd_norm[0], w_in, tm=512, tn=896)
    u3 = u.reshape(b, s, -1)
    c_col = forget_cumsum(u3, fox_forget_b[0], col0=n_qkv + 3 * SC_CH, tc=512)
    c_row = jnp.transpose(c_col[:, :, :FOX_HEADS], (0, 2, 1)).reshape(b, FOX_HEADS, 1, s)
    o = forgetting_attention(u3, c_col, c_row, fox_q_norm[0], fox_k_norm[0], heads=FOX_HEADS)
    y = short_conv(u3, sc_w[0], col0=n_qkv, ts=256)
    h = out_proj_residual(o.reshape(n, fox_width), y.reshape(n, SC_CH),
                          odd_w_out[0].astype(BF16), h, tm=256)
    h = moe(h, 1)
    return h.reshape(b, s, d)
```

```python
import functools

import jax
import jax.numpy as jnp
from jax import lax
from jax.experimental import pallas as pl
from jax.experimental.pallas import tpu as pltpu

F32 = jnp.float32
BF16 = jnp.bfloat16

HEAD_DIM = 128
CONV_CH = 1024
CONV_WIDTH = 31
SB_HEADS = 8
FOX_HEADS = 8
SC_CH = 1024
SC_WIDTH = 3
N_GROUPS = 4
EXPERTS_PER_GROUP = 8
N_EXPERTS = N_GROUPS * EXPERTS_PER_GROUP
MOE_BLOCK = 256
RMS_EPS = 1e-6
LN_EPS = 1e-5

LANES = 128
SUBLANES = 8
VMEM_LIMIT = 56 * 1024 * 1024


def _cparams(*sem):
    return pltpu.CompilerParams(dimension_semantics=sem, vmem_limit_bytes=VMEM_LIMIT)


def _sigmoid(x):
    return 1.0 / (1.0 + jnp.exp(-x))


def _softplus_neg_abs(z):
    return jnp.log(1.0 + jnp.exp(-jnp.abs(z)))


def _rms(x, g):
    ms = jnp.mean(x * x, axis=-1, keepdims=True)
    return x * lax.rsqrt(ms + RMS_EPS) * g


def _split_bf16(x, parts):
    out = []
    r = x
    for _ in range(parts - 1):
        p = r.astype(BF16)
        out.append(p)
        r = r - p.astype(F32)
    out.append(r.astype(BF16))
    return out


def _norm_matmul_kernel(x_ref, g_ref, w_ref, o_ref, xn_ref):
    @pl.when(pl.program_id(1) == 0)
    def _():
        xn_ref[...] = _rms(x_ref[...], g_ref[...]).astype(BF16)

    o_ref[...] = jnp.dot(xn_ref[...], w_ref[...], preferred_element_type=F32)


def norm_matmul(x, gain, w, *, tm, tn):
    n, d = x.shape
    f = w.shape[1]
    return pl.pallas_call(
        _norm_matmul_kernel,
        grid=(n // tm, f // tn),
        in_specs=[pl.BlockSpec((tm, d), lambda i, j: (i, 0)),
                  pl.BlockSpec((1, d), lambda i, j: (0, 0)),
                  pl.BlockSpec((d, tn), lambda i, j: (0, j))],
        out_specs=pl.BlockSpec((tm, tn), lambda i, j: (i, j)),
        out_shape=jax.ShapeDtypeStruct((n, f), F32),
        scratch_shapes=[pltpu.VMEM((tm, d), BF16)],
        compiler_params=_cparams("parallel", "arbitrary"),
        name="norm_matmul",
    )(x, gain.reshape(1, d), w)


def _conv_tile(abuf, w_ref, bias_ref, cbuf, *, ts, halo, width, rows):
    n_ch = cbuf.shape[1]
    off = halo - (width - 1)

    def chan_body(c, carry):
        lanes = pl.ds(pl.multiple_of(c * LANES, LANES), LANES)
        wc = w_ref[:, lanes]
        for r0 in range(0, ts, rows):
            if bias_ref is None:
                acc = jnp.zeros((rows, LANES), F32)
            else:
                acc = jnp.broadcast_to(bias_ref[:, lanes], (rows, LANES))
            for k in range(width):
                acc = acc + wc[k:k + 1, :] * abuf[pl.ds(r0 + off + k, rows), lanes]
            cbuf[pl.ds(r0, rows), lanes] = acc
        return carry

    lax.fori_loop(0, n_ch // LANES, chan_body, 0)


def _carry_halo(abuf, *, ts, halo):
    s = pl.program_id(1)

    @pl.when(s == 0)
    def _():
        abuf[0:halo, :] = jnp.zeros((halo, abuf.shape[1]), F32)

    @pl.when(s > 0)
    def _():
        abuf[0:halo, :] = abuf[ts:ts + halo, :]


CONV_HALO = 32
SC_HALO = 8


def _conformer_kernel(av_ref, ag_ref, w_ref, cb_ref, lg_ref, lb_ref, o_ref, abuf, cbuf, *, ts):
    _carry_halo(abuf, ts=ts, halo=CONV_HALO)
    abuf[CONV_HALO:CONV_HALO + ts, :] = av_ref[...] * _sigmoid(ag_ref[...])
    _conv_tile(abuf, w_ref, cb_ref, cbuf, ts=ts, halo=CONV_HALO, width=CONV_WIDTH, rows=64)
    y = cbuf[...]
    mu = jnp.mean(y, axis=-1, keepdims=True)
    yc = y - mu
    var = jnp.mean(yc * yc, axis=-1, keepdims=True)
    yn = yc * lax.rsqrt(var + LN_EPS) * lg_ref[...] + lb_ref[...]
    o_ref[...] = (yn * _sigmoid(yn)).astype(BF16)


def conformer_conv(u3, conv_w, conv_b, ln_g, ln_b, *, ts):
    b, s, _ = u3.shape
    c = conv_w.shape[1]
    w_pad = jnp.zeros((CONV_HALO, c), F32).at[:CONV_WIDTH].set(conv_w)
    row = lambda a: a.reshape(1, c)
    full = lambda shape: pl.BlockSpec(shape, lambda bi, si: (0, 0))
    return pl.pallas_call(
        functools.partial(_conformer_kernel, ts=ts),
        grid=(b, s // ts),
        in_specs=[pl.BlockSpec((None, ts, c), lambda bi, si: (bi, si, 0)),
                  pl.BlockSpec((None, ts, c), lambda bi, si: (bi, si, 1)),
                  full((CONV_HALO, c)), full((1, c)), full((1, c)), full((1, c))],
        out_specs=pl.BlockSpec((None, ts, c), lambda bi, si: (bi, si, 0)),
        out_shape=jax.ShapeDtypeStruct((b, s, c), BF16),
        scratch_shapes=[pltpu.VMEM((CONV_HALO + ts, c), F32), pltpu.VMEM((ts, c), F32)],
        compiler_params=_cparams("parallel", "arbitrary"),
        name="conformer_conv",
    )(u3, u3, w_pad, row(conv_b), row(ln_g), row(ln_b))


def _short_conv_kernel(bg_ref, cg_ref, xv_ref, w_ref, o_ref, abuf, cbuf, *, ts):
    _carry_halo(abuf, ts=ts, halo=SC_HALO)
    abuf[SC_HALO:SC_HALO + ts, :] = cg_ref[...] * xv_ref[...]
    _conv_tile(abuf, w_ref, None, cbuf, ts=ts, halo=SC_HALO, width=SC_WIDTH, rows=64)
    o_ref[...] = (bg_ref[...] * cbuf[...]).astype(BF16)


def short_conv(u3, sc_w, *, col0, ts):
    b, s, _ = u3.shape
    c = sc_w.shape[1]
    j0 = col0 // c
    w_pad = jnp.zeros((SUBLANES, c), F32).at[:SC_WIDTH].set(sc_w)
    return pl.pallas_call(
        functools.partial(_short_conv_kernel, ts=ts),
        grid=(b, s // ts),
        in_specs=[pl.BlockSpec((None, ts, c), lambda bi, si: (bi, si, j0)),
                  pl.BlockSpec((None, ts, c), lambda bi, si: (bi, si, j0 + 1)),
                  pl.BlockSpec((None, ts, c), lambda bi, si: (bi, si, j0 + 2)),
                  pl.BlockSpec((SUBLANES, c), lambda bi, si: (0, 0))],
        out_specs=pl.BlockSpec((None, ts, c), lambda bi, si: (bi, si, 0)),
        out_shape=jax.ShapeDtypeStruct((b, s, c), BF16),
        scratch_shapes=[pltpu.VMEM((SC_HALO + ts, c), F32), pltpu.VMEM((ts, c), F32)],
        compiler_params=_cparams("parallel", "arbitrary"),
        name="short_conv",
    )(u3, u3, u3, w_pad)


ATTN_HEADS = 2
ATTN_T = 256


def _head(e):
    return slice(e * HEAD_DIM, (e + 1) * HEAD_DIM)


def _prep_kv(k_ref, v_ref, kg_ref, kn_ref, vb_ref):
    @pl.when(pl.program_id(2) == 0)
    def _():
        for e in range(ATTN_HEADS):
            kn_ref[:, _head(e)] = _rms(k_ref[:, _head(e)], kg_ref[...]).astype(BF16)
        vb_ref[...] = v_ref[...].astype(BF16)


def _prep_q(q_ref, qg_ref, e):
    return (_rms(q_ref[:, _head(e)], qg_ref[...]) * (HEAD_DIM ** -0.5)).astype(BF16)


def _qk(qn, kb):
    return lax.dot_general(qn, kb, (((1,), (1,)), ((), ())), preferred_element_type=F32)


SB_SUB = 128
SB_CUTOFF = -110.0


def _sb_kernel(q_ref, k_ref, v_ref, qg_ref, kg_ref, o_ref, kn_ref, vb_ref, acc_ref, r_ref):
    t = ATTN_T
    i = pl.program_id(2)
    _prep_kv(k_ref, v_ref, kg_ref, kn_ref, vb_ref)
    qn = [_prep_q(q_ref, qg_ref, e) for e in range(ATTN_HEADS)]
    acc_ref[...] = jnp.zeros(acc_ref.shape, F32)
    r_ref[...] = jnp.zeros(r_ref.shape, F32)

    jr = lax.broadcasted_iota(jnp.int32, (t, t), 0)
    sc = lax.broadcasted_iota(jnp.int32, (t, t), 1)
    later = jnp.where((jr > sc) & (jr // SB_SUB == sc // SB_SUB), 1.0, 0.0).astype(BF16)
    strict = sc < jr

    def chunk(c, masked):
        ks = pl.multiple_of(c * t, t)
        for e in range(ATTN_HEADS):
            z = _qk(qn[e], kn_ref[pl.ds(ks, t), _head(e)])
            log_beta = jnp.minimum(z, 0.0) - _softplus_neg_abs(z)
            log_om = log_beta - z
            if masked:
                log_om = jnp.where(strict, log_om, 0.0)
            hi, lo = _split_bf16(log_om, 2)
            suffix = (jnp.dot(hi, later, preferred_element_type=F32)
                      + jnp.dot(lo, later, preferred_element_type=F32))
            tot_near = jnp.sum(log_om[:, SB_SUB:], axis=1, keepdims=True)
            tot_far = jnp.sum(log_om[:, :SB_SUB], axis=1, keepdims=True)
            r = r_ref[:, _head(e)]
            offs = jnp.concatenate([r + tot_near, r], axis=1)
            w = jnp.exp(log_beta + suffix + offs)
            if masked:
                w = jnp.where(strict, w, 0.0)
            acc_ref[:, _head(e)] += jnp.dot(w.astype(BF16), vb_ref[pl.ds(ks, t), _head(e)],
                                            preferred_element_type=F32)
            r_ref[:, _head(e)] = r + (tot_near + tot_far)

    chunk(i, True)

    def more(carry):
        n, r_max = carry
        return (n < i) & (r_max > SB_CUTOFF)

    def body(carry):
        n, _ = carry
        chunk(i - 1 - n, False)
        return n + 1, jnp.max(r_ref[...])

    lax.while_loop(more, body, (jnp.int32(0), jnp.max(r_ref[...])))
    o_ref[...] = acc_ref[...].astype(BF16)


def _attn_specs(s, j0, heads):
    w = ATTN_HEADS * HEAD_DIM
    g0 = j0 // ATTN_HEADS
    gh = heads // ATTN_HEADS
    q_spec = pl.BlockSpec((None, ATTN_T, w), lambda bi, hi, qi: (bi, qi, g0 + hi))
    k_spec = pl.BlockSpec((None, s, w), lambda bi, hi, qi: (bi, 0, g0 + gh + hi))
    v_spec = pl.BlockSpec((None, s, w), lambda bi, hi, qi: (bi, 0, g0 + 2 * gh + hi))
    o_spec = pl.BlockSpec((None, ATTN_T, w), lambda bi, hi, qi: (bi, qi, hi))
    return q_spec, k_spec, v_spec, o_spec


def stick_breaking_attention(u3, q_gain, k_gain, *, col0, heads):
    b, s, _ = u3.shape
    w = ATTN_HEADS * HEAD_DIM
    q_spec, k_spec, v_spec, o_spec = _attn_specs(s, col0 // HEAD_DIM, heads)
    gain = lambda g: g.reshape(1, HEAD_DIM)
    full = lambda shape: pl.BlockSpec(shape, lambda bi, hi, qi: (0, 0))
    return pl.pallas_call(
        _sb_kernel,
        grid=(b, heads // ATTN_HEADS, s // ATTN_T),
        in_specs=[q_spec, k_spec, v_spec, full((1, HEAD_DIM)), full((1, HEAD_DIM))],
        out_specs=o_spec,
        out_shape=jax.ShapeDtypeStruct((b, s, heads * HEAD_DIM), BF16),
        scratch_shapes=[pltpu.VMEM((s, w), BF16), pltpu.VMEM((s, w), BF16),
                        pltpu.VMEM((ATTN_T, w), F32), pltpu.VMEM((ATTN_T, w), F32)],
        compiler_params=_cparams("parallel", "parallel", "arbitrary"),
        name="stick_breaking_attention",
    )(u3, u3, u3, gain(q_gain), gain(k_gain))


def _fox_kernel(q_ref, k_ref, v_ref, cc_ref, cr_ref, qg_ref, kg_ref, o_ref,
                kn_ref, vb_ref, m_ref, l_ref, acc_ref):
    t = ATTN_T
    hg = pl.program_id(1)
    i = pl.program_id(2)
    _prep_kv(k_ref, v_ref, kg_ref, kn_ref, vb_ref)
    qn = [_prep_q(q_ref, qg_ref, e) for e in range(ATTN_HEADS)]
    lane = lax.broadcasted_iota(jnp.int32, (t, LANES), 1)
    cc = cc_ref[...]
    cq = [jnp.sum(jnp.where(lane == hg * ATTN_HEADS + e, cc, 0.0), axis=1, keepdims=True)
          for e in range(ATTN_HEADS)]
    m_ref[...] = jnp.full(m_ref.shape, -jnp.inf, F32)
    l_ref[...] = jnp.zeros(l_ref.shape, F32)
    acc_ref[...] = jnp.zeros(acc_ref.shape, F32)
    qpos = lax.broadcasted_iota(jnp.int32, (t, t), 0)
    kpos = lax.broadcasted_iota(jnp.int32, (t, t), 1)

    def chunk(j, masked):
        ks = pl.multiple_of(j * t, t)
        for e in range(ATTN_HEADS):
            z = _qk(qn[e], kn_ref[pl.ds(ks, t), _head(e)]) + (cq[e] - cr_ref[e, :, pl.ds(ks, t)])
            if masked:
                z = jnp.where(kpos <= qpos, z, -jnp.inf)
            m_prev = m_ref[:, _head(e)]
            m_new = jnp.maximum(m_prev, jnp.max(z, axis=1, keepdims=True))
            alpha = jnp.exp(m_prev - m_new)
            p = jnp.exp(z - m_new[:, 0:1])
            l_ref[:, _head(e)] = alpha * l_ref[:, _head(e)] + jnp.sum(p, axis=1, keepdims=True)
            acc_ref[:, _head(e)] = alpha * acc_ref[:, _head(e)] + jnp.dot(
                p.astype(BF16), vb_ref[pl.ds(ks, t), _head(e)], preferred_element_type=F32)
            m_ref[:, _head(e)] = m_new

    def body(j, carry):
        chunk(j, False)
        return carry

    lax.fori_loop(0, i, body, 0)
    chunk(i, True)
    o_ref[...] = (acc_ref[...] / l_ref[...]).astype(BF16)


def forgetting_attention(u3, c_col, c_row, q_gain, k_gain, *, heads):
    b, s, _ = u3.shape
    w = ATTN_HEADS * HEAD_DIM
    q_spec, k_spec, v_spec, o_spec = _attn_specs(s, 0, heads)
    gain = lambda g: g.reshape(1, HEAD_DIM)
    full = lambda shape: pl.BlockSpec(shape, lambda bi, hi, qi: (0, 0))
    return pl.pallas_call(
        _fox_kernel,
        grid=(b, heads // ATTN_HEADS, s // ATTN_T),
        in_specs=[q_spec, k_spec, v_spec,
                  pl.BlockSpec((None, ATTN_T, LANES), lambda bi, hi, qi: (bi, qi, 0)),
                  pl.BlockSpec((None, ATTN_HEADS, 1, s), lambda bi, hi, qi: (bi, hi, 0, 0)),
                  full((1, HEAD_DIM)), full((1, HEAD_DIM))],
        out_specs=o_spec,
        out_shape=jax.ShapeDtypeStruct((b, s, heads * HEAD_DIM), BF16),
        scratch_shapes=[pltpu.VMEM((s, w), BF16), pltpu.VMEM((s, w), BF16),
                        pltpu.VMEM((ATTN_T, w), F32), pltpu.VMEM((ATTN_T, w), F32),
                        pltpu.VMEM((ATTN_T, w), F32)],
        compiler_params=_cparams("parallel", "parallel", "arbitrary"),
        name="forgetting_attention",
    )(u3, u3, u3, c_col, c_row, gain(q_gain), gain(k_gain))


def _forget_cumsum_kernel(f_ref, b_ref, o_ref, carry_ref, *, tc):
    @pl.when(pl.program_id(1) == 0)
    def _():
        carry_ref[...] = jnp.zeros((1, LANES), F32)

    x = f_ref[...] + b_ref[...]
    log_f = jnp.minimum(x, 0.0) - _softplus_neg_abs(x)
    r = lax.broadcasted_iota(jnp.int32, (tc, tc), 0)
    c = lax.broadcasted_iota(jnp.int32, (tc, tc), 1)
    tri = jnp.where(r >= c, 1.0, 0.0).astype(BF16)
    cs = carry_ref[...]
    for part in _split_bf16(log_f, 3):
        cs = cs + jnp.dot(tri, part, preferred_element_type=F32)
    o_ref[...] = cs
    carry_ref[...] = cs[tc - 1:tc, :]


def forget_cumsum(u3, forget_b, *, col0, tc):
    b, s, _ = u3.shape
    j0 = col0 // LANES
    b_pad = jnp.zeros((1, LANES), F32).at[0, :forget_b.shape[0]].set(forget_b)
    return pl.pallas_call(
        functools.partial(_forget_cumsum_kernel, tc=tc),
        grid=(b, s // tc),
        in_specs=[pl.BlockSpec((None, tc, LANES), lambda bi, si: (bi, si, j0)),
                  pl.BlockSpec((1, LANES), lambda bi, si: (0, 0))],
        out_specs=pl.BlockSpec((None, tc, LANES), lambda bi, si: (bi, si, 0)),
        out_shape=jax.ShapeDtypeStruct((b, s, LANES), F32),
        scratch_shapes=[pltpu.VMEM((1, LANES), F32)],
        compiler_params=_cparams("parallel", "arbitrary"),
        name="forget_cumsum",
    )(u3, b_pad)


def _out_proj_kernel(a_ref, b_ref, w1_ref, w2_ref, h_ref, o_ref):
    o_ref[...] = (h_ref[...]
                  + jnp.dot(a_ref[...], w1_ref[...], preferred_element_type=F32)
                  + jnp.dot(b_ref[...], w2_ref[...], preferred_element_type=F32))


def out_proj_residual(a, b, w, h, *, tm):
    n, ka = a.shape
    kb = b.shape[1]
    d = w.shape[1]
    return pl.pallas_call(
        _out_proj_kernel,
        grid=(n // tm,),
        in_specs=[pl.BlockSpec((tm, ka), lambda i: (i, 0)),
                  pl.BlockSpec((tm, kb), lambda i: (i, 0)),
                  pl.BlockSpec((ka, d), lambda i: (0, 0)),
                  pl.BlockSpec((kb, d), lambda i: (0, 0)),
                  pl.BlockSpec((tm, d), lambda i: (i, 0))],
        out_specs=pl.BlockSpec((tm, d), lambda i: (i, 0)),
        out_shape=jax.ShapeDtypeStruct((n, d), F32),
        compiler_params=_cparams("parallel"),
        name="out_proj_residual",
    )(a, b, w[:ka], w[ka:], h)


INFO_E, INFO_GATE, INFO_RANK = 0, 2, 4


def _to_row_tiles(dst_ref, x, n_rows):
    chunks = x.shape[1] // LANES
    for c in range(chunks):
        dst_ref[pl.ds(c, n_rows, stride=chunks), :] = x[:, c * LANES:(c + 1) * LANES]


def _from_row_tiles(src_ref, n_rows, chunks):
    return jnp.concatenate(
        [src_ref[pl.ds(c, n_rows, stride=chunks), :] for c in range(chunks)], axis=1)


def _lane_pick(x, lane, idx):
    return jnp.sum(jnp.where(lane == idx, x, 0.0), axis=1, keepdims=True)


def _router_kernel(h_ref, g_ref, wh_ref, wl_ref, b_ref, xn_ref, info_ref, cnt_ref, carry_ref, *, tm):
    @pl.when(pl.program_id(0) == 0)
    def _():
        carry_ref[...] = jnp.zeros((1, LANES), F32)

    xn = _rms(h_ref[...], g_ref[...])
    _to_row_tiles(xn_ref, xn, tm)

    xh, xl = _split_bf16(xn, 2)
    wh = wh_ref[...]
    logits = (jnp.dot(xh, wh, preferred_element_type=F32)
              + jnp.dot(xl, wh, preferred_element_type=F32)
              + jnp.dot(xh, wl_ref[...], preferred_element_type=F32)) + b_ref[...]

    lane = lax.broadcasted_iota(jnp.int32, (tm, LANES), 1).astype(F32)
    neg = -jnp.inf
    big = float(LANES)
    gl = jnp.where(lane < N_GROUPS, logits, neg)
    gmax = jnp.max(gl, axis=1, keepdims=True)
    g_top_p = 1.0 / jnp.sum(jnp.exp(gl - gmax), axis=1, keepdims=True)
    g_idx = jnp.min(jnp.where(gl == gmax, lane, big), axis=1, keepdims=True)

    lo = N_GROUPS + EXPERTS_PER_GROUP * g_idx
    el = jnp.where((lane >= lo) & (lane < lo + EXPERTS_PER_GROUP), logits, neg)
    m1 = jnp.max(el, axis=1, keepdims=True)
    i1 = jnp.min(jnp.where(el == m1, lane, big), axis=1, keepdims=True)
    el2 = jnp.where(lane == i1, neg, el)
    m2 = jnp.max(el2, axis=1, keepdims=True)
    i2 = jnp.min(jnp.where(el2 == m2, lane, big), axis=1, keepdims=True)
    ratio = jnp.exp(m2 - m1)
    p1 = 1.0 / (1.0 + ratio)
    p2 = ratio * p1
    e1 = i1 - N_GROUPS
    e2 = i2 - N_GROUPS

    onehot = jnp.where((lane == e1) | (lane == e2), 1.0, 0.0)
    r = lax.broadcasted_iota(jnp.int32, (tm, tm), 0)
    c = lax.broadcasted_iota(jnp.int32, (tm, tm), 1)
    before = jnp.where(r > c, 1.0, 0.0).astype(BF16)
    cnt = jnp.dot(before, onehot.astype(BF16), preferred_element_type=F32) + carry_ref[...]
    rank1 = _lane_pick(cnt, lane, e1)
    rank2 = _lane_pick(cnt, lane, e2)
    total = carry_ref[...] + jnp.sum(onehot, axis=0, keepdims=True)
    carry_ref[...] = total
    cnt_ref[...] = jnp.broadcast_to(total, (SUBLANES, LANES))

    info = jnp.zeros((tm, LANES), F32)
    for k, val in ((INFO_E, e1), (INFO_E + 1, e2), (INFO_GATE, g_top_p * p1),
                   (INFO_GATE + 1, g_top_p * p2), (INFO_RANK, rank1), (INFO_RANK + 1, rank2)):
        info = jnp.where(lane == k, val, info)
    info_ref[...] = info


def moe_router(h, gain, group_w, group_b, expert_w, expert_b, *, tm):
    n, d = h.shape
    chunks = d // LANES
    n_logits = N_GROUPS + N_EXPERTS
    w = jnp.zeros((d, LANES), F32).at[:, :N_GROUPS].set(group_w).at[:, N_GROUPS:n_logits].set(expert_w)
    bias = jnp.zeros((1, LANES), F32).at[0, :N_GROUPS].set(group_b).at[0, N_GROUPS:n_logits].set(expert_b)
    w_hi = w.astype(BF16)
    w_lo = (w - w_hi.astype(F32)).astype(BF16)
    full = lambda shape: pl.BlockSpec(shape, lambda i: (0, 0))
    return pl.pallas_call(
        functools.partial(_router_kernel, tm=tm),
        grid=(n // tm,),
        in_specs=[pl.BlockSpec((tm, d), lambda i: (i, 0)), full((1, d)),
                  full((d, LANES)), full((d, LANES)), full((1, LANES))],
        out_specs=[pl.BlockSpec((tm * chunks, LANES), lambda i: (i, 0)),
                   pl.BlockSpec((tm, LANES), lambda i: (i, 0)),
                   full((SUBLANES, LANES))],
        out_shape=[jax.ShapeDtypeStruct((n * chunks, LANES), F32),
                   jax.ShapeDtypeStruct((n, LANES), F32),
                   jax.ShapeDtypeStruct((SUBLANES, LANES), F32)],
        scratch_shapes=[pltpu.VMEM((1, LANES), F32)],
        compiler_params=_cparams("arbitrary"),
        name="moe_router",
    )(h, gain.reshape(1, d), w_hi, w_lo, bias)


def _row_copy(src, dst, sem):
    return pltpu.make_async_copy(src, dst, sem)


def _dispatch_kernel(zstart_ref, nu_ref, dest_ref, xn_ref, xs_ref, zero_ref, sem, *, tm, n_experts):
    tile = pl.program_id(0)
    n_blocks = xs_ref.shape[0] // MOE_BLOCK

    @pl.when(tile == 0)
    def _():
        zero_ref[...] = jnp.zeros(zero_ref.shape, F32)
        pad = lambda start: _row_copy(zero_ref, xs_ref.at[pl.ds(start, MOE_BLOCK)], sem)
        for e in range(n_experts):
            pad(zstart_ref[e]).start()
        for e in range(n_experts):
            pad(0).wait()

        def tail(i, carry):
            cp = pad(pl.multiple_of(i * MOE_BLOCK, MOE_BLOCK))
            cp.start()
            cp.wait()
            return carry

        lax.fori_loop(nu_ref[0], n_blocks, tail, 0)

    def issue(t, carry):
        _row_copy(xn_ref.at[t], xs_ref.at[dest_ref[0, 2 * t]], sem).start()
        _row_copy(xn_ref.at[t], xs_ref.at[dest_ref[0, 2 * t + 1]], sem).start()
        return carry

    lax.fori_loop(0, tm, issue, 0)

    def drain(t, carry):
        _row_copy(xn_ref.at[0], xs_ref.at[0], sem).wait()
        _row_copy(xn_ref.at[0], xs_ref.at[0], sem).wait()
        return carry

    lax.fori_loop(0, tm, drain, 0)


def moe_dispatch(xn_rows, dest, zstart, n_used, *, n_slots, tm):
    n, chunks, _ = xn_rows.shape
    dest3 = dest.reshape(n // tm, 1, 2 * tm)
    return pl.pallas_call(
        functools.partial(_dispatch_kernel, tm=tm, n_experts=zstart.shape[0]),
        grid_spec=pltpu.PrefetchScalarGridSpec(
            num_scalar_prefetch=2,
            grid=(n // tm,),
            in_specs=[pl.BlockSpec((None, 1, 2 * tm), lambda i, z, nu: (i, 0, 0),
                                   memory_space=pltpu.SMEM),
                      pl.BlockSpec((tm, chunks, LANES), lambda i, z, nu: (i, 0, 0))],
            out_specs=pl.BlockSpec(memory_space=pl.ANY),
            scratch_shapes=[pltpu.VMEM((MOE_BLOCK, chunks, LANES), F32),
                            pltpu.SemaphoreType.DMA(())]),
        out_shape=jax.ShapeDtypeStruct((n_slots, chunks, LANES), F32),
        compiler_params=_cparams("arbitrary"),
        name="moe_dispatch",
    )(zstart, n_used, dest3, xn_rows)


def _expert_kernel(be_ref, nu_ref, x_ref, wg_ref, wu_ref, wd_ref, y_ref, *, chunks):
    i = pl.program_id(0)

    @pl.when(i < nu_ref[0])
    def _():
        x = _from_row_tiles(x_ref, MOE_BLOCK, chunks).astype(BF16)
        gate = jnp.dot(x, wg_ref[...], preferred_element_type=F32)
        up = jnp.dot(x, wu_ref[...], preferred_element_type=F32)
        hidden = (gate * _sigmoid(gate) * up).astype(BF16)
        y = jnp.dot(hidden, wd_ref[...], preferred_element_type=F32)
        _to_row_tiles(y_ref, y, MOE_BLOCK)

    @pl.when(i >= nu_ref[0])
    def _():
        y_ref[...] = jnp.zeros(y_ref.shape, F32)


def moe_experts(xs_flat, block_e, n_used, w_gate, w_up, w_down, *, chunks):
    rows = xs_flat.shape[0]
    n_blocks = rows // (MOE_BLOCK * chunks)
    _, d, ff = w_gate.shape
    x_map = lambda i, be, nu: (jnp.minimum(i, nu[0] - 1), 0)
    return pl.pallas_call(
        functools.partial(_expert_kernel, chunks=chunks),
        grid_spec=pltpu.PrefetchScalarGridSpec(
            num_scalar_prefetch=2,
            grid=(n_blocks,),
            in_specs=[pl.BlockSpec((MOE_BLOCK * chunks, LANES), x_map),
                      pl.BlockSpec((None, d, ff), lambda i, be, nu: (be[i], 0, 0)),
                      pl.BlockSpec((None, d, ff), lambda i, be, nu: (be[i], 0, 0)),
                      pl.BlockSpec((None, ff, d), lambda i, be, nu: (be[i], 0, 0))],
            out_specs=pl.BlockSpec((MOE_BLOCK * chunks, LANES), lambda i, be, nu: (i, 0))),
        out_shape=jax.ShapeDtypeStruct((rows, LANES), F32),
        compiler_params=_cparams("arbitrary"),
        name="moe_experts",
    )(block_e, n_used, xs_flat, w_gate, w_up, w_down)


def _combine_kernel(dest_ref, h_ref, info_ref, y_ref, o_ref, buf1, buf2, sem, *, tm, chunks):
    def issue(t, carry):
        rows = pl.ds(pl.multiple_of(t * chunks, chunks), chunks)
        _row_copy(y_ref.at[dest_ref[0, 2 * t]], buf1.at[rows], sem).start()
        _row_copy(y_ref.at[dest_ref[0, 2 * t + 1]], buf2.at[rows], sem).start()
        return carry

    lax.fori_loop(0, tm, issue, 0)

    def drain(t, carry):
        rows = pl.ds(0, chunks)
        _row_copy(y_ref.at[0], buf1.at[rows], sem).wait()
        _row_copy(y_ref.at[0], buf2.at[rows], sem).wait()
        return carry

    lax.fori_loop(0, tm, drain, 0)

    info = info_ref[...]
    lane = lax.broadcasted_iota(jnp.int32, (tm, LANES), 1)
    g1 = jnp.sum(jnp.where(lane == INFO_GATE, info, 0.0), axis=1, keepdims=True)
    g2 = jnp.sum(jnp.where(lane == INFO_GATE + 1, info, 0.0), axis=1, keepdims=True)
    o_ref[...] = (h_ref[...] + g1 * _from_row_tiles(buf1, tm, chunks)
                  + g2 * _from_row_tiles(buf2, tm, chunks))


def moe_combine(h, info, y_rows, dest, *, tm):
    n, d = h.shape
    chunks = d // LANES
    dest3 = dest.reshape(n // tm, 1, 2 * tm)
    return pl.pallas_call(
        functools.partial(_combine_kernel, tm=tm, chunks=chunks),
        grid=(n // tm,),
        in_specs=[pl.BlockSpec((None, 1, 2 * tm), lambda i: (i, 0, 0), memory_space=pltpu.SMEM),
                  pl.BlockSpec((tm, d), lambda i: (i, 0)),
                  pl.BlockSpec((tm, LANES), lambda i: (i, 0)),
                  pl.BlockSpec(memory_space=pl.ANY)],
        out_specs=pl.BlockSpec((tm, d), lambda i: (i, 0)),
        out_shape=jax.ShapeDtypeStruct((n, d), F32),
        scratch_shapes=[pltpu.VMEM((tm * chunks, LANES), F32), pltpu.VMEM((tm * chunks, LANES), F32),
                        pltpu.SemaphoreType.DMA(())],
        compiler_params=_cparams("arbitrary"),
        name="moe_combine",
    )(dest3, h, info, y_rows)


def hierarchical_moe(h, gain, group_w, group_b, expert_w, expert_b, w_gate, w_up, w_down,
                     *, router_tm, dispatch_tm, combine_tm):
    n, d = h.shape
    chunks = d // LANES
    xn_flat, info, cnt = moe_router(h, gain, group_w, group_b, expert_w, expert_b, tm=router_tm)

    n_experts = w_gate.shape[0]
    n_blocks = -(-2 * n // MOE_BLOCK) + n_experts
    n_slots = n_blocks * MOE_BLOCK
    experts = info[:, INFO_E:INFO_E + 2].astype(jnp.int32)
    ranks = info[:, INFO_RANK:INFO_RANK + 2].astype(jnp.int32)
    counts = cnt[0, :n_experts].astype(jnp.int32)
    padded = (counts + MOE_BLOCK - 1) // MOE_BLOCK * MOE_BLOCK
    pends = jnp.cumsum(padded)
    pstarts = pends - padded
    dest = pstarts[experts] + ranks
    n_used = pends[-1] // MOE_BLOCK
    blk = jnp.arange(n_blocks, dtype=jnp.int32)
    block_e = jnp.sum((pends[None, :] <= (blk * MOE_BLOCK)[:, None]).astype(jnp.int32), axis=1)
    block_e = jnp.minimum(block_e, n_experts - 1)
    block_e = jnp.where(blk < n_used, block_e, block_e[n_used - 1]).astype(jnp.int32)
    zstart = jnp.minimum(pstarts + counts, n_slots - MOE_BLOCK).astype(jnp.int32)

    n_used = n_used.reshape(1).astype(jnp.int32)
    xs = moe_dispatch(xn_flat.reshape(n, chunks, LANES), dest, zstart, n_used,
                      n_slots=n_slots, tm=dispatch_tm)
    y = moe_experts(xs.reshape(n_slots * chunks, LANES), block_e, n_used,
                    w_gate, w_up, w_down, chunks=chunks)
    return moe_combine(h, info, y.reshape(n_slots, chunks, LANES), dest, tm=combine_tm)


def kernel(x, even_norm, even_w_in, conv_w, conv_b, conv_norm_g, conv_norm_b, sb_q_norm, sb_k_norm,
           even_w_out, odd_norm, odd_w_in, fox_forget_b, fox_q_norm, fox_k_norm, sc_w, odd_w_out,
           moe_norm, router_group_w, router_group_b, router_expert_w, router_expert_b,
           expert_w_gate, expert_w_up, expert_w_down):
    b, s, d = x.shape
    n = b * s
    h = x.reshape(n, d)
    sb_width = SB_HEADS * HEAD_DIM
    fox_width = FOX_HEADS * HEAD_DIM
    moe_tiles = dict(router_tm=512, dispatch_tm=256, combine_tm=256)

    def moe(h, layer):
        return hierarchical_moe(
            h, moe_norm[layer], router_group_w[layer], router_group_b[layer],
            router_expert_w[layer], router_expert_b[layer],
            expert_w_gate[layer].astype(BF16), expert_w_up[layer].astype(BF16),
            expert_w_down[layer].astype(BF16), **moe_tiles)

    u = norm_matmul(h, even_norm[0], even_w_in[0].astype(BF16), tm=512, tn=1024)
    u3 = u.reshape(b, s, -1)
    a = conformer_conv(u3, conv_w[0], conv_b[0], conv_norm_g[0], conv_norm_b[0], ts=256)
    o = stick_breaking_attention(u3, sb_q_norm[0], sb_k_norm[0], col0=2 * CONV_CH, heads=SB_HEADS)
    h = out_proj_residual(a.reshape(n, CONV_CH), o.reshape(n, sb_width),
                          even_w_out[0].astype(BF16), h, tm=256)
    h = moe(h, 0)

    n_qkv = 3 * fox_width
    w_in = odd_w_in[0]
    w_in = jnp.concatenate(
        [w_in[:, :n_qkv], w_in[:, n_qkv + FOX_HEADS:], w_in[:, n_qkv:n_qkv + FOX_HEADS],
         jnp.zeros((d, LANES - FOX_HEADS), F32)], axis=1).astype(BF16)
    u = norm_matmul(h, odd_norm[0], w_in, tm=512, tn=896)
    u3 = u.reshape(b, s, -1)
    c_col = forget_cumsum(u3, fox_forget_b[0], col0=n_qkv + 3 * SC_CH, tc=512)
    c_row = jnp.transpose(c_col[:, :, :FOX_HEADS], (0, 2, 1)).reshape(b, FOX_HEADS, 1, s)
    o = forgetting_attention(u3, c_col, c_row, fox_q_norm[0], fox_k_norm[0], heads=FOX_HEADS)
    y = short_conv(u3, sc_w[0], col0=n_qkv, ts=256)
    h = out_proj_residual(o.reshape(n, fox_width), y.reshape(n, SC_CH),
                          odd_w_out[0].astype(BF16), h, tm=256)
    h = moe(h, 1)
    return h.reshape(b, s, d)
```

```python
import functools

import jax
import jax.numpy as jnp
from jax import lax
from jax.experimental import pallas as pl
from jax.experimental.pallas import tpu as pltpu

F32 = jnp.float32
BF16 = jnp.bfloat16

HEAD_DIM = 128
CONV_CH = 1024
CONV_WIDTH = 31
SB_HEADS = 8
FOX_HEADS = 8
SC_CH = 1024
SC_WIDTH = 3
N_GROUPS = 4
EXPERTS_PER_GROUP = 8
N_EXPERTS = N_GROUPS * EXPERTS_PER_GROUP
MOE_BLOCK = 256
RMS_EPS = 1e-6
LN_EPS = 1e-5

LANES = 128
SUBLANES = 8
VMEM_LIMIT = 56 * 1024 * 1024


def _cparams(*sem):
    return pltpu.CompilerParams(dimension_semantics=sem, vmem_limit_bytes=VMEM_LIMIT)


def _sigmoid(x):
    return 1.0 / (1.0 + jnp.exp(-x))


def _softplus_neg_abs(z):
    return jnp.log(1.0 + jnp.exp(-jnp.abs(z)))


def _rms(x, g):
    ms = jnp.mean(x * x, axis=-1, keepdims=True)
    return x * lax.rsqrt(ms + RMS_EPS) * g


def _split_bf16(x, parts):
    out = []
    r = x
    for _ in range(parts - 1):
        p = r.astype(BF16)
        out.append(p)
        r = r - p.astype(F32)
    out.append(r.astype(BF16))
    return out


def _norm_matmul_kernel(x_ref, g_ref, w_ref, o_ref, xn_ref):
    @pl.when(pl.program_id(1) == 0)
    def _():
        xn_ref[...] = _rms(x_ref[...], g_ref[...]).astype(BF16)

    o_ref[...] = jnp.dot(xn_ref[...], w_ref[...], preferred_element_type=F32)


def norm_matmul(x, gain, w, *, tm, tn):
    n, d = x.shape
    f = w.shape[1]
    return pl.pallas_call(
        _norm_matmul_kernel,
        grid=(n // tm, f // tn),
        in_specs=[pl.BlockSpec((tm, d), lambda i, j: (i, 0)),
                  pl.BlockSpec((1, d), lambda i, j: (0, 0)),
                  pl.BlockSpec((d, tn), lambda i, j: (0, j))],
        out_specs=pl.BlockSpec((tm, tn), lambda i, j: (i, j)),
        out_shape=jax.ShapeDtypeStruct((n, f), F32),
        scratch_shapes=[pltpu.VMEM((tm, d), BF16)],
        compiler_params=_cparams("parallel", "arbitrary"),
        name="norm_matmul",
    )(x, gain.reshape(1, d), w)


def _conv_tile(abuf, w_ref, bias_ref, cbuf, *, ts, halo, width, rows):
    n_ch = cbuf.shape[1]
    off = halo - (width - 1)

    def chan_body(c, carry):
        lanes = pl.ds(pl.multiple_of(c * LANES, LANES), LANES)
        wc = w_ref[:, lanes]
        for r0 in range(0, ts, rows):
            if bias_ref is None:
                acc = jnp.zeros((rows, LANES), F32)
            else:
                acc = jnp.broadcast_to(bias_ref[:, lanes], (rows, LANES))
            for k in range(width):
                acc = acc + wc[k:k + 1, :] * abuf[pl.ds(r0 + off + k, rows), lanes]
            cbuf[pl.ds(r0, rows), lanes] = acc
        return carry

    lax.fori_loop(0, n_ch // LANES, chan_body, 0)


def _carry_halo(abuf, *, ts, halo):
    s = pl.program_id(1)

    @pl.when(s == 0)
    def _():
        abuf[0:halo, :] = jnp.zeros((halo, abuf.shape[1]), F32)

    @pl.when(s > 0)
    def _():
        abuf[0:halo, :] = abuf[ts:ts + halo, :]


CONV_HALO = 32
SC_HALO = 8


def _conformer_kernel(av_ref, ag_ref, w_ref, cb_ref, lg_ref, lb_ref, o_ref, abuf, cbuf, *, ts):
    _carry_halo(abuf, ts=ts, halo=CONV_HALO)
    abuf[CONV_HALO:CONV_HALO + ts, :] = av_ref[...] * _sigmoid(ag_ref[...])
    _conv_tile(abuf, w_ref, cb_ref, cbuf, ts=ts, halo=CONV_HALO, width=CONV_WIDTH, rows=64)
    y = cbuf[...]
    mu = jnp.mean(y, axis=-1, keepdims=True)
    yc = y - mu
    var = jnp.mean(yc * yc, axis=-1, keepdims=True)
    yn = yc * lax.rsqrt(var + LN_EPS) * lg_ref[...] + lb_ref[...]
    o_ref[...] = (yn * _sigmoid(yn)).astype(BF16)


def conformer_conv(u3, conv_w, conv_b, ln_g, ln_b, *, ts):
    b, s, _ = u3.shape
    c = conv_w.shape[1]
    w_pad = jnp.zeros((CONV_HALO, c), F32).at[:CONV_WIDTH].set(conv_w)
    row = lambda a: a.reshape(1, c)
    full = lambda shape: pl.BlockSpec(shape, lambda bi, si: (0, 0))
    return pl.pallas_call(
        functools.partial(_conformer_kernel, ts=ts),
        grid=(b, s // ts),
        in_specs=[pl.BlockSpec((None, ts, c), lambda bi, si: (bi, si, 0)),
                  pl.BlockSpec((None, ts, c), lambda bi, si: (bi, si, 1)),
                  full((CONV_HALO, c)), full((1, c)), full((1, c)), full((1, c))],
        out_specs=pl.BlockSpec((None, ts, c), lambda bi, si: (bi, si, 0)),
        out_shape=jax.ShapeDtypeStruct((b, s, c), BF16),
        scratch_shapes=[pltpu.VMEM((CONV_HALO + ts, c), F32), pltpu.VMEM((ts, c), F32)],
        compiler_params=_cparams("parallel", "arbitrary"),
        name="conformer_conv",
    )(u3, u3, w_pad, row(conv_b), row(ln_g), row(ln_b))


def _short_conv_kernel(bg_ref, cg_ref, xv_ref, w_ref, o_ref, abuf, cbuf, *, ts):
    _carry_halo(abuf, ts=ts, halo=SC_HALO)
    abuf[SC_HALO:SC_HALO + ts, :] = cg_ref[...] * xv_ref[...]
    _conv_tile(abuf, w_ref, None, cbuf, ts=ts, halo=SC_HALO, width=SC_WIDTH, rows=64)
    o_ref[...] = (bg_ref[...] * cbuf[...]).astype(BF16)


def short_conv(u3, sc_w, *, col0, ts):
    b, s, _ = u3.shape
    c = sc_w.shape[1]
    j0 = col0 // c
    w_pad = jnp.zeros((SUBLANES, c), F32).at[:SC_WIDTH].set(sc_w)
    return pl.pallas_call(
        functools.partial(_short_conv_kernel, ts=ts),
        grid=(b, s // ts),
        in_specs=[pl.BlockSpec((None, ts, c), lambda bi, si: (bi, si, j0)),
                  pl.BlockSpec((None, ts, c), lambda bi, si: (bi, si, j0 + 1)),
                  pl.BlockSpec((None, ts, c), lambda bi, si: (bi, si, j0 + 2)),
                  pl.BlockSpec((SUBLANES, c), lambda bi, si: (0, 0))],
        out_specs=pl.BlockSpec((None, ts, c), lambda bi, si: (bi, si, 0)),
        out_shape=jax.ShapeDtypeStruct((b, s, c), BF16),
        scratch_shapes=[pltpu.VMEM((SC_HALO + ts, c), F32), pltpu.VMEM((ts, c), F32)],
        compiler_params=_cparams("parallel", "arbitrary"),
        name="short_conv",
    )(u3, u3, u3, w_pad)


ATTN_HEADS = 2
ATTN_T = 256


def _head(e):
    return slice(e * HEAD_DIM, (e + 1) * HEAD_DIM)


def _prep_kv(k_ref, v_ref, kg_ref, kn_ref, vb_ref):
    @pl.when(pl.program_id(2) == 0)
    def _():
        for e in range(ATTN_HEADS):
            kn_ref[:, _head(e)] = _rms(k_ref[:, _head(e)], kg_ref[...]).astype(BF16)
        vb_ref[...] = v_ref[...].astype(BF16)


def _prep_q(q_ref, qg_ref, e):
    return (_rms(q_ref[:, _head(e)], qg_ref[...]) * (HEAD_DIM ** -0.5)).astype(BF16)


def _qk(qn, kb):
    return lax.dot_general(qn, kb, (((1,), (1,)), ((), ())), preferred_element_type=F32)


SB_SUB = 128
SB_CUTOFF = -110.0


def _sb_kernel(q_ref, k_ref, v_ref, qg_ref, kg_ref, o_ref, kn_ref, vb_ref, acc_ref, r_ref):
    t = ATTN_T
    i = pl.program_id(2)
    _prep_kv(k_ref, v_ref, kg_ref, kn_ref, vb_ref)
    qn = [_prep_q(q_ref, qg_ref, e) for e in range(ATTN_HEADS)]
    acc_ref[...] = jnp.zeros(acc_ref.shape, F32)
    r_ref[...] = jnp.zeros(r_ref.shape, F32)

    jr = lax.broadcasted_iota(jnp.int32, (t, t), 0)
    sc = lax.broadcasted_iota(jnp.int32, (t, t), 1)
    later = jnp.where((jr > sc) & (jr // SB_SUB == sc // SB_SUB), 1.0, 0.0).astype(BF16)
    strict = sc < jr

    def local_terms(c, masked):
        ks = pl.multiple_of(c * t, t)
        out = []
        for e in range(ATTN_HEADS):
            z = _qk(qn[e], kn_ref[pl.ds(ks, t), _head(e)])
            log_beta = jnp.minimum(z, 0.0) - _softplus_neg_abs(z)
            log_om = log_beta - z
            if masked:
                log_om = jnp.where(strict, log_om, 0.0)
            hi, lo = _split_bf16(log_om, 2)
            suffix = (jnp.dot(hi, later, preferred_element_type=F32)
                      + jnp.dot(lo, later, preferred_element_type=F32))
            out.append((log_beta + suffix,
                        jnp.sum(log_om[:, SB_SUB:], axis=1, keepdims=True),
                        jnp.sum(log_om[:, :SB_SUB], axis=1, keepdims=True)))
        return tuple(out)

    def accumulate(c, masked, terms):
        ks = pl.multiple_of(c * t, t)
        for e in range(ATTN_HEADS):
            base, tot_near, tot_far = terms[e]
            r = r_ref[:, _head(e)]
            w = jnp.exp(base + jnp.concatenate([r + tot_near, r], axis=1))
            if masked:
                w = jnp.where(strict, w, 0.0)
            acc_ref[:, _head(e)] += jnp.dot(w.astype(BF16), vb_ref[pl.ds(ks, t), _head(e)],
                                            preferred_element_type=F32)
            r_ref[:, _head(e)] = r + (tot_near + tot_far)

    accumulate(i, True, local_terms(i, True))

    def more(carry):
        n, r_max, _ = carry
        return (n < i) & (r_max > SB_CUTOFF)

    def body(carry):
        n, _, terms = carry
        c = i - 1 - n
        ahead = local_terms(jnp.maximum(c - 1, 0), False)
        accumulate(c, False, terms)
        return n + 1, jnp.max(r_ref[...]), ahead

    first = local_terms(jnp.maximum(i - 1, 0), False)
    lax.while_loop(more, body, (jnp.int32(0), jnp.max(r_ref[...]), first))
    o_ref[...] = acc_ref[...].astype(BF16)


def _attn_specs(s, j0, heads):
    w = ATTN_HEADS * HEAD_DIM
    g0 = j0 // ATTN_HEADS
    gh = heads // ATTN_HEADS
    q_spec = pl.BlockSpec((None, ATTN_T, w), lambda bi, hi, qi: (bi, qi, g0 + hi))
    k_spec = pl.BlockSpec((None, s, w), lambda bi, hi, qi: (bi, 0, g0 + gh + hi))
    v_spec = pl.BlockSpec((None, s, w), lambda bi, hi, qi: (bi, 0, g0 + 2 * gh + hi))
    o_spec = pl.BlockSpec((None, ATTN_T, w), lambda bi, hi, qi: (bi, qi, hi))
    return q_spec, k_spec, v_spec, o_spec


def stick_breaking_attention(u3, q_gain, k_gain, *, col0, heads):
    b, s, _ = u3.shape
    w = ATTN_HEADS * HEAD_DIM
    q_spec, k_spec, v_spec, o_spec = _attn_specs(s, col0 // HEAD_DIM, heads)
    gain = lambda g: g.reshape(1, HEAD_DIM)
    full = lambda shape: pl.BlockSpec(shape, lambda bi, hi, qi: (0, 0))
    return pl.pallas_call(
        _sb_kernel,
        grid=(b, heads // ATTN_HEADS, s // ATTN_T),
        in_specs=[q_spec, k_spec, v_spec, full((1, HEAD_DIM)), full((1, HEAD_DIM))],
        out_specs=o_spec,
        out_shape=jax.ShapeDtypeStruct((b, s, heads * HEAD_DIM), BF16),
        scratch_shapes=[pltpu.VMEM((s, w), BF16), pltpu.VMEM((s, w), BF16),
                        pltpu.VMEM((ATTN_T, w), F32), pltpu.VMEM((ATTN_T, w), F32)],
        compiler_params=_cparams("parallel", "parallel", "arbitrary"),
        name="stick_breaking_attention",
    )(u3, u3, u3, gain(q_gain), gain(k_gain))


FOX_PREP = 512


def _fox_kernel(q_ref, k_ref, v_ref, cc_ref, cr_ref, qg_ref, kg_ref, o_ref,
                kn_ref, vt_ref, cs_ref, acc_ref):
    t = ATTN_T
    hg = pl.program_id(1)
    i = pl.program_id(2)
    s = k_ref.shape[0]
    rows = lambda e: slice(e * HEAD_DIM, (e + 1) * HEAD_DIM)

    @pl.when(i == 0)
    def _():
        lane = lax.broadcasted_iota(jnp.int32, (FOX_PREP, LANES), 1)
        for e in range(ATTN_HEADS):
            kn_ref[:, _head(e)] = _rms(k_ref[:, _head(e)], kg_ref[...]).astype(BF16)
            for c0 in range(0, s, FOX_PREP):
                blk = slice(c0, c0 + FOX_PREP)
                vt_ref[rows(e), blk] = v_ref[blk, _head(e)].T.astype(BF16)
                col = jnp.sum(jnp.where(lane == hg * ATTN_HEADS + e, cc_ref[blk, :], 0.0),
                              axis=1, keepdims=True)
                cs_ref[blk, _head(e)] = jnp.broadcast_to(col, (FOX_PREP, HEAD_DIM))

    qt = [(_rms(q_ref[:, _head(e)], qg_ref[...]) * (HEAD_DIM ** -0.5)).T.astype(BF16)
          for e in range(ATTN_HEADS)]
    ct = [cr_ref[e, :, pl.ds(pl.multiple_of(i * t, t), t)] for e in range(ATTN_HEADS)]
    acc_ref[...] = jnp.zeros(acc_ref.shape, F32)
    kpos = lax.broadcasted_iota(jnp.int32, (t, t), 0)
    qpos = lax.broadcasted_iota(jnp.int32, (t, t), 1)

    def scores(j):
        ks = pl.multiple_of(j * t, t)
        return tuple(jnp.dot(kn_ref[pl.ds(ks, t), _head(e)], qt[e], preferred_element_type=F32)
                     for e in range(ATTN_HEADS))

    def chunk(j, masked, stats, qk):
        ks = pl.multiple_of(j * t, t)
        out = []
        for e in range(ATTN_HEADS):
            m_prev, l_prev = stats[e]
            cs = cs_ref[pl.ds(ks, t), _head(e)]
            zt = qk[e] + (ct[e] - jnp.concatenate([cs] * (t // HEAD_DIM), axis=1))
            if masked:
                zt = jnp.where(kpos <= qpos, zt, -jnp.inf)
            m_new = jnp.maximum(m_prev, jnp.max(zt, axis=0, keepdims=True))
            alpha = jnp.exp(m_prev - m_new)
            pt = jnp.exp(zt - m_new)
            l_new = alpha * l_prev + jnp.sum(pt, axis=0, keepdims=True)
            acc_ref[rows(e), :] = alpha * acc_ref[rows(e), :] + jnp.dot(
                vt_ref[rows(e), pl.ds(ks, t)], pt.astype(BF16), preferred_element_type=F32)
            out.append((m_new, l_new))
        return tuple(out)

    init = tuple((jnp.full((1, t), -jnp.inf, F32), jnp.zeros((1, t), F32))
                 for _ in range(ATTN_HEADS))

    def body(j, carry):
        stats, qk = carry
        nxt = scores(j + 1)
        return chunk(j, False, stats, qk), nxt

    stats, qk = lax.fori_loop(0, i, body, (init, scores(0)))
    stats = chunk(i, True, stats, qk)
    for e in range(ATTN_HEADS):
        o_ref[:, _head(e)] = (acc_ref[rows(e), :] / stats[e][1]).T.astype(BF16)


def forgetting_attention(u3, c_col, c_row, q_gain, k_gain, *, heads):
    b, s, _ = u3.shape
    w = ATTN_HEADS * HEAD_DIM
    q_spec, k_spec, v_spec, o_spec = _attn_specs(s, 0, heads)
    gain = lambda g: g.reshape(1, HEAD_DIM)
    full = lambda shape: pl.BlockSpec(shape, lambda bi, hi, qi: (0, 0))
    return pl.pallas_call(
        _fox_kernel,
        grid=(b, heads // ATTN_HEADS, s // ATTN_T),
        in_specs=[q_spec, k_spec, v_spec,
                  pl.BlockSpec((None, s, LANES), lambda bi, hi, qi: (bi, 0, 0)),
                  pl.BlockSpec((None, ATTN_HEADS, 1, s), lambda bi, hi, qi: (bi, hi, 0, 0)),
                  full((1, HEAD_DIM)), full((1, HEAD_DIM))],
        out_specs=o_spec,
        out_shape=jax.ShapeDtypeStruct((b, s, heads * HEAD_DIM), BF16),
        scratch_shapes=[pltpu.VMEM((s, w), BF16), pltpu.VMEM((w, s), BF16),
                        pltpu.VMEM((s, w), F32), pltpu.VMEM((w, ATTN_T), F32)],
        compiler_params=_cparams("parallel", "parallel", "arbitrary"),
        name="forgetting_attention",
    )(u3, u3, u3, c_col, c_row, gain(q_gain), gain(k_gain))


def _forget_cumsum_kernel(f_ref, b_ref, o_ref, carry_ref, *, tc):
    @pl.when(pl.program_id(1) == 0)
    def _():
        carry_ref[...] = jnp.zeros((1, LANES), F32)

    x = f_ref[...] + b_ref[...]
    log_f = jnp.minimum(x, 0.0) - _softplus_neg_abs(x)
    r = lax.broadcasted_iota(jnp.int32, (tc, tc), 0)
    c = lax.broadcasted_iota(jnp.int32, (tc, tc), 1)
    tri = jnp.where(r >= c, 1.0, 0.0).astype(BF16)
    cs = carry_ref[...]
    for part in _split_bf16(log_f, 3):
        cs = cs + jnp.dot(tri, part, preferred_element_type=F32)
    o_ref[...] = cs
    carry_ref[...] = cs[tc - 1:tc, :]


def forget_cumsum(u3, forget_b, *, col0, tc):
    b, s, _ = u3.shape
    j0 = col0 // LANES
    b_pad = jnp.zeros((1, LANES), F32).at[0, :forget_b.shape[0]].set(forget_b)
    return pl.pallas_call(
        functools.partial(_forget_cumsum_kernel, tc=tc),
        grid=(b, s // tc),
        in_specs=[pl.BlockSpec((None, tc, LANES), lambda bi, si: (bi, si, j0)),
                  pl.BlockSpec((1, LANES), lambda bi, si: (0, 0))],
        out_specs=pl.BlockSpec((None, tc, LANES), lambda bi, si: (bi, si, 0)),
        out_shape=jax.ShapeDtypeStruct((b, s, LANES), F32),
        scratch_shapes=[pltpu.VMEM((1, LANES), F32)],
        compiler_params=_cparams("parallel", "arbitrary"),
        name="forget_cumsum",
    )(u3, b_pad)


def _out_proj_kernel(a_ref, b_ref, w1_ref, w2_ref, h_ref, o_ref):
    o_ref[...] = (h_ref[...]
                  + jnp.dot(a_ref[...], w1_ref[...], preferred_element_type=F32)
                  + jnp.dot(b_ref[...], w2_ref[...], preferred_element_type=F32))


def out_proj_residual(a, b, w, h, *, tm):
    n, ka = a.shape
    kb = b.shape[1]
    d = w.shape[1]
    return pl.pallas_call(
        _out_proj_kernel,
        grid=(n // tm,),
        in_specs=[pl.BlockSpec((tm, ka), lambda i: (i, 0)),
                  pl.BlockSpec((tm, kb), lambda i: (i, 0)),
                  pl.BlockSpec((ka, d), lambda i: (0, 0)),
                  pl.BlockSpec((kb, d), lambda i: (0, 0)),
                  pl.BlockSpec((tm, d), lambda i: (i, 0))],
        out_specs=pl.BlockSpec((tm, d), lambda i: (i, 0)),
        out_shape=jax.ShapeDtypeStruct((n, d), F32),
        compiler_params=_cparams("parallel"),
        name="out_proj_residual",
    )(a, b, w[:ka], w[ka:], h)


INFO_E, INFO_GATE, INFO_RANK = 0, 2, 4


def _to_row_tiles(dst_ref, x, n_rows):
    chunks = x.shape[1] // LANES
    for c in range(chunks):
        dst_ref[pl.ds(c, n_rows, stride=chunks), :] = x[:, c * LANES:(c + 1) * LANES]


def _from_row_tiles(src_ref, n_rows, chunks):
    return jnp.concatenate(
        [src_ref[pl.ds(c, n_rows, stride=chunks), :] for c in range(chunks)], axis=1)


def _lane_pick(x, lane, idx):
    return jnp.sum(jnp.where(lane == idx, x, 0.0), axis=1, keepdims=True)


def _router_kernel(h_ref, g_ref, wh_ref, wl_ref, b_ref, xn_ref, info_ref, cnt_ref, carry_ref, *, tm):
    @pl.when(pl.program_id(0) == 0)
    def _():
        carry_ref[...] = jnp.zeros((1, LANES), F32)

    xn = _rms(h_ref[...], g_ref[...])
    _to_row_tiles(xn_ref, xn, tm)

    xh, xl = _split_bf16(xn, 2)
    wh = wh_ref[...]
    logits = (jnp.dot(xh, wh, preferred_element_type=F32)
              + jnp.dot(xl, wh, preferred_element_type=F32)
              + jnp.dot(xh, wl_ref[...], preferred_element_type=F32)) + b_ref[...]

    lane = lax.broadcasted_iota(jnp.int32, (tm, LANES), 1).astype(F32)
    neg = -jnp.inf
    big = float(LANES)
    gl = jnp.where(lane < N_GROUPS, logits, neg)
    gmax = jnp.max(gl, axis=1, keepdims=True)
    g_top_p = 1.0 / jnp.sum(jnp.exp(gl - gmax), axis=1, keepdims=True)
    g_idx = jnp.min(jnp.where(gl == gmax, lane, big), axis=1, keepdims=True)

    lo = N_GROUPS + EXPERTS_PER_GROUP * g_idx
    el = jnp.where((lane >= lo) & (lane < lo + EXPERTS_PER_GROUP), logits, neg)
    m1 = jnp.max(el, axis=1, keepdims=True)
    i1 = jnp.min(jnp.where(el == m1, lane, big), axis=1, keepdims=True)
    el2 = jnp.where(lane == i1, neg, el)
    m2 = jnp.max(el2, axis=1, keepdims=True)
    i2 = jnp.min(jnp.where(el2 == m2, lane, big), axis=1, keepdims=True)
    ratio = jnp.exp(m2 - m1)
    p1 = 1.0 / (1.0 + ratio)
    p2 = ratio * p1
    e1 = i1 - N_GROUPS
    e2 = i2 - N_GROUPS

    onehot = jnp.where((lane == e1) | (lane == e2), 1.0, 0.0)
    r = lax.broadcasted_iota(jnp.int32, (tm, tm), 0)
    c = lax.broadcasted_iota(jnp.int32, (tm, tm), 1)
    before = jnp.where(r > c, 1.0, 0.0).astype(BF16)
    cnt = jnp.dot(before, onehot.astype(BF16), preferred_element_type=F32) + carry_ref[...]
    rank1 = _lane_pick(cnt, lane, e1)
    rank2 = _lane_pick(cnt, lane, e2)
    total = carry_ref[...] + jnp.sum(onehot, axis=0, keepdims=True)
    carry_ref[...] = total
    cnt_ref[...] = jnp.broadcast_to(total, (SUBLANES, LANES))

    info = jnp.zeros((tm, LANES), F32)
    for k, val in ((INFO_E, e1), (INFO_E + 1, e2), (INFO_GATE, g_top_p * p1),
                   (INFO_GATE + 1, g_top_p * p2), (INFO_RANK, rank1), (INFO_RANK + 1, rank2)):
        info = jnp.where(lane == k, val, info)
    info_ref[...] = info


def moe_router(h, gain, group_w, group_b, expert_w, expert_b, *, tm):
    n, d = h.shape
    chunks = d // LANES
    n_logits = N_GROUPS + N_EXPERTS
    w = jnp.zeros((d, LANES), F32).at[:, :N_GROUPS].set(group_w).at[:, N_GROUPS:n_logits].set(expert_w)
    bias = jnp.zeros((1, LANES), F32).at[0, :N_GROUPS].set(group_b).at[0, N_GROUPS:n_logits].set(expert_b)
    w_hi = w.astype(BF16)
    w_lo = (w - w_hi.astype(F32)).astype(BF16)
    full = lambda shape: pl.BlockSpec(shape, lambda i: (0, 0))
    return pl.pallas_call(
        functools.partial(_router_kernel, tm=tm),
        grid=(n // tm,),
        in_specs=[pl.BlockSpec((tm, d), lambda i: (i, 0)), full((1, d)),
                  full((d, LANES)), full((d, LANES)), full((1, LANES))],
        out_specs=[pl.BlockSpec((tm * chunks, LANES), lambda i: (i, 0)),
                   pl.BlockSpec((tm, LANES), lambda i: (i, 0)),
                   full((SUBLANES, LANES))],
        out_shape=[jax.ShapeDtypeStruct((n * chunks, LANES), F32),
                   jax.ShapeDtypeStruct((n, LANES), F32),
                   jax.ShapeDtypeStruct((SUBLANES, LANES), F32)],
        scratch_shapes=[pltpu.VMEM((1, LANES), F32)],
        compiler_params=_cparams("arbitrary"),
        name="moe_router",
    )(h, gain.reshape(1, d), w_hi, w_lo, bias)


def _row_copy(src, dst, sem):
    return pltpu.make_async_copy(src, dst, sem)


def _dispatch_kernel(zstart_ref, nu_ref, dest_ref, xn_ref, xs_ref, zero_ref, sem, *, tm, n_experts):
    tile = pl.program_id(0)
    n_blocks = xs_ref.shape[0] // MOE_BLOCK

    @pl.when(tile == 0)
    def _():
        zero_ref[...] = jnp.zeros(zero_ref.shape, F32)
        pad = lambda start: _row_copy(zero_ref, xs_ref.at[pl.ds(start, MOE_BLOCK)], sem)
        for e in range(n_experts):
            pad(zstart_ref[e]).start()
        for e in range(n_experts):
            pad(0).wait()

        def tail(i, carry):
            cp = pad(pl.multiple_of(i * MOE_BLOCK, MOE_BLOCK))
            cp.start()
            cp.wait()
            return carry

        lax.fori_loop(nu_ref[0], n_blocks, tail, 0)

    def issue(t, carry):
        _row_copy(xn_ref.at[t], xs_ref.at[dest_ref[0, 2 * t]], sem).start()
        _row_copy(xn_ref.at[t], xs_ref.at[dest_ref[0, 2 * t + 1]], sem).start()
        return carry

    lax.fori_loop(0, tm, issue, 0)

    def drain(t, carry):
        _row_copy(xn_ref.at[0], xs_ref.at[0], sem).wait()
        _row_copy(xn_ref.at[0], xs_ref.at[0], sem).wait()
        return carry

    lax.fori_loop(0, tm, drain, 0)


def moe_dispatch(xn_rows, dest, zstart, n_used, *, n_slots, tm):
    n, chunks, _ = xn_rows.shape
    dest3 = dest.reshape(n // tm, 1, 2 * tm)
    return pl.pallas_call(
        functools.partial(_dispatch_kernel, tm=tm, n_experts=zstart.shape[0]),
        grid_spec=pltpu.PrefetchScalarGridSpec(
            num_scalar_prefetch=2,
            grid=(n // tm,),
            in_specs=[pl.BlockSpec((None, 1, 2 * tm), lambda i, z, nu: (i, 0, 0),
                                   memory_space=pltpu.SMEM),
                      pl.BlockSpec((tm, chunks, LANES), lambda i, z, nu: (i, 0, 0))],
            out_specs=pl.BlockSpec(memory_space=pl.ANY),
            scratch_shapes=[pltpu.VMEM((MOE_BLOCK, chunks, LANES), F32),
                            pltpu.SemaphoreType.DMA(())]),
        out_shape=jax.ShapeDtypeStruct((n_slots, chunks, LANES), F32),
        compiler_params=_cparams("arbitrary"),
        name="moe_dispatch",
    )(zstart, n_used, dest3, xn_rows)


EXPERT_FF_CHUNK = 512


def _expert_kernel(be_ref, nu_ref, x_ref, wg_ref, wu_ref, wd_ref, y_ref, *, chunks):
    i = pl.program_id(0)

    @pl.when(i < nu_ref[0])
    def _():
        x = _from_row_tiles(x_ref, MOE_BLOCK, chunks).astype(BF16)
        y = None
        for c0 in range(0, wg_ref.shape[1], EXPERT_FF_CHUNK):
            cols = slice(c0, c0 + EXPERT_FF_CHUNK)
            gate = jnp.dot(x, wg_ref[:, cols].astype(BF16), preferred_element_type=F32)
            up = jnp.dot(x, wu_ref[:, cols].astype(BF16), preferred_element_type=F32)
            hidden = (gate * _sigmoid(gate) * up).astype(BF16)
            part = jnp.dot(hidden, wd_ref[cols, :].astype(BF16), preferred_element_type=F32)
            y = part if y is None else y + part
        _to_row_tiles(y_ref, y, MOE_BLOCK)

    @pl.when(i >= nu_ref[0])
    def _():
        y_ref[...] = jnp.zeros(y_ref.shape, F32)


def moe_experts(xs_flat, block_e, n_used, w_gate, w_up, w_down, *, layer, chunks):
    rows = xs_flat.shape[0]
    n_blocks = rows // (MOE_BLOCK * chunks)
    _, _, d, ff = w_gate.shape
    x_map = lambda i, be, nu: (jnp.minimum(i, nu[0] - 1), 0)
    w_map = lambda i, be, nu: (layer, be[i], 0, 0)
    w_spec = lambda shape, bufs: pl.BlockSpec((None, None) + shape, w_map,
                                              pipeline_mode=pl.Buffered(bufs))
    return pl.pallas_call(
        functools.partial(_expert_kernel, chunks=chunks),
        grid_spec=pltpu.PrefetchScalarGridSpec(
            num_scalar_prefetch=2,
            grid=(n_blocks,),
            in_specs=[pl.BlockSpec((MOE_BLOCK * chunks, LANES), x_map),
                      w_spec((d, ff), 1), w_spec((d, ff), 1), w_spec((ff, d), 2)],
            out_specs=pl.BlockSpec((MOE_BLOCK * chunks, LANES), lambda i, be, nu: (i, 0))),
        out_shape=jax.ShapeDtypeStruct((rows, LANES), F32),
        compiler_params=_cparams("arbitrary"),
        name="moe_experts",
    )(block_e, n_used, xs_flat, w_gate, w_up, w_down)


def _combine_kernel(dest_ref, next_ref, h_ref, info_ref, y_ref, o_ref, buf1, buf2, sems, *, tm, chunks):
    i = pl.program_id(0)
    slot = lax.rem(i, 2)

    def request(d_ref, s):
        def issue(t, carry):
            rows = pl.ds(pl.multiple_of(t * chunks, chunks), chunks)
            _row_copy(y_ref.at[d_ref[0, 2 * t]], buf1.at[s, rows], sems.at[s]).start()
            _row_copy(y_ref.at[d_ref[0, 2 * t + 1]], buf2.at[s, rows], sems.at[s]).start()
            return carry

        lax.fori_loop(0, tm, issue, 0)

    @pl.when(i == 0)
    def _():
        request(dest_ref, 0)

    @pl.when(i + 1 < pl.num_programs(0))
    def _():
        request(next_ref, 1 - slot)

    def drain(t, carry):
        rows = pl.ds(0, chunks)
        _row_copy(y_ref.at[0], buf1.at[slot, rows], sems.at[slot]).wait()
        _row_copy(y_ref.at[0], buf2.at[slot, rows], sems.at[slot]).wait()
        return carry

    lax.fori_loop(0, tm, drain, 0)

    info = info_ref[...]
    lane = lax.broadcasted_iota(jnp.int32, (tm, LANES), 1)
    g1 = jnp.sum(jnp.where(lane == INFO_GATE, info, 0.0), axis=1, keepdims=True)
    g2 = jnp.sum(jnp.where(lane == INFO_GATE + 1, info, 0.0), axis=1, keepdims=True)
    o_ref[...] = (h_ref[...] + g1 * _from_row_tiles(buf1.at[slot], tm, chunks)
                  + g2 * _from_row_tiles(buf2.at[slot], tm, chunks))


def moe_combine(h, info, y_rows, dest, *, tm):
    n, d = h.shape
    chunks = d // LANES
    tiles = n // tm
    dest3 = dest.reshape(tiles, 1, 2 * tm)
    dest_spec = lambda index_map: pl.BlockSpec((None, 1, 2 * tm), index_map, memory_space=pltpu.SMEM)
    return pl.pallas_call(
        functools.partial(_combine_kernel, tm=tm, chunks=chunks),
        grid=(tiles,),
        in_specs=[dest_spec(lambda i: (i, 0, 0)),
                  dest_spec(lambda i: (jnp.minimum(i + 1, tiles - 1), 0, 0)),
                  pl.BlockSpec((tm, d), lambda i: (i, 0)),
                  pl.BlockSpec((tm, LANES), lambda i: (i, 0)),
                  pl.BlockSpec(memory_space=pl.ANY)],
        out_specs=pl.BlockSpec((tm, d), lambda i: (i, 0)),
        out_shape=jax.ShapeDtypeStruct((n, d), F32),
        scratch_shapes=[pltpu.VMEM((2, tm * chunks, LANES), F32),
                        pltpu.VMEM((2, tm * chunks, LANES), F32),
                        pltpu.SemaphoreType.DMA((2,))],
        compiler_params=_cparams("arbitrary"),
        name="moe_combine",
    )(dest3, dest3, h, info, y_rows)


def hierarchical_moe(h, gain, group_w, group_b, expert_w, expert_b, w_gate, w_up, w_down,
                     *, layer, router_tm, dispatch_tm, combine_tm):
    n, d = h.shape
    chunks = d // LANES
    xn_flat, info, cnt = moe_router(h, gain, group_w, group_b, expert_w, expert_b, tm=router_tm)

    n_experts = w_gate.shape[1]
    n_blocks = -(-2 * n // MOE_BLOCK) + n_experts
    n_slots = n_blocks * MOE_BLOCK
    experts = info[:, INFO_E:INFO_E + 2].astype(jnp.int32)
    ranks = info[:, INFO_RANK:INFO_RANK + 2].astype(jnp.int32)
    counts = cnt[0, :n_experts].astype(jnp.int32)
    padded = (counts + MOE_BLOCK - 1) // MOE_BLOCK * MOE_BLOCK
    pends = jnp.cumsum(padded)
    pstarts = pends - padded
    dest = pstarts[experts] + ranks
    n_used = pends[-1] // MOE_BLOCK
    blk = jnp.arange(n_blocks, dtype=jnp.int32)
    block_e = jnp.sum((pends[None, :] <= (blk * MOE_BLOCK)[:, None]).astype(jnp.int32), axis=1)
    block_e = jnp.minimum(block_e, n_experts - 1)
    block_e = jnp.where(blk < n_used, block_e, block_e[n_used - 1]).astype(jnp.int32)
    zstart = jnp.minimum(pstarts + counts, n_slots - MOE_BLOCK).astype(jnp.int32)

    n_used = n_used.reshape(1).astype(jnp.int32)
    xs = moe_dispatch(xn_flat.reshape(n, chunks, LANES), dest, zstart, n_used,
                      n_slots=n_slots, tm=dispatch_tm)
    y = moe_experts(xs.reshape(n_slots * chunks, LANES), block_e, n_used,
                    w_gate, w_up, w_down, layer=layer, chunks=chunks)
    return moe_combine(h, info, y.reshape(n_slots, chunks, LANES), dest, tm=combine_tm)


def kernel(x, even_norm, even_w_in, conv_w, conv_b, conv_norm_g, conv_norm_b, sb_q_norm, sb_k_norm,
           even_w_out, odd_norm, odd_w_in, fox_forget_b, fox_q_norm, fox_k_norm, sc_w, odd_w_out,
           moe_norm, router_group_w, router_group_b, router_expert_w, router_expert_b,
           expert_w_gate, expert_w_up, expert_w_down):
    b, s, d = x.shape
    n = b * s
    h = x.reshape(n, d)
    sb_width = SB_HEADS * HEAD_DIM
    fox_width = FOX_HEADS * HEAD_DIM
    moe_tiles = dict(router_tm=512, dispatch_tm=256, combine_tm=256)

    def moe(h, layer):
        return hierarchical_moe(
            h, moe_norm[layer], router_group_w[layer], router_group_b[layer],
            router_expert_w[layer], router_expert_b[layer],
            expert_w_gate, expert_w_up, expert_w_down, layer=layer, **moe_tiles)

    u = norm_matmul(h, even_norm[0], even_w_in[0].astype(BF16), tm=1024, tn=1024)
    u3 = u.reshape(b, s, -1)
    a = conformer_conv(u3, conv_w[0], conv_b[0], conv_norm_g[0], conv_norm_b[0], ts=256)
    o = stick_breaking_attention(u3, sb_q_norm[0], sb_k_norm[0], col0=2 * CONV_CH, heads=SB_HEADS)
    h = out_proj_residual(a.reshape(n, CONV_CH), o.reshape(n, sb_width),
                          even_w_out[0].astype(BF16), h, tm=256)
    h = moe(h, 0)

    n_qkv = 3 * fox_width
    w_in = odd_w_in[0]
    w_in = jnp.concatenate(
        [w_in[:, :n_qkv], w_in[:, n_qkv + FOX_HEADS:], w_in[:, n_qkv:n_qkv + FOX_HEADS],
         jnp.zeros((d, LANES - FOX_HEADS), F32)], axis=1).astype(BF16)
    u = norm_matmul(h, odd_norm[0], w_in, tm=1024, tn=896)
    u3 = u.reshape(b, s, -1)
    c_col = forget_cumsum(u3, fox_forget_b[0], col0=n_qkv + 3 * SC_CH, tc=512)
    c_row = jnp.transpose(c_col[:, :, :FOX_HEADS], (0, 2, 1)).reshape(b, FOX_HEADS, 1, s)
    o = forgetting_attention(u3, c_col, c_row, fox_q_norm[0], fox_k_norm[0], heads=FOX_HEADS)
    y = short_conv(u3, sc_w[0], col0=n_qkv, ts=256)
    h = out_proj_residual(o.reshape(n, fox_width), y.reshape(n, SC_CH),
                          odd_w_out[0].astype(BF16), h, tm=256)
    h = moe(h, 1)
    return h.reshape(b, s, d)
```

```python
import functools

import jax
import jax.numpy as jnp
from jax import lax
from jax.experimental import pallas as pl
from jax.experimental.pallas import tpu as pltpu

F32 = jnp.float32
BF16 = jnp.bfloat16

HEAD_DIM = 128
CONV_CH = 1024
CONV_WIDTH = 31
SB_HEADS = 8
FOX_HEADS = 8
SC_CH = 1024
SC_WIDTH = 3
N_GROUPS = 4
EXPERTS_PER_GROUP = 8
N_EXPERTS = N_GROUPS * EXPERTS_PER_GROUP
MOE_BLOCK = 256
RMS_EPS = 1e-6
LN_EPS = 1e-5

LANES = 128
SUBLANES = 8
VMEM_LIMIT = 56 * 1024 * 1024


def _cparams(*sem):
    return pltpu.CompilerParams(dimension_semantics=sem, vmem_limit_bytes=VMEM_LIMIT)


def _sigmoid(x):
    return 1.0 / (1.0 + jnp.exp(-x))


def _softplus_neg_abs(z):
    return jnp.log(1.0 + jnp.exp(-jnp.abs(z)))


def _rms(x, g):
    ms = jnp.mean(x * x, axis=-1, keepdims=True)
    return x * lax.rsqrt(ms + RMS_EPS) * g


def _split_bf16(x, parts):
    out = []
    r = x
    for _ in range(parts - 1):
        p = r.astype(BF16)
        out.append(p)
        r = r - p.astype(F32)
    out.append(r.astype(BF16))
    return out


def _norm_matmul_kernel(x_ref, g_ref, w_ref, o_ref, xn_ref):
    @pl.when(pl.program_id(1) == 0)
    def _():
        xn_ref[...] = _rms(x_ref[...], g_ref[...]).astype(BF16)

    o_ref[...] = jnp.dot(xn_ref[...], w_ref[...], preferred_element_type=F32)


def norm_matmul(x, gain, w, *, tm, tn):
    n, d = x.shape
    f = w.shape[1]
    return pl.pallas_call(
        _norm_matmul_kernel,
        grid=(n // tm, f // tn),
        in_specs=[pl.BlockSpec((tm, d), lambda i, j: (i, 0)),
                  pl.BlockSpec((1, d), lambda i, j: (0, 0)),
                  pl.BlockSpec((d, tn), lambda i, j: (0, j))],
        out_specs=pl.BlockSpec((tm, tn), lambda i, j: (i, j)),
        out_shape=jax.ShapeDtypeStruct((n, f), F32),
        scratch_shapes=[pltpu.VMEM((tm, d), BF16)],
        compiler_params=_cparams("parallel", "arbitrary"),
        name="norm_matmul",
    )(x, gain.reshape(1, d), w)


def _conv_tile(abuf, w_ref, bias_ref, cbuf, *, ts, halo, width, rows):
    n_ch = cbuf.shape[1]
    off = halo - (width - 1)

    def chan_body(c, carry):
        lanes = pl.ds(pl.multiple_of(c * LANES, LANES), LANES)
        wc = w_ref[:, lanes]
        for r0 in range(0, ts, rows):
            if bias_ref is None:
                acc = jnp.zeros((rows, LANES), F32)
            else:
                acc = jnp.broadcast_to(bias_ref[:, lanes], (rows, LANES))
            win = abuf[pl.ds(r0, rows + halo), lanes]
            for shift in range(SUBLANES):
                taps = [k for k in range(width) if (off + k) % SUBLANES == shift]
                if not taps:
                    continue
                shifted = win if shift == 0 else pltpu.roll(win, rows + halo - shift, axis=0)
                for k in taps:
                    base = off + k - shift
                    acc = acc + wc[k:k + 1, :] * shifted[base:base + rows]
            cbuf[pl.ds(r0, rows), lanes] = acc
        return carry

    lax.fori_loop(0, n_ch // LANES, chan_body, 0)


def _carry_halo(abuf, *, ts, halo):
    s = pl.program_id(1)

    @pl.when(s == 0)
    def _():
        abuf[0:halo, :] = jnp.zeros((halo, abuf.shape[1]), F32)

    @pl.when(s > 0)
    def _():
        abuf[0:halo, :] = abuf[ts:ts + halo, :]


CONV_HALO = 32
SC_HALO = 8


def _conformer_kernel(av_ref, ag_ref, w_ref, cb_ref, lg_ref, lb_ref, o_ref, abuf, cbuf, *, ts):
    _carry_halo(abuf, ts=ts, halo=CONV_HALO)
    abuf[CONV_HALO:CONV_HALO + ts, :] = av_ref[...] * _sigmoid(ag_ref[...])
    _conv_tile(abuf, w_ref, cb_ref, cbuf, ts=ts, halo=CONV_HALO, width=CONV_WIDTH, rows=64)
    y = cbuf[...]
    mu = jnp.mean(y, axis=-1, keepdims=True)
    yc = y - mu
    var = jnp.mean(yc * yc, axis=-1, keepdims=True)
    yn = yc * lax.rsqrt(var + LN_EPS) * lg_ref[...] + lb_ref[...]
    o_ref[...] = (yn * _sigmoid(yn)).astype(BF16)


def conformer_conv(u3, conv_w, conv_b, ln_g, ln_b, *, ts):
    b, s, _ = u3.shape
    c = conv_w.shape[1]
    w_pad = jnp.zeros((CONV_HALO, c), F32).at[:CONV_WIDTH].set(conv_w)
    row = lambda a: a.reshape(1, c)
    full = lambda shape: pl.BlockSpec(shape, lambda bi, si: (0, 0))
    return pl.pallas_call(
        functools.partial(_conformer_kernel, ts=ts),
        grid=(b, s // ts),
        in_specs=[pl.BlockSpec((None, ts, c), lambda bi, si: (bi, si, 0)),
                  pl.BlockSpec((None, ts, c), lambda bi, si: (bi, si, 1)),
                  full((CONV_HALO, c)), full((1, c)), full((1, c)), full((1, c))],
        out_specs=pl.BlockSpec((None, ts, c), lambda bi, si: (bi, si, 0)),
        out_shape=jax.ShapeDtypeStruct((b, s, c), BF16),
        scratch_shapes=[pltpu.VMEM((CONV_HALO + ts, c), F32), pltpu.VMEM((ts, c), F32)],
        compiler_params=_cparams("parallel", "arbitrary"),
        name="conformer_conv",
    )(u3, u3, w_pad, row(conv_b), row(ln_g), row(ln_b))


def _short_conv_kernel(bg_ref, cg_ref, xv_ref, w_ref, o_ref, abuf, cbuf, *, ts):
    _carry_halo(abuf, ts=ts, halo=SC_HALO)
    abuf[SC_HALO:SC_HALO + ts, :] = cg_ref[...] * xv_ref[...]
    _conv_tile(abuf, w_ref, None, cbuf, ts=ts, halo=SC_HALO, width=SC_WIDTH, rows=64)
    o_ref[...] = (bg_ref[...] * cbuf[...]).astype(BF16)


def short_conv(u3, sc_w, *, col0, ts):
    b, s, _ = u3.shape
    c = sc_w.shape[1]
    j0 = col0 // c
    w_pad = jnp.zeros((SUBLANES, c), F32).at[:SC_WIDTH].set(sc_w)
    return pl.pallas_call(
        functools.partial(_short_conv_kernel, ts=ts),
        grid=(b, s // ts),
        in_specs=[pl.BlockSpec((None, ts, c), lambda bi, si: (bi, si, j0)),
                  pl.BlockSpec((None, ts, c), lambda bi, si: (bi, si, j0 + 1)),
                  pl.BlockSpec((None, ts, c), lambda bi, si: (bi, si, j0 + 2)),
                  pl.BlockSpec((SUBLANES, c), lambda bi, si: (0, 0))],
        out_specs=pl.BlockSpec((None, ts, c), lambda bi, si: (bi, si, 0)),
        out_shape=jax.ShapeDtypeStruct((b, s, c), BF16),
        scratch_shapes=[pltpu.VMEM((SC_HALO + ts, c), F32), pltpu.VMEM((ts, c), F32)],
        compiler_params=_cparams("parallel", "arbitrary"),
        name="short_conv",
    )(u3, u3, u3, w_pad)


ATTN_HEADS = 2
ATTN_T = 256


def _head(e):
    return slice(e * HEAD_DIM, (e + 1) * HEAD_DIM)


def _prep_kv(k_ref, v_ref, kg_ref, kn_ref, vb_ref):
    @pl.when(pl.program_id(2) == 0)
    def _():
        for e in range(ATTN_HEADS):
            kn_ref[:, _head(e)] = _rms(k_ref[:, _head(e)], kg_ref[...]).astype(BF16)
        vb_ref[...] = v_ref[...].astype(BF16)


def _prep_q(q_ref, qg_ref, e):
    return (_rms(q_ref[:, _head(e)], qg_ref[...]) * (HEAD_DIM ** -0.5)).astype(BF16)


def _qk(qn, kb):
    return lax.dot_general(qn, kb, (((1,), (1,)), ((), ())), preferred_element_type=F32)


SB_SUB = 128
SB_CUTOFF = -110.0


def _sb_kernel(q_ref, k_ref, v_ref, qg_ref, kg_ref, o_ref, kn_ref, vb_ref, acc_ref, r_ref):
    t = ATTN_T
    i = pl.program_id(2)
    _prep_kv(k_ref, v_ref, kg_ref, kn_ref, vb_ref)
    qn = [_prep_q(q_ref, qg_ref, e) for e in range(ATTN_HEADS)]
    acc_ref[...] = jnp.zeros(acc_ref.shape, F32)
    r_ref[...] = jnp.zeros(r_ref.shape, F32)

    jr = lax.broadcasted_iota(jnp.int32, (t, t), 0)
    sc = lax.broadcasted_iota(jnp.int32, (t, t), 1)
    later = jnp.where((jr > sc) & (jr // SB_SUB == sc // SB_SUB), 1.0, 0.0).astype(BF16)
    strict = sc < jr

    def local_terms(c, masked):
        ks = pl.multiple_of(c * t, t)
        out = []
        for e in range(ATTN_HEADS):
            z = _qk(qn[e], kn_ref[pl.ds(ks, t), _head(e)])
            log_beta = jnp.minimum(z, 0.0) - _softplus_neg_abs(z)
            log_om = log_beta - z
            if masked:
                log_om = jnp.where(strict, log_om, 0.0)
            hi, lo = _split_bf16(log_om, 2)
            suffix = (jnp.dot(hi, later, preferred_element_type=F32)
                      + jnp.dot(lo, later, preferred_element_type=F32))
            out.append((log_beta + suffix,
                        jnp.sum(log_om[:, SB_SUB:], axis=1, keepdims=True),
                        jnp.sum(log_om[:, :SB_SUB], axis=1, keepdims=True)))
        return tuple(out)

    def accumulate(c, masked, terms):
        ks = pl.multiple_of(c * t, t)
        for e in range(ATTN_HEADS):
            base, tot_near, tot_far = terms[e]
            r = r_ref[:, _head(e)]
            w = jnp.exp(base + jnp.concatenate([r + tot_near, r], axis=1))
            if masked:
                w = jnp.where(strict, w, 0.0)
            acc_ref[:, _head(e)] += jnp.dot(w.astype(BF16), vb_ref[pl.ds(ks, t), _head(e)],
                                            preferred_element_type=F32)
            r_ref[:, _head(e)] = r + (tot_near + tot_far)

    accumulate(i, True, local_terms(i, True))

    def more(carry):
        n, r_max, _ = carry
        return (n < i) & (r_max > SB_CUTOFF)

    def body(carry):
        n, _, terms = carry
        c = i - 1 - n
        ahead = local_terms(jnp.maximum(c - 1, 0), False)
        accumulate(c, False, terms)
        return n + 1, jnp.max(r_ref[...]), ahead

    first = local_terms(jnp.maximum(i - 1, 0), False)
    lax.while_loop(more, body, (jnp.int32(0), jnp.max(r_ref[...]), first))
    o_ref[...] = acc_ref[...].astype(BF16)


def _attn_specs(s, j0, heads):
    w = ATTN_HEADS * HEAD_DIM
    g0 = j0 // ATTN_HEADS
    gh = heads // ATTN_HEADS
    q_spec = pl.BlockSpec((None, ATTN_T, w), lambda bi, hi, qi: (bi, qi, g0 + hi))
    k_spec = pl.BlockSpec((None, s, w), lambda bi, hi, qi: (bi, 0, g0 + gh + hi))
    v_spec = pl.BlockSpec((None, s, w), lambda bi, hi, qi: (bi, 0, g0 + 2 * gh + hi))
    o_spec = pl.BlockSpec((None, ATTN_T, w), lambda bi, hi, qi: (bi, qi, hi))
    return q_spec, k_spec, v_spec, o_spec


def stick_breaking_attention(u3, q_gain, k_gain, *, col0, heads):
    b, s, _ = u3.shape
    w = ATTN_HEADS * HEAD_DIM
    q_spec, k_spec, v_spec, o_spec = _attn_specs(s, col0 // HEAD_DIM, heads)
    gain = lambda g: g.reshape(1, HEAD_DIM)
    full = lambda shape: pl.BlockSpec(shape, lambda bi, hi, qi: (0, 0))
    return pl.pallas_call(
        _sb_kernel,
        grid=(b, heads // ATTN_HEADS, s // ATTN_T),
        in_specs=[q_spec, k_spec, v_spec, full((1, HEAD_DIM)), full((1, HEAD_DIM))],
        out_specs=o_spec,
        out_shape=jax.ShapeDtypeStruct((b, s, heads * HEAD_DIM), BF16),
        scratch_shapes=[pltpu.VMEM((s, w), BF16), pltpu.VMEM((s, w), BF16),
                        pltpu.VMEM((ATTN_T, w), F32), pltpu.VMEM((ATTN_T, w), F32)],
        compiler_params=_cparams("parallel", "parallel", "arbitrary"),
        name="stick_breaking_attention",
    )(u3, u3, u3, gain(q_gain), gain(k_gain))


FOX_PREP = 512


def _fox_kernel(q_ref, k_ref, v_ref, cc_ref, cr_ref, qg_ref, kg_ref, o_ref,
                kn_ref, vt_ref, cs_ref, acc_ref):
    t = ATTN_T
    hg = pl.program_id(1)
    i = pl.program_id(2)
    s = k_ref.shape[0]
    rows = lambda e: slice(e * HEAD_DIM, (e + 1) * HEAD_DIM)

    @pl.when(i == 0)
    def _():
        lane = lax.broadcasted_iota(jnp.int32, (FOX_PREP, LANES), 1)
        for e in range(ATTN_HEADS):
            kn_ref[:, _head(e)] = _rms(k_ref[:, _head(e)], kg_ref[...]).astype(BF16)
            for c0 in range(0, s, FOX_PREP):
                blk = slice(c0, c0 + FOX_PREP)
                vt_ref[rows(e), blk] = v_ref[blk, _head(e)].T.astype(BF16)
                col = jnp.sum(jnp.where(lane == hg * ATTN_HEADS + e, cc_ref[blk, :], 0.0),
                              axis=1, keepdims=True)
                cs_ref[blk, _head(e)] = jnp.broadcast_to(col, (FOX_PREP, HEAD_DIM))

    qn = [_prep_q(q_ref, qg_ref, e) for e in range(ATTN_HEADS)]
    ct = [cr_ref[e, :, pl.ds(pl.multiple_of(i * t, t), t)] for e in range(ATTN_HEADS)]
    acc_ref[...] = jnp.zeros(acc_ref.shape, F32)
    kpos = lax.broadcasted_iota(jnp.int32, (t, t), 0)
    qpos = lax.broadcasted_iota(jnp.int32, (t, t), 1)

    def scores(j):
        ks = pl.multiple_of(j * t, t)
        return tuple(_qk(kn_ref[pl.ds(ks, t), _head(e)], qn[e])
                     for e in range(ATTN_HEADS))

    def chunk(j, masked, stats, qk):
        ks = pl.multiple_of(j * t, t)
        out = []
        for e in range(ATTN_HEADS):
            m_prev, l_prev = stats[e]
            cs = cs_ref[pl.ds(ks, t), _head(e)]
            zt = qk[e] + (ct[e] - jnp.concatenate([cs] * (t // HEAD_DIM), axis=1))
            if masked:
                zt = jnp.where(kpos <= qpos, zt, -jnp.inf)
            m_new = jnp.maximum(m_prev, jnp.max(zt, axis=0, keepdims=True))
            alpha = jnp.exp(m_prev - m_new)
            pt = jnp.exp(zt - m_new)
            l_new = alpha * l_prev + jnp.sum(pt, axis=0, keepdims=True)
            acc_ref[rows(e), :] = alpha * acc_ref[rows(e), :] + jnp.dot(
                vt_ref[rows(e), pl.ds(ks, t)], pt.astype(BF16), preferred_element_type=F32)
            out.append((m_new, l_new))
        return tuple(out)

    init = tuple((jnp.full((1, t), -jnp.inf, F32), jnp.zeros((1, t), F32))
                 for _ in range(ATTN_HEADS))

    def body(j, carry):
        stats, qk = carry
        nxt = scores(j + 1)
        return chunk(j, False, stats, qk), nxt

    stats, qk = lax.fori_loop(0, i, body, (init, scores(0)))
    stats = chunk(i, True, stats, qk)
    for e in range(ATTN_HEADS):
        o_ref[:, _head(e)] = (acc_ref[rows(e), :] / stats[e][1]).T.astype(BF16)


def forgetting_attention(u3, c_col, c_row, q_gain, k_gain, *, heads):
    b, s, _ = u3.shape
    w = ATTN_HEADS * HEAD_DIM
    q_spec, k_spec, v_spec, o_spec = _attn_specs(s, 0, heads)
    gain = lambda g: g.reshape(1, HEAD_DIM)
    full = lambda shape: pl.BlockSpec(shape, lambda bi, hi, qi: (0, 0))
    return pl.pallas_call(
        _fox_kernel,
        grid=(b, heads // ATTN_HEADS, s // ATTN_T),
        in_specs=[q_spec, k_spec, v_spec,
                  pl.BlockSpec((None, s, LANES), lambda bi, hi, qi: (bi, 0, 0)),
                  pl.BlockSpec((None, ATTN_HEADS, 1, s), lambda bi, hi, qi: (bi, hi, 0, 0)),
                  full((1, HEAD_DIM)), full((1, HEAD_DIM))],
        out_specs=o_spec,
        out_shape=jax.ShapeDtypeStruct((b, s, heads * HEAD_DIM), BF16),
        scratch_shapes=[pltpu.VMEM((s, w), BF16), pltpu.VMEM((w, s), BF16),
                        pltpu.VMEM((s, w), F32), pltpu.VMEM((w, ATTN_T), F32)],
        compiler_params=_cparams("parallel", "parallel", "arbitrary"),
        name="forgetting_attention",
    )(u3, u3, u3, c_col, c_row, gain(q_gain), gain(k_gain))


def _forget_cumsum_kernel(f_ref, b_ref, o_ref, carry_ref, *, tc):
    @pl.when(pl.program_id(1) == 0)
    def _():
        carry_ref[...] = jnp.zeros((1, LANES), F32)

    x = f_ref[...] + b_ref[...]
    log_f = jnp.minimum(x, 0.0) - _softplus_neg_abs(x)
    r = lax.broadcasted_iota(jnp.int32, (tc, tc), 0)
    c = lax.broadcasted_iota(jnp.int32, (tc, tc), 1)
    tri = jnp.where(r >= c, 1.0, 0.0).astype(BF16)
    cs = carry_ref[...]
    for part in _split_bf16(log_f, 3):
        cs = cs + jnp.dot(tri, part, preferred_element_type=F32)
    o_ref[...] = cs
    carry_ref[...] = cs[tc - 1:tc, :]


def forget_cumsum(u3, forget_b, *, col0, tc):
    b, s, _ = u3.shape
    j0 = col0 // LANES
    b_pad = jnp.zeros((1, LANES), F32).at[0, :forget_b.shape[0]].set(forget_b)
    return pl.pallas_call(
        functools.partial(_forget_cumsum_kernel, tc=tc),
        grid=(b, s // tc),
        in_specs=[pl.BlockSpec((None, tc, LANES), lambda bi, si: (bi, si, j0)),
                  pl.BlockSpec((1, LANES), lambda bi, si: (0, 0))],
        out_specs=pl.BlockSpec((None, tc, LANES), lambda bi, si: (bi, si, 0)),
        out_shape=jax.ShapeDtypeStruct((b, s, LANES), F32),
        scratch_shapes=[pltpu.VMEM((1, LANES), F32)],
        compiler_params=_cparams("parallel", "arbitrary"),
        name="forget_cumsum",
    )(u3, b_pad)


def _out_proj_kernel(a_ref, b_ref, w1_ref, w2_ref, h_ref, o_ref):
    o_ref[...] = (h_ref[...]
                  + jnp.dot(a_ref[...], w1_ref[...], preferred_element_type=F32)
                  + jnp.dot(b_ref[...], w2_ref[...], preferred_element_type=F32))


def out_proj_residual(a, b, w, h, *, tm):
    n, ka = a.shape
    kb = b.shape[1]
    d = w.shape[1]
    return pl.pallas_call(
        _out_proj_kernel,
        grid=(n // tm,),
        in_specs=[pl.BlockSpec((tm, ka), lambda i: (i, 0)),
                  pl.BlockSpec((tm, kb), lambda i: (i, 0)),
                  pl.BlockSpec((ka, d), lambda i: (0, 0)),
                  pl.BlockSpec((kb, d), lambda i: (0, 0)),
                  pl.BlockSpec((tm, d), lambda i: (i, 0))],
        out_specs=pl.BlockSpec((tm, d), lambda i: (i, 0)),
        out_shape=jax.ShapeDtypeStruct((n, d), F32),
        compiler_params=_cparams("parallel"),
        name="out_proj_residual",
    )(a, b, w[:ka], w[ka:], h)


INFO_E, INFO_GATE, INFO_RANK = 0, 2, 4


def _to_row_tiles(dst_ref, x, n_rows):
    chunks = x.shape[1] // LANES
    for c in range(chunks):
        dst_ref[pl.ds(c, n_rows, stride=chunks), :] = x[:, c * LANES:(c + 1) * LANES]


def _from_row_tiles(src_ref, n_rows, chunks, stride=None):
    stride = stride or chunks
    return jnp.concatenate(
        [src_ref[pl.ds(c, n_rows, stride=stride), :] for c in range(chunks)], axis=1)


def _lane_pick(x, lane, idx):
    return jnp.sum(jnp.where(lane == idx, x, 0.0), axis=1, keepdims=True)


def _router_kernel(h_ref, g_ref, wh_ref, wl_ref, b_ref, xn_ref, info_ref, cnt_ref, carry_ref, *, tm):
    @pl.when(pl.program_id(0) == 0)
    def _():
        carry_ref[...] = jnp.zeros((1, LANES), F32)

    xn = _rms(h_ref[...], g_ref[...])
    _to_row_tiles(xn_ref, xn, tm)

    xh, xl = _split_bf16(xn, 2)
    wh = wh_ref[...]
    logits = (jnp.dot(xh, wh, preferred_element_type=F32)
              + jnp.dot(xl, wh, preferred_element_type=F32)
              + jnp.dot(xh, wl_ref[...], preferred_element_type=F32)) + b_ref[...]

    lane = lax.broadcasted_iota(jnp.int32, (tm, LANES), 1).astype(F32)
    neg = -jnp.inf
    big = float(LANES)
    gl = jnp.where(lane < N_GROUPS, logits, neg)
    gmax = jnp.max(gl, axis=1, keepdims=True)
    g_top_p = 1.0 / jnp.sum(jnp.exp(gl - gmax), axis=1, keepdims=True)
    g_idx = jnp.min(jnp.where(gl == gmax, lane, big), axis=1, keepdims=True)

    lo = N_GROUPS + EXPERTS_PER_GROUP * g_idx
    el = jnp.where((lane >= lo) & (lane < lo + EXPERTS_PER_GROUP), logits, neg)
    m1 = jnp.max(el, axis=1, keepdims=True)
    i1 = jnp.min(jnp.where(el == m1, lane, big), axis=1, keepdims=True)
    el2 = jnp.where(lane == i1, neg, el)
    m2 = jnp.max(el2, axis=1, keepdims=True)
    i2 = jnp.min(jnp.where(el2 == m2, lane, big), axis=1, keepdims=True)
    ratio = jnp.exp(m2 - m1)
    p1 = 1.0 / (1.0 + ratio)
    p2 = ratio * p1
    e1 = i1 - N_GROUPS
    e2 = i2 - N_GROUPS

    onehot = jnp.where((lane == e1) | (lane == e2), 1.0, 0.0)
    r = lax.broadcasted_iota(jnp.int32, (tm, tm), 0)
    c = lax.broadcasted_iota(jnp.int32, (tm, tm), 1)
    before = jnp.where(r > c, 1.0, 0.0).astype(BF16)
    cnt = jnp.dot(before, onehot.astype(BF16), preferred_element_type=F32) + carry_ref[...]
    rank1 = _lane_pick(cnt, lane, e1)
    rank2 = _lane_pick(cnt, lane, e2)
    total = carry_ref[...] + jnp.sum(onehot, axis=0, keepdims=True)
    carry_ref[...] = total
    cnt_ref[...] = jnp.broadcast_to(total, (SUBLANES, LANES))

    info = jnp.zeros((tm, LANES), F32)
    for k, val in ((INFO_E, e1), (INFO_E + 1, e2), (INFO_GATE, g_top_p * p1),
                   (INFO_GATE + 1, g_top_p * p2), (INFO_RANK, rank1), (INFO_RANK + 1, rank2)):
        info = jnp.where(lane == k, val, info)
    info_ref[...] = info


def moe_router(h, gain, group_w, group_b, expert_w, expert_b, *, tm):
    n, d = h.shape
    chunks = d // LANES
    n_logits = N_GROUPS + N_EXPERTS
    w = jnp.zeros((d, LANES), F32).at[:, :N_GROUPS].set(group_w).at[:, N_GROUPS:n_logits].set(expert_w)
    bias = jnp.zeros((1, LANES), F32).at[0, :N_GROUPS].set(group_b).at[0, N_GROUPS:n_logits].set(expert_b)
    w_hi = w.astype(BF16)
    w_lo = (w - w_hi.astype(F32)).astype(BF16)
    full = lambda shape: pl.BlockSpec(shape, lambda i: (0, 0))
    return pl.pallas_call(
        functools.partial(_router_kernel, tm=tm),
        grid=(n // tm,),
        in_specs=[pl.BlockSpec((tm, d), lambda i: (i, 0)), full((1, d)),
                  full((d, LANES)), full((d, LANES)), full((1, LANES))],
        out_specs=[pl.BlockSpec((tm * chunks, LANES), lambda i: (i, 0)),
                   pl.BlockSpec((tm, LANES), lambda i: (i, 0)),
                   full((SUBLANES, LANES))],
        out_shape=[jax.ShapeDtypeStruct((n * chunks, LANES), F32),
                   jax.ShapeDtypeStruct((n, LANES), F32),
                   jax.ShapeDtypeStruct((SUBLANES, LANES), F32)],
        scratch_shapes=[pltpu.VMEM((1, LANES), F32)],
        compiler_params=_cparams("arbitrary"),
        name="moe_router",
    )(h, gain.reshape(1, d), w_hi, w_lo, bias)


def _row_copy(src, dst, sem):
    return pltpu.make_async_copy(src, dst, sem)


def _dispatch_kernel(zstart_ref, nu_ref, dest_ref, xn_ref, xs_ref, zero_ref, sem, *, tm, n_experts):
    tile = pl.program_id(0)
    n_blocks = xs_ref.shape[0] // MOE_BLOCK

    @pl.when(tile == 0)
    def _():
        zero_ref[...] = jnp.zeros(zero_ref.shape, F32)
        pad = lambda start: _row_copy(zero_ref, xs_ref.at[pl.ds(start, MOE_BLOCK)], sem)
        for e in range(n_experts):
            pad(zstart_ref[e]).start()
        for e in range(n_experts):
            pad(0).wait()

        def tail(i, carry):
            cp = pad(pl.multiple_of(i * MOE_BLOCK, MOE_BLOCK))
            cp.start()
            cp.wait()
            return carry

        lax.fori_loop(nu_ref[0], n_blocks, tail, 0)

    def issue(t, carry):
        _row_copy(xn_ref.at[t], xs_ref.at[dest_ref[0, 2 * t]], sem).start()
        _row_copy(xn_ref.at[t], xs_ref.at[dest_ref[0, 2 * t + 1]], sem).start()
        return carry

    lax.fori_loop(0, tm, issue, 0)

    def drain(t, carry):
        _row_copy(xn_ref.at[0], xs_ref.at[0], sem).wait()
        _row_copy(xn_ref.at[0], xs_ref.at[0], sem).wait()
        return carry

    lax.fori_loop(0, tm, drain, 0)


def moe_dispatch(xn_rows, dest, zstart, n_used, *, n_slots, tm):
    n, chunks, _ = xn_rows.shape
    dest3 = dest.reshape(n // tm, 1, 2 * tm)
    return pl.pallas_call(
        functools.partial(_dispatch_kernel, tm=tm, n_experts=zstart.shape[0]),
        grid_spec=pltpu.PrefetchScalarGridSpec(
            num_scalar_prefetch=2,
            grid=(n // tm,),
            in_specs=[pl.BlockSpec((None, 1, 2 * tm), lambda i, z, nu: (i, 0, 0),
                                   memory_space=pltpu.SMEM),
                      pl.BlockSpec((tm, chunks, LANES), lambda i, z, nu: (i, 0, 0))],
            out_specs=pl.BlockSpec(memory_space=pl.ANY),
            scratch_shapes=[pltpu.VMEM((MOE_BLOCK, chunks, LANES), F32),
                            pltpu.SemaphoreType.DMA(())]),
        out_shape=jax.ShapeDtypeStruct((n_slots, chunks, LANES), F32),
        compiler_params=_cparams("arbitrary"),
        name="moe_dispatch",
    )(zstart, n_used, dest3, xn_rows)


EXPERT_FF_CHUNK = 512


def _expert_kernel(be_ref, nu_ref, x_ref, wg_ref, wu_ref, wd_ref, y_ref, *, chunks):
    i = pl.program_id(0)

    @pl.when(i < nu_ref[0])
    def _():
        x = _from_row_tiles(x_ref, MOE_BLOCK, chunks).astype(BF16)
        y = None
        for c0 in range(0, wg_ref.shape[1], EXPERT_FF_CHUNK):
            cols = slice(c0, c0 + EXPERT_FF_CHUNK)
            gate = jnp.dot(x, wg_ref[:, cols].astype(BF16), preferred_element_type=F32)
            up = jnp.dot(x, wu_ref[:, cols].astype(BF16), preferred_element_type=F32)
            hidden = (gate * _sigmoid(gate) * up).astype(BF16)
            part = jnp.dot(hidden, wd_ref[cols, :].astype(BF16), preferred_element_type=F32)
            y = part if y is None else y + part
        _to_row_tiles(y_ref, y, MOE_BLOCK)

    @pl.when(i >= nu_ref[0])
    def _():
        y_ref[...] = jnp.zeros(y_ref.shape, F32)


def moe_experts(xs_flat, block_e, n_used, w_gate, w_up, w_down, *, layer, chunks):
    rows = xs_flat.shape[0]
    n_blocks = rows // (MOE_BLOCK * chunks)
    _, _, d, ff = w_gate.shape
    x_map = lambda i, be, nu: (jnp.minimum(i, nu[0] - 1), 0)
    w_map = lambda i, be, nu: (layer, be[i], 0, 0)
    w_spec = lambda shape, bufs: pl.BlockSpec((None, None) + shape, w_map,
                                              pipeline_mode=pl.Buffered(bufs))
    return pl.pallas_call(
        functools.partial(_expert_kernel, chunks=chunks),
        grid_spec=pltpu.PrefetchScalarGridSpec(
            num_scalar_prefetch=2,
            grid=(n_blocks,),
            in_specs=[pl.BlockSpec((MOE_BLOCK * chunks, LANES), x_map),
                      w_spec((d, ff), 2), w_spec((d, ff), 1), w_spec((ff, d), 2)],
            out_specs=pl.BlockSpec((MOE_BLOCK * chunks, LANES), lambda i, be, nu: (i, 0))),
        out_shape=jax.ShapeDtypeStruct((rows, LANES), F32),
        compiler_params=_cparams("arbitrary"),
        name="moe_experts",
    )(block_e, n_used, xs_flat, w_gate, w_up, w_down)


COMBINE_ROW_STRIDE = 24


def _combine_kernel(d0_ref, d1_ref, d2_ref, h_ref, info_ref, y_ref, o_ref, buf1, buf2, sems, *, tm, chunks):
    p = pl.program_id(0)
    stride = COMBINE_ROW_STRIDE

    def request(d_ref, s):
        def issue(t, carry):
            rows = pl.ds(pl.multiple_of(t * stride, SUBLANES), chunks)
            _row_copy(y_ref.at[d_ref[0, 2 * t]], buf1.at[s, rows], sems.at[s]).start()
            _row_copy(y_ref.at[d_ref[0, 2 * t + 1]], buf2.at[s, rows], sems.at[s]).start()
            return carry

        lax.fori_loop(0, tm, issue, 0)

    def combine(s):
        def drain(t, carry):
            rows = pl.ds(0, chunks)
            _row_copy(y_ref.at[0], buf1.at[s, rows], sems.at[s]).wait()
            _row_copy(y_ref.at[0], buf2.at[s, rows], sems.at[s]).wait()
            return carry

        lax.fori_loop(0, tm, drain, 0)
        tile = slice(s * tm, (s + 1) * tm)
        info = info_ref[tile, :]
        lane = lax.broadcasted_iota(jnp.int32, (tm, LANES), 1)
        g1 = jnp.sum(jnp.where(lane == INFO_GATE, info, 0.0), axis=1, keepdims=True)
        g2 = jnp.sum(jnp.where(lane == INFO_GATE + 1, info, 0.0), axis=1, keepdims=True)
        o_ref[tile, :] = (h_ref[tile, :] + g1 * _from_row_tiles(buf1.at[s], tm, chunks, stride)
                          + g2 * _from_row_tiles(buf2.at[s], tm, chunks, stride))

    @pl.when(p == 0)
    def _():
        request(d0_ref, 0)

    request(d1_ref, 1)
    combine(0)

    @pl.when(p + 1 < pl.num_programs(0))
    def _():
        request(d2_ref, 0)

    combine(1)


def moe_combine(h, info, y_rows, dest, *, tm):
    n, d = h.shape
    chunks = d // LANES
    tiles = n // tm
    dest3 = dest.reshape(tiles, 1, 2 * tm)
    dest_spec = lambda index_map: pl.BlockSpec((None, 1, 2 * tm), index_map, memory_space=pltpu.SMEM)
    buf = pltpu.VMEM((2, tm * COMBINE_ROW_STRIDE, LANES), F32)
    return pl.pallas_call(
        functools.partial(_combine_kernel, tm=tm, chunks=chunks),
        grid=(tiles // 2,),
        in_specs=[dest_spec(lambda p: (2 * p, 0, 0)),
                  dest_spec(lambda p: (2 * p + 1, 0, 0)),
                  dest_spec(lambda p: (jnp.minimum(2 * p + 2, tiles - 1), 0, 0)),
                  pl.BlockSpec((2 * tm, d), lambda p: (p, 0)),
                  pl.BlockSpec((2 * tm, LANES), lambda p: (p, 0)),
                  pl.BlockSpec(memory_space=pl.ANY)],
        out_specs=pl.BlockSpec((2 * tm, d), lambda p: (p, 0)),
        out_shape=jax.ShapeDtypeStruct((n, d), F32),
        scratch_shapes=[buf, buf, pltpu.SemaphoreType.DMA((2,))],
        compiler_params=_cparams("arbitrary"),
        name="moe_combine",
    )(dest3, dest3, dest3, h, info, y_rows)


def hierarchical_moe(h, gain, group_w, group_b, expert_w, expert_b, w_gate, w_up, w_down,
                     *, layer, router_tm, dispatch_tm, combine_tm):
    n, d = h.shape
    chunks = d // LANES
    xn_flat, info, cnt = moe_router(h, gain, group_w, group_b, expert_w, expert_b, tm=router_tm)

    n_experts = w_gate.shape[1]
    n_blocks = -(-2 * n // MOE_BLOCK) + n_experts
    n_slots = n_blocks * MOE_BLOCK
    experts = info[:, INFO_E:INFO_E + 2].astype(jnp.int32)
    ranks = info[:, INFO_RANK:INFO_RANK + 2].astype(jnp.int32)
    counts = cnt[0, :n_experts].astype(jnp.int32)
    padded = (counts + MOE_BLOCK - 1) // MOE_BLOCK * MOE_BLOCK
    pends = jnp.cumsum(padded)
    pstarts = pends - padded
    dest = pstarts[experts] + ranks
    n_used = pends[-1] // MOE_BLOCK
    blk = jnp.arange(n_blocks, dtype=jnp.int32)
    block_e = jnp.sum((pends[None, :] <= (blk * MOE_BLOCK)[:, None]).astype(jnp.int32), axis=1)
    block_e = jnp.minimum(block_e, n_experts - 1)
    block_e = jnp.where(blk < n_used, block_e, block_e[n_used - 1]).astype(jnp.int32)
    zstart = jnp.minimum(pstarts + counts, n_slots - MOE_BLOCK).astype(jnp.int32)

    n_used = n_used.reshape(1).astype(jnp.int32)
    xs = moe_dispatch(xn_flat.reshape(n, chunks, LANES), dest, zstart, n_used,
                      n_slots=n_slots, tm=dispatch_tm)
    y = moe_experts(xs.reshape(n_slots * chunks, LANES), block_e, n_used,
                    w_gate, w_up, w_down, layer=layer, chunks=chunks)
    return moe_combine(h, info, y.reshape(n_slots, chunks, LANES), dest, tm=combine_tm)


def kernel(x, even_norm, even_w_in, conv_w, conv_b, conv_norm_g, conv_norm_b, sb_q_norm, sb_k_norm,
           even_w_out, odd_norm, odd_w_in, fox_forget_b, fox_q_norm, fox_k_norm, sc_w, odd_w_out,
           moe_norm, router_group_w, router_group_b, router_expert_w, router_expert_b,
           expert_w_gate, expert_w_up, expert_w_down):
    b, s, d = x.shape
    n = b * s
    h = x.reshape(n, d)
    sb_width = SB_HEADS * HEAD_DIM
    fox_width = FOX_HEADS * HEAD_DIM
    moe_tiles = dict(router_tm=512, dispatch_tm=256, combine_tm=128)

    def moe(h, layer):
        return hierarchical_moe(
            h, moe_norm[layer], router_group_w[layer], router_group_b[layer],
            router_expert_w[layer], router_expert_b[layer],
            expert_w_gate, expert_w_up, expert_w_down, layer=layer, **moe_tiles)

    u = norm_matmul(h, even_norm[0], even_w_in[0].astype(BF16), tm=1024, tn=1024)
    u3 = u.reshape(b, s, -1)
    a = conformer_conv(u3, conv_w[0], conv_b[0], conv_norm_g[0], conv_norm_b[0], ts=256)
    o = stick_breaking_attention(u3, sb_q_norm[0], sb_k_norm[0], col0=2 * CONV_CH, heads=SB_HEADS)
    h = out_proj_residual(a.reshape(n, CONV_CH), o.reshape(n, sb_width),
                          even_w_out[0].astype(BF16), h, tm=256)
    h = moe(h, 0)

    n_qkv = 3 * fox_width
    w_in = odd_w_in[0]
    w_in = jnp.concatenate(
        [w_in[:, :n_qkv], w_in[:, n_qkv + FOX_HEADS:], w_in[:, n_qkv:n_qkv + FOX_HEADS],
         jnp.zeros((d, LANES - FOX_HEADS), F32)], axis=1).astype(BF16)
    u = norm_matmul(h, odd_norm[0], w_in, tm=1024, tn=896)
    u3 = u.reshape(b, s, -1)
    c_col = forget_cumsum(u3, fox_forget_b[0], col0=n_qkv + 3 * SC_CH, tc=512)
    c_row = jnp.transpose(c_col[:, :, :FOX_HEADS], (0, 2, 1)).reshape(b, FOX_HEADS, 1, s)
    o = forgetting_attention(u3, c_col, c_row, fox_q_norm[0], fox_k_norm[0], heads=FOX_HEADS)
    y = short_conv(u3, sc_w[0], col0=n_qkv, ts=256)
    h = out_proj_residual(o.reshape(n, fox_width), y.reshape(n, SC_CH),
                          odd_w_out[0].astype(BF16), h, tm=256)
    h = moe(h, 1)
    return h.reshape(b, s, d)
```

```python
import functools

import jax
import jax.numpy as jnp
from jax import lax
from jax.experimental import pallas as pl
from jax.experimental.pallas import tpu as pltpu

F32 = jnp.float32
BF16 = jnp.bfloat16

HEAD_DIM = 128
CONV_CH = 1024
CONV_WIDTH = 31
SB_HEADS = 8
FOX_HEADS = 8
SC_CH = 1024
SC_WIDTH = 3
N_GROUPS = 4
EXPERTS_PER_GROUP = 8
N_EXPERTS = N_GROUPS * EXPERTS_PER_GROUP
MOE_BLOCK = 256
RMS_EPS = 1e-6
LN_EPS = 1e-5

LANES = 128
SUBLANES = 8
VMEM_CAPACITY = 64 * 1024 * 1024
VMEM_LIMIT = VMEM_CAPACITY * 7 // 8
VMEM_LIMIT_EXPERTS = VMEM_CAPACITY - 2 * 1024 * 1024


def _cparams(*sem, vmem_limit=VMEM_LIMIT):
    return pltpu.CompilerParams(dimension_semantics=sem, vmem_limit_bytes=vmem_limit)


def _sigmoid(x):
    return 1.0 / (1.0 + jnp.exp(-x))


def _softplus_neg_abs(z):
    return jnp.log(1.0 + jnp.exp(-jnp.abs(z)))


def _rms(x, g):
    ms = jnp.mean(x * x, axis=-1, keepdims=True)
    return x * lax.rsqrt(ms + RMS_EPS) * g


def _split_bf16(x, parts):
    out = []
    r = x
    for _ in range(parts - 1):
        p = r.astype(BF16)
        out.append(p)
        r = r - p.astype(F32)
    out.append(r.astype(BF16))
    return out


def _norm_matmul_kernel(x_ref, g_ref, w_ref, o_ref, xn_ref):
    @pl.when(pl.program_id(1) == 0)
    def _():
        xn_ref[...] = _rms(x_ref[...], g_ref[...]).astype(BF16)

    o_ref[...] = jnp.dot(xn_ref[...], w_ref[...], preferred_element_type=F32)


def norm_matmul(x, gain, w, *, tm, tn):
    n, d = x.shape
    f = w.shape[1]
    return pl.pallas_call(
        _norm_matmul_kernel,
        grid=(n // tm, f // tn),
        in_specs=[pl.BlockSpec((tm, d), lambda i, j: (i, 0)),
                  pl.BlockSpec((1, d), lambda i, j: (0, 0)),
                  pl.BlockSpec((d, tn), lambda i, j: (0, j))],
        out_specs=pl.BlockSpec((tm, tn), lambda i, j: (i, j)),
        out_shape=jax.ShapeDtypeStruct((n, f), F32),
        scratch_shapes=[pltpu.VMEM((tm, d), BF16)],
        compiler_params=_cparams("parallel", "arbitrary"),
        name="norm_matmul",
    )(x, gain.reshape(1, d), w)


def _conv_tile(abuf, w_ref, bias_ref, cbuf, *, ts, halo, width, rows):
    n_ch = cbuf.shape[1]
    off = halo - (width - 1)

    def chan_body(c, carry):
        lanes = pl.ds(pl.multiple_of(c * LANES, LANES), LANES)
        wc = w_ref[:, lanes]
        for r0 in range(0, ts, rows):
            if bias_ref is None:
                acc = jnp.zeros((rows, LANES), F32)
            else:
                acc = jnp.broadcast_to(bias_ref[:, lanes], (rows, LANES))
            win = abuf[pl.ds(r0, rows + halo), lanes]
            for shift in range(SUBLANES):
                taps = [k for k in range(width) if (off + k) % SUBLANES == shift]
                if not taps:
                    continue
                shifted = win if shift == 0 else pltpu.roll(win, rows + halo - shift, axis=0)
                for k in taps:
                    base = off + k - shift
                    acc = acc + wc[k:k + 1, :] * shifted[base:base + rows]
            cbuf[pl.ds(r0, rows), lanes] = acc
        return carry

    lax.fori_loop(0, n_ch // LANES, chan_body, 0)


def _carry_halo(abuf, *, ts, halo):
    s = pl.program_id(1)

    @pl.when(s == 0)
    def _():
        abuf[0:halo, :] = jnp.zeros((halo, abuf.shape[1]), F32)

    @pl.when(s > 0)
    def _():
        abuf[0:halo, :] = abuf[ts:ts + halo, :]


CONV_HALO = 32
SC_HALO = 8


def _conformer_kernel(av_ref, ag_ref, w_ref, cb_ref, lg_ref, lb_ref, o_ref, abuf, cbuf, *, ts):
    _carry_halo(abuf, ts=ts, halo=CONV_HALO)
    abuf[CONV_HALO:CONV_HALO + ts, :] = av_ref[...] * _sigmoid(ag_ref[...])
    _conv_tile(abuf, w_ref, cb_ref, cbuf, ts=ts, halo=CONV_HALO, width=CONV_WIDTH, rows=64)
    y = cbuf[...]
    mu = jnp.mean(y, axis=-1, keepdims=True)
    yc = y - mu
    var = jnp.mean(yc * yc, axis=-1, keepdims=True)
    yn = yc * lax.rsqrt(var + LN_EPS) * lg_ref[...] + lb_ref[...]
    o_ref[...] = (yn * _sigmoid(yn)).astype(BF16)


def conformer_conv(u3, conv_w, conv_b, ln_g, ln_b, *, ts):
    b, s, _ = u3.shape
    c = conv_w.shape[1]
    w_pad = jnp.zeros((CONV_HALO, c), F32).at[:CONV_WIDTH].set(conv_w)
    row = lambda a: a.reshape(1, c)
    full = lambda shape: pl.BlockSpec(shape, lambda bi, si: (0, 0))
    return pl.pallas_call(
        functools.partial(_conformer_kernel, ts=ts),
        grid=(b, s // ts),
        in_specs=[pl.BlockSpec((None, ts, c), lambda bi, si: (bi, si, 0)),
                  pl.BlockSpec((None, ts, c), lambda bi, si: (bi, si, 1)),
                  full((CONV_HALO, c)), full((1, c)), full((1, c)), full((1, c))],
        out_specs=pl.BlockSpec((None, ts, c), lambda bi, si: (bi, si, 0)),
        out_shape=jax.ShapeDtypeStruct((b, s, c), BF16),
        scratch_shapes=[pltpu.VMEM((CONV_HALO + ts, c), F32), pltpu.VMEM((ts, c), F32)],
        compiler_params=_cparams("parallel", "arbitrary"),
        name="conformer_conv",
    )(u3, u3, w_pad, row(conv_b), row(ln_g), row(ln_b))


def _short_conv_kernel(bg_ref, cg_ref, xv_ref, w_ref, o_ref, abuf, cbuf, *, ts):
    _carry_halo(abuf, ts=ts, halo=SC_HALO)
    abuf[SC_HALO:SC_HALO + ts, :] = cg_ref[...] * xv_ref[...]
    _conv_tile(abuf, w_ref, None, cbuf, ts=ts, halo=SC_HALO, width=SC_WIDTH, rows=64)
    o_ref[...] = (bg_ref[...] * cbuf[...]).astype(BF16)


def short_conv(u3, sc_w, *, col0, ts):
    b, s, _ = u3.shape
    c = sc_w.shape[1]
    j0 = col0 // c
    w_pad = jnp.zeros((SUBLANES, c), F32).at[:SC_WIDTH].set(sc_w)
    return pl.pallas_call(
        functools.partial(_short_conv_kernel, ts=ts),
        grid=(b, s // ts),
        in_specs=[pl.BlockSpec((None, ts, c), lambda bi, si: (bi, si, j0)),
                  pl.BlockSpec((None, ts, c), lambda bi, si: (bi, si, j0 + 1)),
                  pl.BlockSpec((None, ts, c), lambda bi, si: (bi, si, j0 + 2)),
                  pl.BlockSpec((SUBLANES, c), lambda bi, si: (0, 0))],
        out_specs=pl.BlockSpec((None, ts, c), lambda bi, si: (bi, si, 0)),
        out_shape=jax.ShapeDtypeStruct((b, s, c), BF16),
        scratch_shapes=[pltpu.VMEM((SC_HALO + ts, c), F32), pltpu.VMEM((ts, c), F32)],
        compiler_params=_cparams("parallel", "arbitrary"),
        name="short_conv",
    )(u3, u3, u3, w_pad)


ATTN_HEADS = 2
ATTN_T = 256


def _head(e):
    return slice(e * HEAD_DIM, (e + 1) * HEAD_DIM)


def _prep_kv(k_ref, v_ref, kg_ref, kn_ref, vb_ref):
    @pl.when(pl.program_id(2) == 0)
    def _():
        for e in range(ATTN_HEADS):
            kn_ref[:, _head(e)] = _rms(k_ref[:, _head(e)], kg_ref[...]).astype(BF16)
        vb_ref[...] = v_ref[...].astype(BF16)


def _prep_q(q_ref, qg_ref, e):
    return (_rms(q_ref[:, _head(e)], qg_ref[...]) * (HEAD_DIM ** -0.5)).astype(BF16)


def _qk(qn, kb):
    return lax.dot_general(qn, kb, (((1,), (1,)), ((), ())), preferred_element_type=F32)


SB_SUB = 128
SB_CUTOFF = -110.0


def _sb_kernel(q_ref, k_ref, v_ref, qg_ref, kg_ref, o_ref, kn_ref, vb_ref, acc_ref, r_ref):
    t = ATTN_T
    i = pl.program_id(2)
    _prep_kv(k_ref, v_ref, kg_ref, kn_ref, vb_ref)
    qn = [_prep_q(q_ref, qg_ref, e) for e in range(ATTN_HEADS)]
    acc_ref[...] = jnp.zeros(acc_ref.shape, F32)
    r_ref[...] = jnp.zeros(r_ref.shape, F32)

    jr = lax.broadcasted_iota(jnp.int32, (t, t), 0)
    sc = lax.broadcasted_iota(jnp.int32, (t, t), 1)
    later = jnp.where((jr > sc) & (jr // SB_SUB == sc // SB_SUB), 1.0, 0.0).astype(BF16)
    strict = sc < jr

    def local_terms(c, masked):
        ks = pl.multiple_of(c * t, t)
        out = []
        for e in range(ATTN_HEADS):
            z = _qk(qn[e], kn_ref[pl.ds(ks, t), _head(e)])
            log_beta = jnp.minimum(z, 0.0) - _softplus_neg_abs(z)
            log_om = log_beta - z
            if masked:
                log_om = jnp.where(strict, log_om, 0.0)
            hi, lo = _split_bf16(log_om, 2)
            suffix = (jnp.dot(hi, later, preferred_element_type=F32)
                      + jnp.dot(lo, later, preferred_element_type=F32))
            out.append((log_beta + suffix,
                        jnp.sum(log_om[:, SB_SUB:], axis=1, keepdims=True),
                        jnp.sum(log_om[:, :SB_SUB], axis=1, keepdims=True)))
        return tuple(out)

    def accumulate(c, masked, terms):
        ks = pl.multiple_of(c * t, t)
        for e in range(ATTN_HEADS):
            base, tot_near, tot_far = terms[e]
            r = r_ref[:, _head(e)]
            w = jnp.exp(base + jnp.concatenate([r + tot_near, r], axis=1))
            if masked:
                w = jnp.where(strict, w, 0.0)
            acc_ref[:, _head(e)] += jnp.dot(w.astype(BF16), vb_ref[pl.ds(ks, t), _head(e)],
                                            preferred_element_type=F32)
            r_ref[:, _head(e)] = r + (tot_near + tot_far)

    accumulate(i, True, local_terms(i, True))

    def more(carry):
        n, r_max, _ = carry
        return (n < i) & (r_max > SB_CUTOFF)

    def body(carry):
        n, _, terms = carry
        c = i - 1 - n
        ahead = local_terms(jnp.maximum(c - 1, 0), False)
        accumulate(c, False, terms)
        return n + 1, jnp.max(r_ref[...]), ahead

    first = local_terms(jnp.maximum(i - 1, 0), False)
    lax.while_loop(more, body, (jnp.int32(0), jnp.max(r_ref[...]), first))
    o_ref[...] = acc_ref[...].astype(BF16)


def _attn_specs(s, j0, heads):
    w = ATTN_HEADS * HEAD_DIM
    g0 = j0 // ATTN_HEADS
    gh = heads // ATTN_HEADS
    q_spec = pl.BlockSpec((None, ATTN_T, w), lambda bi, hi, qi: (bi, qi, g0 + hi))
    k_spec = pl.BlockSpec((None, s, w), lambda bi, hi, qi: (bi, 0, g0 + gh + hi))
    v_spec = pl.BlockSpec((None, s, w), lambda bi, hi, qi: (bi, 0, g0 + 2 * gh + hi))
    o_spec = pl.BlockSpec((None, ATTN_T, w), lambda bi, hi, qi: (bi, qi, hi))
    return q_spec, k_spec, v_spec, o_spec


def stick_breaking_attention(u3, q_gain, k_gain, *, col0, heads):
    b, s, _ = u3.shape
    w = ATTN_HEADS * HEAD_DIM
    q_spec, k_spec, v_spec, o_spec = _attn_specs(s, col0 // HEAD_DIM, heads)
    gain = lambda g: g.reshape(1, HEAD_DIM)
    full = lambda shape: pl.BlockSpec(shape, lambda bi, hi, qi: (0, 0))
    return pl.pallas_call(
        _sb_kernel,
        grid=(b, heads // ATTN_HEADS, s // ATTN_T),
        in_specs=[q_spec, k_spec, v_spec, full((1, HEAD_DIM)), full((1, HEAD_DIM))],
        out_specs=o_spec,
        out_shape=jax.ShapeDtypeStruct((b, s, heads * HEAD_DIM), BF16),
        scratch_shapes=[pltpu.VMEM((s, w), BF16), pltpu.VMEM((s, w), BF16),
                        pltpu.VMEM((ATTN_T, w), F32), pltpu.VMEM((ATTN_T, w), F32)],
        compiler_params=_cparams("parallel", "parallel", "arbitrary"),
        name="stick_breaking_attention",
    )(u3, u3, u3, gain(q_gain), gain(k_gain))


FOX_PREP = 512


def _fox_kernel(q_ref, k_ref, v_ref, cc_ref, cr_ref, qg_ref, kg_ref, o_ref,
                kn_ref, vt_ref, cs_ref, acc_ref):
    t = ATTN_T
    hg = pl.program_id(1)
    i = pl.program_id(2)
    s = k_ref.shape[0]
    rows = lambda e: slice(e * HEAD_DIM, (e + 1) * HEAD_DIM)

    @pl.when(i == 0)
    def _():
        lane = lax.broadcasted_iota(jnp.int32, (FOX_PREP, LANES), 1)
        for e in range(ATTN_HEADS):
            kn_ref[:, _head(e)] = _rms(k_ref[:, _head(e)], kg_ref[...]).astype(BF16)
            for c0 in range(0, s, FOX_PREP):
                blk = slice(c0, c0 + FOX_PREP)
                vt_ref[rows(e), blk] = v_ref[blk, _head(e)].T.astype(BF16)
                col = jnp.sum(jnp.where(lane == hg * ATTN_HEADS + e, cc_ref[blk, :], 0.0),
                              axis=1, keepdims=True)
                cs_ref[blk, _head(e)] = jnp.broadcast_to(col, (FOX_PREP, HEAD_DIM))

    qn = [_prep_q(q_ref, qg_ref, e) for e in range(ATTN_HEADS)]
    ct = [cr_ref[e, :, pl.ds(pl.multiple_of(i * t, t), t)] for e in range(ATTN_HEADS)]
    acc_ref[...] = jnp.zeros(acc_ref.shape, F32)
    kpos = lax.broadcasted_iota(jnp.int32, (t, t), 0)
    qpos = lax.broadcasted_iota(jnp.int32, (t, t), 1)

    def scores(j):
        ks = pl.multiple_of(j * t, t)
        return tuple(_qk(kn_ref[pl.ds(ks, t), _head(e)], qn[e])
                     for e in range(ATTN_HEADS))

    def chunk(j, masked, stats, qk):
        ks = pl.multiple_of(j * t, t)
        out = []
        for e in range(ATTN_HEADS):
            m_prev, l_prev = stats[e]
            cs = cs_ref[pl.ds(ks, t), _head(e)]
            zt = qk[e] + (ct[e] - jnp.concatenate([cs] * (t // HEAD_DIM), axis=1))
            if masked:
                zt = jnp.where(kpos <= qpos, zt, -jnp.inf)
            m_new = jnp.maximum(m_prev, jnp.max(zt, axis=0, keepdims=True))
            alpha = jnp.exp(m_prev - m_new)
            pt = jnp.exp(zt - m_new)
            l_new = alpha * l_prev + jnp.sum(pt, axis=0, keepdims=True)
            acc_ref[rows(e), :] = alpha * acc_ref[rows(e), :] + jnp.dot(
                vt_ref[rows(e), pl.ds(ks, t)], pt.astype(BF16), preferred_element_type=F32)
            out.append((m_new, l_new))
        return tuple(out)

    init = tuple((jnp.full((1, t), -jnp.inf, F32), jnp.zeros((1, t), F32))
                 for _ in range(ATTN_HEADS))

    def body(j, carry):
        stats, qk = carry
        nxt = scores(j + 1)
        return chunk(j, False, stats, qk), nxt

    stats, qk = lax.fori_loop(0, i, body, (init, scores(0)))
    stats = chunk(i, True, stats, qk)
    for e in range(ATTN_HEADS):
        o_ref[:, _head(e)] = (acc_ref[rows(e), :] / stats[e][1]).T.astype(BF16)


def forgetting_attention(u3, c_col, c_row, q_gain, k_gain, *, heads):
    b, s, _ = u3.shape
    w = ATTN_HEADS * HEAD_DIM
    q_spec, k_spec, v_spec, o_spec = _attn_specs(s, 0, heads)
    gain = lambda g: g.reshape(1, HEAD_DIM)
    full = lambda shape: pl.BlockSpec(shape, lambda bi, hi, qi: (0, 0))
    return pl.pallas_call(
        _fox_kernel,
        grid=(b, heads // ATTN_HEADS, s // ATTN_T),
        in_specs=[q_spec, k_spec, v_spec,
                  pl.BlockSpec((None, s, LANES), lambda bi, hi, qi: (bi, 0, 0)),
                  pl.BlockSpec((None, ATTN_HEADS, 1, s), lambda bi, hi, qi: (bi, hi, 0, 0)),
                  full((1, HEAD_DIM)), full((1, HEAD_DIM))],
        out_specs=o_spec,
        out_shape=jax.ShapeDtypeStruct((b, s, heads * HEAD_DIM), BF16),
        scratch_shapes=[pltpu.VMEM((s, w), BF16), pltpu.VMEM((w, s), BF16),
                        pltpu.VMEM((s, w), F32), pltpu.VMEM((w, ATTN_T), F32)],
        compiler_params=_cparams("parallel", "parallel", "arbitrary"),
        name="forgetting_attention",
    )(u3, u3, u3, c_col, c_row, gain(q_gain), gain(k_gain))


def _forget_cumsum_kernel(f_ref, b_ref, o_ref, carry_ref, *, tc):
    @pl.when(pl.program_id(1) == 0)
    def _():
        carry_ref[...] = jnp.zeros((1, LANES), F32)

    x = f_ref[...] + b_ref[...]
    log_f = jnp.minimum(x, 0.0) - _softplus_neg_abs(x)
    r = lax.broadcasted_iota(jnp.int32, (tc, tc), 0)
    c = lax.broadcasted_iota(jnp.int32, (tc, tc), 1)
    tri = jnp.where(r >= c, 1.0, 0.0).astype(BF16)
    cs = carry_ref[...]
    for part in _split_bf16(log_f, 3):
        cs = cs + jnp.dot(tri, part, preferred_element_type=F32)
    o_ref[...] = cs
    carry_ref[...] = cs[tc - 1:tc, :]


def forget_cumsum(u3, forget_b, *, col0, tc):
    b, s, _ = u3.shape
    j0 = col0 // LANES
    b_pad = jnp.zeros((1, LANES), F32).at[0, :forget_b.shape[0]].set(forget_b)
    return pl.pallas_call(
        functools.partial(_forget_cumsum_kernel, tc=tc),
        grid=(b, s // tc),
        in_specs=[pl.BlockSpec((None, tc, LANES), lambda bi, si: (bi, si, j0)),
                  pl.BlockSpec((1, LANES), lambda bi, si: (0, 0))],
        out_specs=pl.BlockSpec((None, tc, LANES), lambda bi, si: (bi, si, 0)),
        out_shape=jax.ShapeDtypeStruct((b, s, LANES), F32),
        scratch_shapes=[pltpu.VMEM((1, LANES), F32)],
        compiler_params=_cparams("parallel", "arbitrary"),
        name="forget_cumsum",
    )(u3, b_pad)


def _out_proj_kernel(a_ref, b_ref, w1_ref, w2_ref, h_ref, o_ref):
    o_ref[...] = (h_ref[...]
                  + jnp.dot(a_ref[...], w1_ref[...], preferred_element_type=F32)
                  + jnp.dot(b_ref[...], w2_ref[...], preferred_element_type=F32))


def out_proj_residual(a, b, w, h, *, tm):
    n, ka = a.shape
    kb = b.shape[1]
    d = w.shape[1]
    return pl.pallas_call(
        _out_proj_kernel,
        grid=(n // tm,),
        in_specs=[pl.BlockSpec((tm, ka), lambda i: (i, 0)),
                  pl.BlockSpec((tm, kb), lambda i: (i, 0)),
                  pl.BlockSpec((ka, d), lambda i: (0, 0)),
                  pl.BlockSpec((kb, d), lambda i: (0, 0)),
                  pl.BlockSpec((tm, d), lambda i: (i, 0))],
        out_specs=pl.BlockSpec((tm, d), lambda i: (i, 0)),
        out_shape=jax.ShapeDtypeStruct((n, d), F32),
        compiler_params=_cparams("parallel"),
        name="out_proj_residual",
    )(a, b, w[:ka], w[ka:], h)


INFO_E, INFO_GATE, INFO_RANK = 0, 2, 4


def _lane_pick(x, lane, idx):
    return jnp.sum(jnp.where(lane == idx, x, 0.0), axis=1, keepdims=True)


def _router_kernel(h_ref, g_ref, wh_ref, wl_ref, b_ref, xn_ref, info_ref, cnt_ref, carry_ref, *, tm):
    @pl.when(pl.program_id(0) == 0)
    def _():
        carry_ref[...] = jnp.zeros((1, LANES), F32)

    xn = _rms(h_ref[...], g_ref[...])
    xn_ref[...] = xn

    xh, xl = _split_bf16(xn, 2)
    wh = wh_ref[...]
    logits = (jnp.dot(xh, wh, preferred_element_type=F32)
              + jnp.dot(xl, wh, preferred_element_type=F32)
              + jnp.dot(xh, wl_ref[...], preferred_element_type=F32)) + b_ref[...]

    lane = lax.broadcasted_iota(jnp.int32, (tm, LANES), 1).astype(F32)
    neg = -jnp.inf
    big = float(LANES)
    gl = jnp.where(lane < N_GROUPS, logits, neg)
    gmax = jnp.max(gl, axis=1, keepdims=True)
    g_top_p = 1.0 / jnp.sum(jnp.exp(gl - gmax), axis=1, keepdims=True)
    g_idx = jnp.min(jnp.where(gl == gmax, lane, big), axis=1, keepdims=True)

    lo = N_GROUPS + EXPERTS_PER_GROUP * g_idx
    el = jnp.where((lane >= lo) & (lane < lo + EXPERTS_PER_GROUP), logits, neg)
    m1 = jnp.max(el, axis=1, keepdims=True)
    i1 = jnp.min(jnp.where(el == m1, lane, big), axis=1, keepdims=True)
    el2 = jnp.where(lane == i1, neg, el)
    m2 = jnp.max(el2, axis=1, keepdims=True)
    i2 = jnp.min(jnp.where(el2 == m2, lane, big), axis=1, keepdims=True)
    ratio = jnp.exp(m2 - m1)
    p1 = 1.0 / (1.0 + ratio)
    p2 = ratio * p1
    e1 = i1 - N_GROUPS
    e2 = i2 - N_GROUPS

    onehot = jnp.where((lane == e1) | (lane == e2), 1.0, 0.0)
    r = lax.broadcasted_iota(jnp.int32, (tm, tm), 0)
    c = lax.broadcasted_iota(jnp.int32, (tm, tm), 1)
    before = jnp.where(r > c, 1.0, 0.0).astype(BF16)
    cnt = jnp.dot(before, onehot.astype(BF16), preferred_element_type=F32) + carry_ref[...]
    rank1 = _lane_pick(cnt, lane, e1)
    rank2 = _lane_pick(cnt, lane, e2)
    total = carry_ref[...] + jnp.sum(onehot, axis=0, keepdims=True)
    carry_ref[...] = total
    cnt_ref[...] = jnp.broadcast_to(total, (SUBLANES, LANES))

    info = jnp.zeros((tm, LANES), F32)
    for k, val in ((INFO_E, e1), (INFO_E + 1, e2), (INFO_GATE, g_top_p * p1),
                   (INFO_GATE + 1, g_top_p * p2), (INFO_RANK, rank1), (INFO_RANK + 1, rank2)):
        info = jnp.where(lane == k, val, info)
    info_ref[...] = info


def moe_router(h, gain, group_w, group_b, expert_w, expert_b, *, tm):
    n, d = h.shape
    n_logits = N_GROUPS + N_EXPERTS
    w = jnp.zeros((d, LANES), F32).at[:, :N_GROUPS].set(group_w).at[:, N_GROUPS:n_logits].set(expert_w)
    bias = jnp.zeros((1, LANES), F32).at[0, :N_GROUPS].set(group_b).at[0, N_GROUPS:n_logits].set(expert_b)
    w_hi = w.astype(BF16)
    w_lo = (w - w_hi.astype(F32)).astype(BF16)
    full = lambda shape: pl.BlockSpec(shape, lambda i: (0, 0))
    return pl.pallas_call(
        functools.partial(_router_kernel, tm=tm),
        grid=(n // tm,),
        in_specs=[pl.BlockSpec((tm, d), lambda i: (i, 0)), full((1, d)),
                  full((d, LANES)), full((d, LANES)), full((1, LANES))],
        out_specs=[pl.BlockSpec((tm, d), lambda i: (i, 0)),
                   pl.BlockSpec((tm, LANES), lambda i: (i, 0)),
                   full((SUBLANES, LANES))],
        out_shape=[jax.ShapeDtypeStruct((n, d), F32),
                   jax.ShapeDtypeStruct((n, LANES), F32),
                   jax.ShapeDtypeStruct((SUBLANES, LANES), F32)],
        scratch_shapes=[pltpu.VMEM((1, LANES), F32)],
        compiler_params=_cparams("arbitrary"),
        name="moe_router",
    )(h, gain.reshape(1, d), w_hi, w_lo, bias)


def _row(ref, i):
    return ref.at[pl.ds(i, 1), :]


def _row_copy(src, dst, sem):
    return pltpu.make_async_copy(src, dst, sem)


ROW_DMA_UNROLL = 8


def _dispatch_kernel(zstart_ref, nu_ref, dest_ref, xn_ref, xs_ref, zero_ref, sem, *, tm, n_experts):
    tile = pl.program_id(0)
    n_blocks = xs_ref.shape[0] // MOE_BLOCK

    @pl.when(tile == 0)
    def _():
        zero_ref[...] = jnp.zeros(zero_ref.shape, F32)
        block = lambda start: _row_copy(
            zero_ref, xs_ref.at[pl.ds(pl.multiple_of(start, MOE_BLOCK), MOE_BLOCK), :], sem)
        for e in range(n_experts):
            block(zstart_ref[e]).start()
        for e in range(n_experts):
            block(0).wait()

        def tail(i, carry):
            cp = block(i * MOE_BLOCK)
            cp.start()
            cp.wait()
            return carry

        lax.fori_loop(nu_ref[0], n_blocks, tail, 0)

    def issue(t, carry):
        _row_copy(_row(xn_ref, t), _row(xs_ref, dest_ref[0, 2 * t]), sem).start()
        _row_copy(_row(xn_ref, t), _row(xs_ref, dest_ref[0, 2 * t + 1]), sem).start()
        return carry

    lax.fori_loop(0, tm, issue, 0, unroll=ROW_DMA_UNROLL)

    def drain(t, carry):
        _row_copy(_row(xn_ref, 0), _row(xs_ref, 0), sem).wait()
        _row_copy(_row(xn_ref, 0), _row(xs_ref, 0), sem).wait()
        return carry

    lax.fori_loop(0, tm, drain, 0)


def moe_dispatch(xn, dest, zstart, n_used, *, n_slots, tm):
    n, d = xn.shape
    dest3 = dest.reshape(n // tm, 1, 2 * tm)
    return pl.pallas_call(
        functools.partial(_dispatch_kernel, tm=tm, n_experts=zstart.shape[0]),
        grid_spec=pltpu.PrefetchScalarGridSpec(
            num_scalar_prefetch=2,
            grid=(n // tm,),
            in_specs=[pl.BlockSpec((None, 1, 2 * tm), lambda i, z, nu: (i, 0, 0),
                                   memory_space=pltpu.SMEM),
                      pl.BlockSpec((tm, d), lambda i, z, nu: (i, 0))],
            out_specs=pl.BlockSpec(memory_space=pl.ANY),
            scratch_shapes=[pltpu.VMEM((MOE_BLOCK, d), F32),
                            pltpu.SemaphoreType.DMA(())]),
        out_shape=jax.ShapeDtypeStruct((n_slots, d), F32),
        compiler_params=_cparams("arbitrary"),
        name="moe_dispatch",
    )(zstart, n_used, dest3, xn)


EXPERT_FF_CHUNK = 512


def _expert_kernel(be_ref, nu_ref, x_ref, wg_ref, wu_ref, wd_ref, y_ref):
    i = pl.program_id(0)

    @pl.when(i < nu_ref[0])
    def _():
        x = x_ref[...].astype(BF16)
        y = None
        for c0 in range(0, wg_ref.shape[1], EXPERT_FF_CHUNK):
            cols = slice(c0, c0 + EXPERT_FF_CHUNK)
            gate = jnp.dot(x, wg_ref[:, cols].astype(BF16), preferred_element_type=F32)
            up = jnp.dot(x, wu_ref[:, cols].astype(BF16), preferred_element_type=F32)
            hidden = (gate * _sigmoid(gate) * up).astype(BF16)
            part = jnp.dot(hidden, wd_ref[cols, :].astype(BF16), preferred_element_type=F32)
            y = part if y is None else y + part
        y_ref[...] = y

    @pl.when(i >= nu_ref[0])
    def _():
        y_ref[...] = jnp.zeros(y_ref.shape, F32)


def moe_experts(xs, block_e, n_used, w_gate, w_up, w_down, *, layer):
    n_slots = xs.shape[0]
    n_blocks = n_slots // MOE_BLOCK
    _, _, d, ff = w_gate.shape
    x_map = lambda i, be, nu: (jnp.minimum(i, nu[0] - 1), 0)
    w_map = lambda i, be, nu: (layer, be[i], 0, 0)
    w_spec = lambda shape: pl.BlockSpec((None, None) + shape, w_map)
    return pl.pallas_call(
        _expert_kernel,
        grid_spec=pltpu.PrefetchScalarGridSpec(
            num_scalar_prefetch=2,
            grid=(n_blocks,),
            in_specs=[pl.BlockSpec((MOE_BLOCK, d), x_map),
                      w_spec((d, ff)), w_spec((d, ff)), w_spec((ff, d))],
            out_specs=pl.BlockSpec((MOE_BLOCK, d), lambda i, be, nu: (i, 0))),
        out_shape=jax.ShapeDtypeStruct((n_slots, d), F32),
        compiler_params=_cparams("arbitrary", vmem_limit=VMEM_LIMIT_EXPERTS),
        name="moe_experts",
    )(block_e, n_used, xs, w_gate, w_up, w_down)


def _combine_kernel(d0_ref, d1_ref, d2_ref, h_ref, info_ref, y_ref, o_ref, buf1, buf2, sems, *, tm):
    p = pl.program_id(0)

    def request(d_ref, s):
        def issue(t, carry):
            _row_copy(_row(y_ref, d_ref[0, 2 * t]), _row(buf1.at[s], t), sems.at[s]).start()
            _row_copy(_row(y_ref, d_ref[0, 2 * t + 1]), _row(buf2.at[s], t), sems.at[s]).start()
            return carry

        lax.fori_loop(0, tm, issue, 0, unroll=ROW_DMA_UNROLL)

    def combine(s):
        def drain(t, carry):
            _row_copy(_row(y_ref, 0), _row(buf1.at[s], 0), sems.at[s]).wait()
            _row_copy(_row(y_ref, 0), _row(buf2.at[s], 0), sems.at[s]).wait()
            return carry

        lax.fori_loop(0, tm, drain, 0)
        tile = slice(s * tm, (s + 1) * tm)
        info = info_ref[tile, :]
        lane = lax.broadcasted_iota(jnp.int32, (tm, LANES), 1)
        g1 = jnp.sum(jnp.where(lane == INFO_GATE, info, 0.0), axis=1, keepdims=True)
        g2 = jnp.sum(jnp.where(lane == INFO_GATE + 1, info, 0.0), axis=1, keepdims=True)
        o_ref[tile, :] = h_ref[tile, :] + g1 * buf1[s] + g2 * buf2[s]

    @pl.when(p == 0)
    def _():
        request(d0_ref, 0)

    request(d1_ref, 1)
    combine(0)

    @pl.when(p + 1 < pl.num_programs(0))
    def _():
        request(d2_ref, 0)

    combine(1)


def moe_combine(h, info, y, dest, *, tm):
    n, d = h.shape
    tiles = n // tm
    dest3 = dest.reshape(tiles, 1, 2 * tm)
    dest_spec = lambda index_map: pl.BlockSpec((None, 1, 2 * tm), index_map, memory_space=pltpu.SMEM)
    buf = pltpu.VMEM((2, tm, d), F32)
    return pl.pallas_call(
        functools.partial(_combine_kernel, tm=tm),
        grid=(tiles // 2,),
        in_specs=[dest_spec(lambda p: (2 * p, 0, 0)),
                  dest_spec(lambda p: (2 * p + 1, 0, 0)),
                  dest_spec(lambda p: (jnp.minimum(2 * p + 2, tiles - 1), 0, 0)),
                  pl.BlockSpec((2 * tm, d), lambda p: (p, 0)),
                  pl.BlockSpec((2 * tm, LANES), lambda p: (p, 0)),
                  pl.BlockSpec(memory_space=pl.ANY)],
        out_specs=pl.BlockSpec((2 * tm, d), lambda p: (p, 0)),
        out_shape=jax.ShapeDtypeStruct((n, d), F32),
        scratch_shapes=[buf, buf, pltpu.SemaphoreType.DMA((2,))],
        compiler_params=_cparams("arbitrary"),
        name="moe_combine",
    )(dest3, dest3, dest3, h, info, y)


def hierarchical_moe(h, gain, group_w, group_b, expert_w, expert_b, w_gate, w_up, w_down,
                     *, layer, router_tm, dispatch_tm, combine_tm):
    n, d = h.shape
    xn, info, cnt = moe_router(h, gain, group_w, group_b, expert_w, expert_b, tm=router_tm)

    n_experts = w_gate.shape[1]
    n_blocks = -(-2 * n // MOE_BLOCK) + n_experts
    n_slots = n_blocks * MOE_BLOCK
    experts = info[:, INFO_E:INFO_E + 2].astype(jnp.int32)
    ranks = info[:, INFO_RANK:INFO_RANK + 2].astype(jnp.int32)
    counts = cnt[0, :n_experts].astype(jnp.int32)
    padded = (counts + MOE_BLOCK - 1) // MOE_BLOCK * MOE_BLOCK
    pends = jnp.cumsum(padded)
    pstarts = pends - padded
    dest = pstarts[experts] + ranks
    n_used = pends[-1] // MOE_BLOCK
    blk = jnp.arange(n_blocks, dtype=jnp.int32)
    block_e = jnp.sum((pends[None, :] <= (blk * MOE_BLOCK)[:, None]).astype(jnp.int32), axis=1)
    block_e = jnp.minimum(block_e, n_experts - 1)
    block_e = jnp.where(blk < n_used, block_e, block_e[n_used - 1]).astype(jnp.int32)
    zstart = jnp.maximum(pends - MOE_BLOCK, 0).astype(jnp.int32)

    n_used = n_used.reshape(1).astype(jnp.int32)
    xs = moe_dispatch(xn, dest, zstart, n_used, n_slots=n_slots, tm=dispatch_tm)
    y = moe_experts(xs, block_e, n_used, w_gate, w_up, w_down, layer=layer)
    return moe_combine(h, info, y, dest, tm=combine_tm)


def kernel(x, even_norm, even_w_in, conv_w, conv_b, conv_norm_g, conv_norm_b, sb_q_norm, sb_k_norm,
           even_w_out, odd_norm, odd_w_in, fox_forget_b, fox_q_norm, fox_k_norm, sc_w, odd_w_out,
           moe_norm, router_group_w, router_group_b, router_expert_w, router_expert_b,
           expert_w_gate, expert_w_up, expert_w_down):
    b, s, d = x.shape
    n = b * s
    h = x.reshape(n, d)
    sb_width = SB_HEADS * HEAD_DIM
    fox_width = FOX_HEADS * HEAD_DIM
    moe_tiles = dict(router_tm=512, dispatch_tm=256, combine_tm=128)

    def moe(h, layer):
        return hierarchical_moe(
            h, moe_norm[layer], router_group_w[layer], router_group_b[layer],
            router_expert_w[layer], router_expert_b[layer],
            expert_w_gate, expert_w_up, expert_w_down, layer=layer, **moe_tiles)

    u = norm_matmul(h, even_norm[0], even_w_in[0].astype(BF16), tm=1024, tn=1024)
    u3 = u.reshape(b, s, -1)
    a = conformer_conv(u3, conv_w[0], conv_b[0], conv_norm_g[0], conv_norm_b[0], ts=256)
    o = stick_breaking_attention(u3, sb_q_norm[0], sb_k_norm[0], col0=2 * CONV_CH, heads=SB_HEADS)
    h = out_proj_residual(a.reshape(n, CONV_CH), o.reshape(n, sb_width),
                          even_w_out[0].astype(BF16), h, tm=512)
    h = moe(h, 0)

    n_qkv = 3 * fox_width
    w_in = odd_w_in[0]
    w_in = jnp.concatenate(
        [w_in[:, :n_qkv], w_in[:, n_qkv + FOX_HEADS:], w_in[:, n_qkv:n_qkv + FOX_HEADS],
         jnp.zeros((d, LANES - FOX_HEADS), F32)], axis=1).astype(BF16)
    u = norm_matmul(h, odd_norm[0], w_in, tm=1024, tn=896)
    u3 = u.reshape(b, s, -1)
    c_col = forget_cumsum(u3, fox_forget_b[0], col0=n_qkv + 3 * SC_CH, tc=512)
    c_row = jnp.transpose(c_col[:, :, :FOX_HEADS], (0, 2, 1)).reshape(b, FOX_HEADS, 1, s)
    o = forgetting_attention(u3, c_col, c_row, fox_q_norm[0], fox_k_norm[0], heads=FOX_HEADS)
    y = short_conv(u3, sc_w[0], col0=n_qkv, ts=256)
    h = out_proj_residual(o.reshape(n, fox_width), y.reshape(n, SC_CH),
                          odd_w_out[0].astype(BF16), h, tm=512)
    h = moe(h, 1)
    return h.reshape(b, s, d)
```

```python
import functools

import jax
import jax.numpy as jnp
from jax import lax
from jax.experimental import pallas as pl
from jax.experimental.pallas import tpu as pltpu

F32 = jnp.float32
BF16 = jnp.bfloat16

HEAD_DIM = 128
CONV_CH = 1024
CONV_WIDTH = 31
SB_HEADS = 8
FOX_HEADS = 8
SC_CH = 1024
SC_WIDTH = 3
N_GROUPS = 4
EXPERTS_PER_GROUP = 8
N_EXPERTS = N_GROUPS * EXPERTS_PER_GROUP
MOE_BLOCK = 256
RMS_EPS = 1e-6
LN_EPS = 1e-5

LANES = 128
SUBLANES = 8
VMEM_CAPACITY = 64 * 1024 * 1024
VMEM_LIMIT = VMEM_CAPACITY * 7 // 8
VMEM_LIMIT_EXPERTS = VMEM_CAPACITY - 2 * 1024 * 1024


def _cparams(*sem, vmem_limit=VMEM_LIMIT):
    return pltpu.CompilerParams(dimension_semantics=sem, vmem_limit_bytes=vmem_limit)


def _sigmoid(x):
    return 1.0 / (1.0 + jnp.exp(-x))


def _softplus_neg_abs(z):
    return jnp.log(1.0 + jnp.exp(-jnp.abs(z)))


def _rms(x, g):
    ms = jnp.mean(x * x, axis=-1, keepdims=True)
    return x * lax.rsqrt(ms + RMS_EPS) * g


def _split_bf16(x, parts):
    out = []
    r = x
    for _ in range(parts - 1):
        p = r.astype(BF16)
        out.append(p)
        r = r - p.astype(F32)
    out.append(r.astype(BF16))
    return out


def _norm_matmul_kernel(x_ref, g_ref, w_ref, o_ref, xn_ref):
    @pl.when(pl.program_id(1) == 0)
    def _():
        xn_ref[...] = _rms(x_ref[...], g_ref[...]).astype(BF16)

    o_ref[...] = jnp.dot(xn_ref[...], w_ref[...], preferred_element_type=F32)


def norm_matmul(x, gain, w, *, tm, tn):
    n, d = x.shape
    f = w.shape[1]
    return pl.pallas_call(
        _norm_matmul_kernel,
        grid=(n // tm, f // tn),
        in_specs=[pl.BlockSpec((tm, d), lambda i, j: (i, 0)),
                  pl.BlockSpec((1, d), lambda i, j: (0, 0)),
                  pl.BlockSpec((d, tn), lambda i, j: (0, j))],
        out_specs=pl.BlockSpec((tm, tn), lambda i, j: (i, j)),
        out_shape=jax.ShapeDtypeStruct((n, f), F32),
        scratch_shapes=[pltpu.VMEM((tm, d), BF16)],
        compiler_params=_cparams("parallel", "arbitrary"),
        name="norm_matmul",
    )(x, gain.reshape(1, d), w)


def _conv_tile(abuf, w_ref, bias_ref, cbuf, *, ts, halo, width, rows):
    n_ch = cbuf.shape[1]
    off = halo - (width - 1)

    def chan_body(c, carry):
        lanes = pl.ds(pl.multiple_of(c * LANES, LANES), LANES)
        wc = w_ref[:, lanes]
        for r0 in range(0, ts, rows):
            if bias_ref is None:
                acc = jnp.zeros((rows, LANES), F32)
            else:
                acc = jnp.broadcast_to(bias_ref[:, lanes], (rows, LANES))
            win = abuf[pl.ds(r0, rows + halo), lanes]
            for shift in range(SUBLANES):
                taps = [k for k in range(width) if (off + k) % SUBLANES == shift]
                if not taps:
                    continue
                shifted = win if shift == 0 else pltpu.roll(win, rows + halo - shift, axis=0)
                for k in taps:
                    base = off + k - shift
                    acc = acc + wc[k:k + 1, :] * shifted[base:base + rows]
            cbuf[pl.ds(r0, rows), lanes] = acc
        return carry

    lax.fori_loop(0, n_ch // LANES, chan_body, 0)


def _carry_halo(abuf, *, ts, halo):
    s = pl.program_id(1)

    @pl.when(s == 0)
    def _():
        abuf[0:halo, :] = jnp.zeros((halo, abuf.shape[1]), F32)

    @pl.when(s > 0)
    def _():
        abuf[0:halo, :] = abuf[ts:ts + halo, :]


CONV_HALO = 32
SC_HALO = 8


def _conformer_kernel(av_ref, ag_ref, w_ref, cb_ref, lg_ref, lb_ref, o_ref, abuf, cbuf, *, ts):
    _carry_halo(abuf, ts=ts, halo=CONV_HALO)
    abuf[CONV_HALO:CONV_HALO + ts, :] = av_ref[...] * _sigmoid(ag_ref[...])
    _conv_tile(abuf, w_ref, cb_ref, cbuf, ts=ts, halo=CONV_HALO, width=CONV_WIDTH, rows=64)
    y = cbuf[...]
    mu = jnp.mean(y, axis=-1, keepdims=True)
    yc = y - mu
    var = jnp.mean(yc * yc, axis=-1, keepdims=True)
    yn = yc * lax.rsqrt(var + LN_EPS) * lg_ref[...] + lb_ref[...]
    o_ref[...] = (yn * _sigmoid(yn)).astype(BF16)


def conformer_conv(u3, conv_w, conv_b, ln_g, ln_b, *, ts):
    b, s, _ = u3.shape
    c = conv_w.shape[1]
    w_pad = jnp.zeros((CONV_HALO, c), F32).at[:CONV_WIDTH].set(conv_w)
    row = lambda a: a.reshape(1, c)
    full = lambda shape: pl.BlockSpec(shape, lambda bi, si: (0, 0))
    return pl.pallas_call(
        functools.partial(_conformer_kernel, ts=ts),
        grid=(b, s // ts),
        in_specs=[pl.BlockSpec((None, ts, c), lambda bi, si: (bi, si, 0)),
                  pl.BlockSpec((None, ts, c), lambda bi, si: (bi, si, 1)),
                  full((CONV_HALO, c)), full((1, c)), full((1, c)), full((1, c))],
        out_specs=pl.BlockSpec((None, ts, c), lambda bi, si: (bi, si, 0)),
        out_shape=jax.ShapeDtypeStruct((b, s, c), BF16),
        scratch_shapes=[pltpu.VMEM((CONV_HALO + ts, c), F32), pltpu.VMEM((ts, c), F32)],
        compiler_params=_cparams("parallel", "arbitrary"),
        name="conformer_conv",
    )(u3, u3, w_pad, row(conv_b), row(ln_g), row(ln_b))


def _short_conv_kernel(bg_ref, cg_ref, xv_ref, w_ref, o_ref, abuf, cbuf, *, ts):
    _carry_halo(abuf, ts=ts, halo=SC_HALO)
    abuf[SC_HALO:SC_HALO + ts, :] = cg_ref[...] * xv_ref[...]
    _conv_tile(abuf, w_ref, None, cbuf, ts=ts, halo=SC_HALO, width=SC_WIDTH, rows=64)
    o_ref[...] = (bg_ref[...] * cbuf[...]).astype(BF16)


def short_conv(u3, sc_w, *, col0, ts):
    b, s, _ = u3.shape
    c = sc_w.shape[1]
    j0 = col0 // c
    w_pad = jnp.zeros((SUBLANES, c), F32).at[:SC_WIDTH].set(sc_w)
    return pl.pallas_call(
        functools.partial(_short_conv_kernel, ts=ts),
        grid=(b, s // ts),
        in_specs=[pl.BlockSpec((None, ts, c), lambda bi, si: (bi, si, j0)),
                  pl.BlockSpec((None, ts, c), lambda bi, si: (bi, si, j0 + 1)),
                  pl.BlockSpec((None, ts, c), lambda bi, si: (bi, si, j0 + 2)),
                  pl.BlockSpec((SUBLANES, c), lambda bi, si: (0, 0))],
        out_specs=pl.BlockSpec((None, ts, c), lambda bi, si: (bi, si, 0)),
        out_shape=jax.ShapeDtypeStruct((b, s, c), BF16),
        scratch_shapes=[pltpu.VMEM((SC_HALO + ts, c), F32), pltpu.VMEM((ts, c), F32)],
        compiler_params=_cparams("parallel", "arbitrary"),
        name="short_conv",
    )(u3, u3, u3, w_pad)


ATTN_HEADS = 4
ATTN_T = 256


def _head(e):
    return slice(e * HEAD_DIM, (e + 1) * HEAD_DIM)


def _prep_kv(k_ref, v_ref, kg_ref, kn_ref, vb_ref):
    @pl.when(pl.program_id(2) == 0)
    def _():
        for e in range(ATTN_HEADS):
            kn_ref[:, _head(e)] = _rms(k_ref[:, _head(e)], kg_ref[...]).astype(BF16)
        vb_ref[...] = v_ref[...].astype(BF16)


def _prep_q(q_ref, qg_ref, e):
    return (_rms(q_ref[:, _head(e)], qg_ref[...]) * (HEAD_DIM ** -0.5)).astype(BF16)


def _qk(qn, kb):
    return lax.dot_general(qn, kb, (((1,), (1,)), ((), ())), preferred_element_type=F32)


SB_SUB = 128
SB_CUTOFF = -110.0


def _sb_kernel(q_ref, k_ref, v_ref, qg_ref, kg_ref, o_ref, kn_ref, vb_ref, acc_ref, r_ref):
    t = ATTN_T
    i = pl.program_id(2)
    _prep_kv(k_ref, v_ref, kg_ref, kn_ref, vb_ref)
    qn = [_prep_q(q_ref, qg_ref, e) for e in range(ATTN_HEADS)]
    acc_ref[...] = jnp.zeros(acc_ref.shape, F32)
    r_ref[...] = jnp.zeros(r_ref.shape, F32)

    jr = lax.broadcasted_iota(jnp.int32, (t, t), 0)
    sc = lax.broadcasted_iota(jnp.int32, (t, t), 1)
    later = jnp.where((jr > sc) & (jr // SB_SUB == sc // SB_SUB), 1.0, 0.0).astype(BF16)
    strict = sc < jr

    def local_terms(c, masked):
        ks = pl.multiple_of(c * t, t)
        out = []
        for e in range(ATTN_HEADS):
            z = _qk(qn[e], kn_ref[pl.ds(ks, t), _head(e)])
            log_beta = jnp.minimum(z, 0.0) - _softplus_neg_abs(z)
            log_om = log_beta - z
            if masked:
                log_om = jnp.where(strict, log_om, 0.0)
            hi, lo = _split_bf16(log_om, 2)
            suffix = (jnp.dot(hi, later, preferred_element_type=F32)
                      + jnp.dot(lo, later, preferred_element_type=F32))
            out.append((log_beta + suffix,
                        jnp.sum(log_om[:, SB_SUB:], axis=1, keepdims=True),
                        jnp.sum(log_om[:, :SB_SUB], axis=1, keepdims=True)))
        return tuple(out)

    def accumulate(c, masked, terms):
        ks = pl.multiple_of(c * t, t)
        for e in range(ATTN_HEADS):
            base, tot_near, tot_far = terms[e]
            r = r_ref[:, _head(e)]
            w = jnp.exp(base + jnp.concatenate([r + tot_near, r], axis=1))
            if masked:
                w = jnp.where(strict, w, 0.0)
            acc_ref[:, _head(e)] += jnp.dot(w.astype(BF16), vb_ref[pl.ds(ks, t), _head(e)],
                                            preferred_element_type=F32)
            r_ref[:, _head(e)] = r + (tot_near + tot_far)

    accumulate(i, True, local_terms(i, True))

    def more(carry):
        n, r_max, _ = carry
        return (n < i) & (r_max > SB_CUTOFF)

    def body(carry):
        n, _, terms = carry
        c = i - 1 - n
        ahead = local_terms(jnp.maximum(c - 1, 0), False)
        accumulate(c, False, terms)
        return n + 1, jnp.max(r_ref[...]), ahead

    first = local_terms(jnp.maximum(i - 1, 0), False)
    lax.while_loop(more, body, (jnp.int32(0), jnp.max(r_ref[...]), first))
    o_ref[...] = acc_ref[...].astype(BF16)


def _attn_specs(s, j0, heads):
    w = ATTN_HEADS * HEAD_DIM
    g0 = j0 // ATTN_HEADS
    gh = heads // ATTN_HEADS
    q_spec = pl.BlockSpec((None, ATTN_T, w), lambda bi, hi, qi: (bi, qi, g0 + hi))
    k_spec = pl.BlockSpec((None, s, w), lambda bi, hi, qi: (bi, 0, g0 + gh + hi),
                          pipeline_mode=pl.Buffered(1))
    v_spec = pl.BlockSpec((None, s, w), lambda bi, hi, qi: (bi, 0, g0 + 2 * gh + hi),
                          pipeline_mode=pl.Buffered(1))
    o_spec = pl.BlockSpec((None, ATTN_T, w), lambda bi, hi, qi: (bi, qi, hi))
    return q_spec, k_spec, v_spec, o_spec


def stick_breaking_attention(u3, q_gain, k_gain, *, col0, heads):
    b, s, _ = u3.shape
    w = ATTN_HEADS * HEAD_DIM
    q_spec, k_spec, v_spec, o_spec = _attn_specs(s, col0 // HEAD_DIM, heads)
    gain = lambda g: g.reshape(1, HEAD_DIM)
    full = lambda shape: pl.BlockSpec(shape, lambda bi, hi, qi: (0, 0))
    return pl.pallas_call(
        _sb_kernel,
        grid=(b, heads // ATTN_HEADS, s // ATTN_T),
        in_specs=[q_spec, k_spec, v_spec, full((1, HEAD_DIM)), full((1, HEAD_DIM))],
        out_specs=o_spec,
        out_shape=jax.ShapeDtypeStruct((b, s, heads * HEAD_DIM), BF16),
        scratch_shapes=[pltpu.VMEM((s, w), BF16), pltpu.VMEM((s, w), BF16),
                        pltpu.VMEM((ATTN_T, w), F32), pltpu.VMEM((ATTN_T, w), F32)],
        compiler_params=_cparams("parallel", "parallel", "arbitrary"),
        name="stick_breaking_attention",
    )(u3, u3, u3, gain(q_gain), gain(k_gain))


FOX_PREP = 512
FOX_K = 2 * HEAD_DIM


def _bias_lanes(c_rep, lane, first, sign):
    other = 3 - first
    tile = jnp.where((lane >= other) & (lane < other + 3), 1.0, 0.0)
    for k, part in enumerate(_split_bf16(sign * c_rep, 3)):
        tile = jnp.where(lane == first + k, part.astype(F32), tile)
    return tile.astype(BF16)


def _fox_kernel(q_ref, k_ref, v_ref, cc_ref, qg_ref, kg_ref, o_ref, kn_ref, vt_ref, cs_ref, acc_ref):
    t = ATTN_T
    hg = pl.program_id(1)
    i = pl.program_id(2)
    s = k_ref.shape[0]
    rows = lambda e: slice(e * HEAD_DIM, (e + 1) * HEAD_DIM)
    kcols = lambda e: slice(e * FOX_K, (e + 1) * FOX_K)

    @pl.when(i == 0)
    def _():
        lane = lax.broadcasted_iota(jnp.int32, (FOX_PREP, LANES), 1)
        for e in range(ATTN_HEADS):
            for c0 in range(0, s, FOX_PREP):
                blk = slice(c0, c0 + FOX_PREP)
                vt_ref[rows(e), blk] = v_ref[blk, _head(e)].T.astype(BF16)
                col = jnp.sum(jnp.where(lane == hg * ATTN_HEADS + e, cc_ref[blk, :], 0.0),
                              axis=1, keepdims=True)
                c_rep = jnp.broadcast_to(col, (FOX_PREP, LANES))
                cs_ref[blk, _head(e)] = c_rep
                kn = _rms(k_ref[blk, _head(e)], kg_ref[...]).astype(BF16)
                kn_ref[blk, kcols(e)] = jnp.concatenate(
                    [kn, _bias_lanes(c_rep, lane, 0, -1.0)], axis=1)

    qlane = lax.broadcasted_iota(jnp.int32, (t, LANES), 1)
    qrows = pl.ds(pl.multiple_of(i * t, t), t)
    qn = [jnp.concatenate([_prep_q(q_ref, qg_ref, e),
                           _bias_lanes(cs_ref[qrows, _head(e)], qlane, 3, 1.0)], axis=1)
          for e in range(ATTN_HEADS)]
    acc_ref[...] = jnp.zeros(acc_ref.shape, F32)
    kpos = lax.broadcasted_iota(jnp.int32, (t, t), 0)
    qpos = lax.broadcasted_iota(jnp.int32, (t, t), 1)

    def scores(j):
        ks = pl.multiple_of(j * t, t)
        return tuple(_qk(kn_ref[pl.ds(ks, t), kcols(e)], qn[e])
                     for e in range(ATTN_HEADS))

    def chunk(j, masked, stats, qk):
        ks = pl.multiple_of(j * t, t)
        out = []
        for e in range(ATTN_HEADS):
            m_prev, l_prev = stats[e]
            zt = qk[e]
            if masked:
                zt = jnp.where(kpos <= qpos, zt, -jnp.inf)
            m_new = jnp.maximum(m_prev, jnp.max(zt, axis=0, keepdims=True))
            alpha = jnp.exp(m_prev - m_new)
            pt = jnp.exp(zt - m_new)
            l_new = alpha * l_prev + jnp.sum(pt, axis=0, keepdims=True)
            acc_ref[rows(e), :] = alpha * acc_ref[rows(e), :] + jnp.dot(
                vt_ref[rows(e), pl.ds(ks, t)], pt.astype(BF16), preferred_element_type=F32)
            out.append((m_new, l_new))
        return tuple(out)

    init = tuple((jnp.full((1, t), -jnp.inf, F32), jnp.zeros((1, t), F32))
                 for _ in range(ATTN_HEADS))

    def body(j, carry):
        stats, qk = carry
        nxt = scores(j + 1)
        return chunk(j, False, stats, qk), nxt

    stats, qk = lax.fori_loop(0, i, body, (init, scores(0)))
    stats = chunk(i, True, stats, qk)
    for e in range(ATTN_HEADS):
        o_ref[:, _head(e)] = (acc_ref[rows(e), :] / stats[e][1]).T.astype(BF16)


def forgetting_attention(u3, c_col, q_gain, k_gain, *, heads):
    b, s, _ = u3.shape
    w = ATTN_HEADS * HEAD_DIM
    q_spec, k_spec, v_spec, o_spec = _attn_specs(s, 0, heads)
    gain = lambda g: g.reshape(1, HEAD_DIM)
    full = lambda shape: pl.BlockSpec(shape, lambda bi, hi, qi: (0, 0))
    return pl.pallas_call(
        _fox_kernel,
        grid=(b, heads // ATTN_HEADS, s // ATTN_T),
        in_specs=[q_spec, k_spec, v_spec,
                  pl.BlockSpec((None, s, LANES), lambda bi, hi, qi: (bi, 0, 0)),
                  full((1, HEAD_DIM)), full((1, HEAD_DIM))],
        out_specs=o_spec,
        out_shape=jax.ShapeDtypeStruct((b, s, heads * HEAD_DIM), BF16),
        scratch_shapes=[pltpu.VMEM((s, ATTN_HEADS * FOX_K), BF16), pltpu.VMEM((w, s), BF16),
                        pltpu.VMEM((s, w), F32), pltpu.VMEM((w, ATTN_T), F32)],
        compiler_params=_cparams("parallel", "parallel", "arbitrary"),
        name="forgetting_attention",
    )(u3, u3, u3, c_col, gain(q_gain), gain(k_gain))


def _forget_cumsum_kernel(f_ref, b_ref, o_ref, carry_ref, *, tc):
    @pl.when(pl.program_id(1) == 0)
    def _():
        carry_ref[...] = jnp.zeros((1, LANES), F32)

    x = f_ref[...] + b_ref[...]
    log_f = jnp.minimum(x, 0.0) - _softplus_neg_abs(x)
    r = lax.broadcasted_iota(jnp.int32, (tc, tc), 0)
    c = lax.broadcasted_iota(jnp.int32, (tc, tc), 1)
    tri = jnp.where(r >= c, 1.0, 0.0).astype(BF16)
    cs = carry_ref[...]
    for part in _split_bf16(log_f, 3):
        cs = cs + jnp.dot(tri, part, preferred_element_type=F32)
    o_ref[...] = cs
    carry_ref[...] = cs[tc - 1:tc, :]


def forget_cumsum(u3, forget_b, *, col0, tc):
    b, s, _ = u3.shape
    j0 = col0 // LANES
    b_pad = jnp.zeros((1, LANES), F32).at[0, :forget_b.shape[0]].set(forget_b)
    return pl.pallas_call(
        functools.partial(_forget_cumsum_kernel, tc=tc),
        grid=(b, s // tc),
        in_specs=[pl.BlockSpec((None, tc, LANES), lambda bi, si: (bi, si, j0)),
                  pl.BlockSpec((1, LANES), lambda bi, si: (0, 0))],
        out_specs=pl.BlockSpec((None, tc, LANES), lambda bi, si: (bi, si, 0)),
        out_shape=jax.ShapeDtypeStruct((b, s, LANES), F32),
        scratch_shapes=[pltpu.VMEM((1, LANES), F32)],
        compiler_params=_cparams("parallel", "arbitrary"),
        name="forget_cumsum",
    )(u3, b_pad)


def _out_proj_kernel(a_ref, b_ref, w1_ref, w2_ref, h_ref, o_ref):
    o_ref[...] = (h_ref[...]
                  + jnp.dot(a_ref[...], w1_ref[...], preferred_element_type=F32)
                  + jnp.dot(b_ref[...], w2_ref[...], preferred_element_type=F32))


def out_proj_residual(a, b, w, h, *, tm):
    n, ka = a.shape
    kb = b.shape[1]
    d = w.shape[1]
    return pl.pallas_call(
        _out_proj_kernel,
        grid=(n // tm,),
        in_specs=[pl.BlockSpec((tm, ka), lambda i: (i, 0)),
                  pl.BlockSpec((tm, kb), lambda i: (i, 0)),
                  pl.BlockSpec((ka, d), lambda i: (0, 0)),
                  pl.BlockSpec((kb, d), lambda i: (0, 0)),
                  pl.BlockSpec((tm, d), lambda i: (i, 0))],
        out_specs=pl.BlockSpec((tm, d), lambda i: (i, 0)),
        out_shape=jax.ShapeDtypeStruct((n, d), F32),
        compiler_params=_cparams("parallel"),
        name="out_proj_residual",
    )(a, b, w[:ka], w[ka:], h)


INFO_E, INFO_GATE, INFO_RANK = 0, 2, 4


def _lane_pick(x, lane, idx):
    return jnp.sum(jnp.where(lane == idx, x, 0.0), axis=1, keepdims=True)


def _router_kernel(h_ref, g_ref, wh_ref, wl_ref, b_ref, xn_ref, info_ref, cnt_ref, carry_ref, *, tm):
    @pl.when(pl.program_id(0) == 0)
    def _():
        carry_ref[...] = jnp.zeros((1, LANES), F32)

    xn = _rms(h_ref[...], g_ref[...])
    xn_ref[...] = xn

    xh, xl = _split_bf16(xn, 2)
    wh = wh_ref[...]
    logits = (jnp.dot(xh, wh, preferred_element_type=F32)
              + jnp.dot(xl, wh, preferred_element_type=F32)
              + jnp.dot(xh, wl_ref[...], preferred_element_type=F32)) + b_ref[...]

    lane = lax.broadcasted_iota(jnp.int32, (tm, LANES), 1).astype(F32)
    neg = -jnp.inf
    big = float(LANES)
    gl = jnp.where(lane < N_GROUPS, logits, neg)
    gmax = jnp.max(gl, axis=1, keepdims=True)
    g_top_p = 1.0 / jnp.sum(jnp.exp(gl - gmax), axis=1, keepdims=True)
    g_idx = jnp.min(jnp.where(gl == gmax, lane, big), axis=1, keepdims=True)

    lo = N_GROUPS + EXPERTS_PER_GROUP * g_idx
    el = jnp.where((lane >= lo) & (lane < lo + EXPERTS_PER_GROUP), logits, neg)
    m1 = jnp.max(el, axis=1, keepdims=True)
    i1 = jnp.min(jnp.where(el == m1, lane, big), axis=1, keepdims=True)
    el2 = jnp.where(lane == i1, neg, el)
    m2 = jnp.max(el2, axis=1, keepdims=True)
    i2 = jnp.min(jnp.where(el2 == m2, lane, big), axis=1, keepdims=True)
    ratio = jnp.exp(m2 - m1)
    p1 = 1.0 / (1.0 + ratio)
    p2 = ratio * p1
    e1 = i1 - N_GROUPS
    e2 = i2 - N_GROUPS

    onehot = jnp.where((lane == e1) | (lane == e2), 1.0, 0.0)
    r = lax.broadcasted_iota(jnp.int32, (tm, tm), 0)
    c = lax.broadcasted_iota(jnp.int32, (tm, tm), 1)
    before = jnp.where(r > c, 1.0, 0.0).astype(BF16)
    cnt = jnp.dot(before, onehot.astype(BF16), preferred_element_type=F32) + carry_ref[...]
    rank1 = _lane_pick(cnt, lane, e1)
    rank2 = _lane_pick(cnt, lane, e2)
    total = carry_ref[...] + jnp.sum(onehot, axis=0, keepdims=True)
    carry_ref[...] = total
    cnt_ref[...] = jnp.broadcast_to(total, (SUBLANES, LANES))

    info = jnp.zeros((tm, LANES), F32)
    for k, val in ((INFO_E, e1), (INFO_E + 1, e2), (INFO_GATE, g_top_p * p1),
                   (INFO_GATE + 1, g_top_p * p2), (INFO_RANK, rank1), (INFO_RANK + 1, rank2)):
        info = jnp.where(lane == k, val, info)
    info_ref[...] = info


def moe_router(h, gain, group_w, group_b, expert_w, expert_b, *, tm):
    n, d = h.shape
    n_logits = N_GROUPS + N_EXPERTS
    w = jnp.zeros((d, LANES), F32).at[:, :N_GROUPS].set(group_w).at[:, N_GROUPS:n_logits].set(expert_w)
    bias = jnp.zeros((1, LANES), F32).at[0, :N_GROUPS].set(group_b).at[0, N_GROUPS:n_logits].set(expert_b)
    w_hi = w.astype(BF16)
    w_lo = (w - w_hi.astype(F32)).astype(BF16)
    full = lambda shape: pl.BlockSpec(shape, lambda i: (0, 0))
    return pl.pallas_call(
        functools.partial(_router_kernel, tm=tm),
        grid=(n // tm,),
        in_specs=[pl.BlockSpec((tm, d), lambda i: (i, 0)), full((1, d)),
                  full((d, LANES)), full((d, LANES)), full((1, LANES))],
        out_specs=[pl.BlockSpec((tm, d), lambda i: (i, 0)),
                   pl.BlockSpec((tm, LANES), lambda i: (i, 0)),
                   full((SUBLANES, LANES))],
        out_shape=[jax.ShapeDtypeStruct((n, d), F32),
                   jax.ShapeDtypeStruct((n, LANES), F32),
                   jax.ShapeDtypeStruct((SUBLANES, LANES), F32)],
        scratch_shapes=[pltpu.VMEM((1, LANES), F32)],
        compiler_params=_cparams("arbitrary"),
        name="moe_router",
    )(h, gain.reshape(1, d), w_hi, w_lo, bias)


def _row(ref, i):
    return ref.at[pl.ds(i, 1), :]


def _row_copy(src, dst, sem):
    return pltpu.make_async_copy(src, dst, sem)


ROW_DMA_UNROLL = 8


def _dispatch_kernel(zstart_ref, nu_ref, dest_ref, xn_ref, xs_ref, zero_ref, sem, *, tm, n_experts):
    tile = pl.program_id(0)
    n_blocks = xs_ref.shape[0] // MOE_BLOCK

    @pl.when(tile == 0)
    def _():
        zero_ref[...] = jnp.zeros(zero_ref.shape, F32)
        block = lambda start: _row_copy(
            zero_ref, xs_ref.at[pl.ds(pl.multiple_of(start, MOE_BLOCK), MOE_BLOCK), :], sem)
        for e in range(n_experts):
            block(zstart_ref[e]).start()
        for e in range(n_experts):
            block(0).wait()

        def tail(i, carry):
            cp = block(i * MOE_BLOCK)
            cp.start()
            cp.wait()
            return carry

        lax.fori_loop(nu_ref[0], n_blocks, tail, 0)

    def issue(t, carry):
        _row_copy(_row(xn_ref, t), _row(xs_ref, dest_ref[0, 2 * t]), sem).start()
        _row_copy(_row(xn_ref, t), _row(xs_ref, dest_ref[0, 2 * t + 1]), sem).start()
        return carry

    lax.fori_loop(0, tm, issue, 0, unroll=ROW_DMA_UNROLL)

    def drain(t, carry):
        _row_copy(_row(xn_ref, 0), _row(xs_ref, 0), sem).wait()
        _row_copy(_row(xn_ref, 0), _row(xs_ref, 0), sem).wait()
        return carry

    lax.fori_loop(0, tm, drain, 0)


def moe_dispatch(xn, dest, zstart, n_used, *, n_slots, tm):
    n, d = xn.shape
    dest3 = dest.reshape(n // tm, 1, 2 * tm)
    return pl.pallas_call(
        functools.partial(_dispatch_kernel, tm=tm, n_experts=zstart.shape[0]),
        grid_spec=pltpu.PrefetchScalarGridSpec(
            num_scalar_prefetch=2,
            grid=(n // tm,),
            in_specs=[pl.BlockSpec((None, 1, 2 * tm), lambda i, z, nu: (i, 0, 0),
                                   memory_space=pltpu.SMEM),
                      pl.BlockSpec((tm, d), lambda i, z, nu: (i, 0))],
            out_specs=pl.BlockSpec(memory_space=pl.ANY),
            scratch_shapes=[pltpu.VMEM((MOE_BLOCK, d), F32),
                            pltpu.SemaphoreType.DMA(())]),
        out_shape=jax.ShapeDtypeStruct((n_slots, d), F32),
        compiler_params=_cparams("arbitrary"),
        name="moe_dispatch",
    )(zstart, n_used, dest3, xn)


EXPERT_FF_CHUNK = 512


def _expert_kernel(be_ref, nu_ref, x_ref, wg_ref, wu_ref, wd_ref, y_ref):
    i = pl.program_id(0)

    @pl.when(i < nu_ref[0])
    def _():
        x = x_ref[...].astype(BF16)
        y = None
        for c0 in range(0, wg_ref.shape[1], EXPERT_FF_CHUNK):
            cols = slice(c0, c0 + EXPERT_FF_CHUNK)
            gate = jnp.dot(x, wg_ref[:, cols].astype(BF16), preferred_element_type=F32)
            up = jnp.dot(x, wu_ref[:, cols].astype(BF16), preferred_element_type=F32)
            hidden = (gate * _sigmoid(gate) * up).astype(BF16)
            part = jnp.dot(hidden, wd_ref[cols, :].astype(BF16), preferred_element_type=F32)
            y = part if y is None else y + part
        y_ref[...] = y

    @pl.when(i >= nu_ref[0])
    def _():
        y_ref[...] = jnp.zeros(y_ref.shape, F32)


def moe_experts(xs, block_e, n_used, w_gate, w_up, w_down, *, layer):
    n_slots = xs.shape[0]
    n_blocks = n_slots // MOE_BLOCK
    _, _, d, ff = w_gate.shape
    x_map = lambda i, be, nu: (jnp.minimum(i, nu[0] - 1), 0)
    w_map = lambda i, be, nu: (layer, be[i], 0, 0)
    w_spec = lambda shape: pl.BlockSpec((None, None) + shape, w_map)
    return pl.pallas_call(
        _expert_kernel,
        grid_spec=pltpu.PrefetchScalarGridSpec(
            num_scalar_prefetch=2,
            grid=(n_blocks,),
            in_specs=[pl.BlockSpec((MOE_BLOCK, d), x_map),
                      w_spec((d, ff)), w_spec((d, ff)), w_spec((ff, d))],
            out_specs=pl.BlockSpec((MOE_BLOCK, d), lambda i, be, nu: (i, 0))),
        out_shape=jax.ShapeDtypeStruct((n_slots, d), F32),
        compiler_params=_cparams("arbitrary", vmem_limit=VMEM_LIMIT_EXPERTS),
        name="moe_experts",
    )(block_e, n_used, xs, w_gate, w_up, w_down)


def _combine_kernel(d0_ref, d1_ref, d2_ref, h_ref, info_ref, y_ref, o_ref, buf1, buf2, sems, *, tm):
    p = pl.program_id(0)

    def request(d_ref, s):
        def issue(t, carry):
            _row_copy(_row(y_ref, d_ref[0, 2 * t]), _row(buf1.at[s], t), sems.at[s]).start()
            _row_copy(_row(y_ref, d_ref[0, 2 * t + 1]), _row(buf2.at[s], t), sems.at[s]).start()
            return carry

        lax.fori_loop(0, tm, issue, 0, unroll=ROW_DMA_UNROLL)

    def combine(s):
        def drain(t, carry):
            _row_copy(_row(y_ref, 0), _row(buf1.at[s], 0), sems.at[s]).wait()
            _row_copy(_row(y_ref, 0), _row(buf2.at[s], 0), sems.at[s]).wait()
            return carry

        lax.fori_loop(0, tm, drain, 0)
        tile = slice(s * tm, (s + 1) * tm)
        info = info_ref[tile, :]
        lane = lax.broadcasted_iota(jnp.int32, (tm, LANES), 1)
        g1 = jnp.sum(jnp.where(lane == INFO_GATE, info, 0.0), axis=1, keepdims=True)
        g2 = jnp.sum(jnp.where(lane == INFO_GATE + 1, info, 0.0), axis=1, keepdims=True)
        o_ref[tile, :] = h_ref[tile, :] + g1 * buf1[s] + g2 * buf2[s]

    @pl.when(p == 0)
    def _():
        request(d0_ref, 0)

    request(d1_ref, 1)
    combine(0)

    @pl.when(p + 1 < pl.num_programs(0))
    def _():
        request(d2_ref, 0)

    combine(1)


def moe_combine(h, info, y, dest, *, tm):
    n, d = h.shape
    tiles = n // tm
    dest3 = dest.reshape(tiles, 1, 2 * tm)
    dest_spec = lambda index_map: pl.BlockSpec((None, 1, 2 * tm), index_map, memory_space=pltpu.SMEM)
    buf = pltpu.VMEM((2, tm, d), F32)
    return pl.pallas_call(
        functools.partial(_combine_kernel, tm=tm),
        grid=(tiles // 2,),
        in_specs=[dest_spec(lambda p: (2 * p, 0, 0)),
                  dest_spec(lambda p: (2 * p + 1, 0, 0)),
                  dest_spec(lambda p: (jnp.minimum(2 * p + 2, tiles - 1), 0, 0)),
                  pl.BlockSpec((2 * tm, d), lambda p: (p, 0)),
                  pl.BlockSpec((2 * tm, LANES), lambda p: (p, 0)),
                  pl.BlockSpec(memory_space=pl.ANY)],
        out_specs=pl.BlockSpec((2 * tm, d), lambda p: (p, 0)),
        out_shape=jax.ShapeDtypeStruct((n, d), F32),
        scratch_shapes=[buf, buf, pltpu.SemaphoreType.DMA((2,))],
        compiler_params=_cparams("arbitrary"),
        name="moe_combine",
    )(dest3, dest3, dest3, h, info, y)


def hierarchical_moe(h, gain, group_w, group_b, expert_w, expert_b, w_gate, w_up, w_down,
                     *, layer, router_tm, dispatch_tm, combine_tm):
    n, d = h.shape
    xn, info, cnt = moe_router(h, gain, group_w, group_b, expert_w, expert_b, tm=router_tm)

    n_experts = w_gate.shape[1]
    n_blocks = -(-2 * n // MOE_BLOCK) + n_experts
    n_slots = n_blocks * MOE_BLOCK
    experts = info[:, INFO_E:INFO_E + 2].astype(jnp.int32)
    ranks = info[:, INFO_RANK:INFO_RANK + 2].astype(jnp.int32)
    counts = cnt[0, :n_experts].astype(jnp.int32)
    padded = (counts + MOE_BLOCK - 1) // MOE_BLOCK * MOE_BLOCK
    pends = jnp.cumsum(padded)
    pstarts = pends - padded
    dest = pstarts[experts] + ranks
    n_used = pends[-1] // MOE_BLOCK
    blk = jnp.arange(n_blocks, dtype=jnp.int32)
    block_e = jnp.sum((pends[None, :] <= (blk * MOE_BLOCK)[:, None]).astype(jnp.int32), axis=1)
    block_e = jnp.minimum(block_e, n_experts - 1)
    block_e = jnp.where(blk < n_used, block_e, block_e[n_used - 1]).astype(jnp.int32)
    zstart = jnp.maximum(pends - MOE_BLOCK, 0).astype(jnp.int32)

    n_used = n_used.reshape(1).astype(jnp.int32)
    xs = moe_dispatch(xn, dest, zstart, n_used, n_slots=n_slots, tm=dispatch_tm)
    y = moe_experts(xs, block_e, n_used, w_gate, w_up, w_down, layer=layer)
    return moe_combine(h, info, y, dest, tm=combine_tm)


def kernel(x, even_norm, even_w_in, conv_w, conv_b, conv_norm_g, conv_norm_b, sb_q_norm, sb_k_norm,
           even_w_out, odd_norm, odd_w_in, fox_forget_b, fox_q_norm, fox_k_norm, sc_w, odd_w_out,
           moe_norm, router_group_w, router_group_b, router_expert_w, router_expert_b,
           expert_w_gate, expert_w_up, expert_w_down):
    b, s, d = x.shape
    n = b * s
    h = x.reshape(n, d)
    sb_width = SB_HEADS * HEAD_DIM
    fox_width = FOX_HEADS * HEAD_DIM
    moe_tiles = dict(router_tm=512, dispatch_tm=256, combine_tm=128)

    def moe(h, layer):
        return hierarchical_moe(
            h, moe_norm[layer], router_group_w[layer], router_group_b[layer],
            router_expert_w[layer], router_expert_b[layer],
            expert_w_gate, expert_w_up, expert_w_down, layer=layer, **moe_tiles)

    u = norm_matmul(h, even_norm[0], even_w_in[0].astype(BF16), tm=1024, tn=1024)
    u3 = u.reshape(b, s, -1)
    a = conformer_conv(u3, conv_w[0], conv_b[0], conv_norm_g[0], conv_norm_b[0], ts=256)
    o = stick_breaking_attention(u3, sb_q_norm[0], sb_k_norm[0], col0=2 * CONV_CH, heads=SB_HEADS)
    h = out_proj_residual(a.reshape(n, CONV_CH), o.reshape(n, sb_width),
                          even_w_out[0].astype(BF16), h, tm=512)
    h = moe(h, 0)

    n_qkv = 3 * fox_width
    w_in = odd_w_in[0]
    w_in = jnp.concatenate(
        [w_in[:, :n_qkv], w_in[:, n_qkv + FOX_HEADS:], w_in[:, n_qkv:n_qkv + FOX_HEADS],
         jnp.zeros((d, LANES - FOX_HEADS), F32)], axis=1).astype(BF16)
    u = norm_matmul(h, odd_norm[0], w_in, tm=1024, tn=896)
    u3 = u.reshape(b, s, -1)
    c_col = forget_cumsum(u3, fox_forget_b[0], col0=n_qkv + 3 * SC_CH, tc=512)
    o = forgetting_attention(u3, c_col, fox_q_norm[0], fox_k_norm[0], heads=FOX_HEADS)
    y = short_conv(u3, sc_w[0], col0=n_qkv, ts=256)
    h = out_proj_residual(o.reshape(n, fox_width), y.reshape(n, SC_CH),
                          odd_w_out[0].astype(BF16), h, tm=512)
    h = moe(h, 1)
    return h.reshape(b, s, d)
```

```python
import functools

import jax
import jax.numpy as jnp
from jax import lax
from jax.experimental import pallas as pl
from jax.experimental.pallas import tpu as pltpu

F32 = jnp.float32
BF16 = jnp.bfloat16

HEAD_DIM = 128
CONV_CH = 1024
CONV_WIDTH = 31
SB_HEADS = 8
FOX_HEADS = 8
SC_CH = 1024
SC_WIDTH = 3
N_GROUPS = 4
EXPERTS_PER_GROUP = 8
N_EXPERTS = N_GROUPS * EXPERTS_PER_GROUP
MOE_BLOCK = 256
RMS_EPS = 1e-6
LN_EPS = 1e-5

LANES = 128
SUBLANES = 8
VMEM_CAPACITY = 64 * 1024 * 1024
VMEM_LIMIT = VMEM_CAPACITY * 7 // 8
VMEM_LIMIT_EXPERTS = VMEM_CAPACITY - 2 * 1024 * 1024


def _cparams(*sem, vmem_limit=VMEM_LIMIT):
    return pltpu.CompilerParams(dimension_semantics=sem, vmem_limit_bytes=vmem_limit)


def _sigmoid(x):
    return 1.0 / (1.0 + jnp.exp(-x))


def _softplus_neg_abs(z):
    return jnp.log(1.0 + jnp.exp(-jnp.abs(z)))


def _rms(x, g):
    ms = jnp.mean(x * x, axis=-1, keepdims=True)
    return x * lax.rsqrt(ms + RMS_EPS) * g


def _split_bf16(x, parts):
    out = []
    r = x
    for _ in range(parts - 1):
        p = r.astype(BF16)
        out.append(p)
        r = r - p.astype(F32)
    out.append(r.astype(BF16))
    return out


def _norm_matmul_kernel(x_ref, g_ref, w_ref, o_ref, xn_ref):
    @pl.when(pl.program_id(1) == 0)
    def _():
        xn_ref[...] = _rms(x_ref[...], g_ref[...]).astype(BF16)

    o_ref[...] = jnp.dot(xn_ref[...], w_ref[...], preferred_element_type=F32)


def norm_matmul(x, gain, w, *, tm, tn):
    n, d = x.shape
    f = w.shape[1]
    return pl.pallas_call(
        _norm_matmul_kernel,
        grid=(n // tm, f // tn),
        in_specs=[pl.BlockSpec((tm, d), lambda i, j: (i, 0)),
                  pl.BlockSpec((1, d), lambda i, j: (0, 0)),
                  pl.BlockSpec((d, tn), lambda i, j: (0, j))],
        out_specs=pl.BlockSpec((tm, tn), lambda i, j: (i, j)),
        out_shape=jax.ShapeDtypeStruct((n, f), F32),
        scratch_shapes=[pltpu.VMEM((tm, d), BF16)],
        compiler_params=_cparams("parallel", "arbitrary"),
        name="norm_matmul",
    )(x, gain.reshape(1, d), w)


def _conv_tile(abuf, w_ref, bias_ref, cbuf, *, ts, halo, width, rows):
    n_ch = cbuf.shape[1]
    off = halo - (width - 1)

    def chan_body(c, carry):
        lanes = pl.ds(pl.multiple_of(c * LANES, LANES), LANES)
        wc = w_ref[:, lanes]
        for r0 in range(0, ts, rows):
            if bias_ref is None:
                acc = jnp.zeros((rows, LANES), F32)
            else:
                acc = jnp.broadcast_to(bias_ref[:, lanes], (rows, LANES))
            win = abuf[pl.ds(r0, rows + halo), lanes]
            for shift in range(SUBLANES):
                taps = [k for k in range(width) if (off + k) % SUBLANES == shift]
                if not taps:
                    continue
                shifted = win if shift == 0 else pltpu.roll(win, rows + halo - shift, axis=0)
                for k in taps:
                    base = off + k - shift
                    acc = acc + wc[k:k + 1, :] * shifted[base:base + rows]
            cbuf[pl.ds(r0, rows), lanes] = acc
        return carry

    lax.fori_loop(0, n_ch // LANES, chan_body, 0)


def _carry_halo(abuf, *, ts, halo):
    s = pl.program_id(1)

    @pl.when(s == 0)
    def _():
        abuf[0:halo, :] = jnp.zeros((halo, abuf.shape[1]), F32)

    @pl.when(s > 0)
    def _():
        abuf[0:halo, :] = abuf[ts:ts + halo, :]


CONV_HALO = 32
SC_HALO = 8


def _conformer_kernel(av_ref, ag_ref, w_ref, cb_ref, lg_ref, lb_ref, o_ref, abuf, cbuf, *, ts):
    _carry_halo(abuf, ts=ts, halo=CONV_HALO)
    abuf[CONV_HALO:CONV_HALO + ts, :] = av_ref[...] * _sigmoid(ag_ref[...])
    _conv_tile(abuf, w_ref, cb_ref, cbuf, ts=ts, halo=CONV_HALO, width=CONV_WIDTH, rows=64)
    y = cbuf[...]
    mu = jnp.mean(y, axis=-1, keepdims=True)
    yc = y - mu
    var = jnp.mean(yc * yc, axis=-1, keepdims=True)
    yn = yc * lax.rsqrt(var + LN_EPS) * lg_ref[...] + lb_ref[...]
    o_ref[...] = (yn * _sigmoid(yn)).astype(BF16)


def conformer_conv(u3, conv_w, conv_b, ln_g, ln_b, *, ts):
    b, s, _ = u3.shape
    c = conv_w.shape[1]
    w_pad = jnp.zeros((CONV_HALO, c), F32).at[:CONV_WIDTH].set(conv_w)
    row = lambda a: a.reshape(1, c)
    full = lambda shape: pl.BlockSpec(shape, lambda bi, si: (0, 0))
    return pl.pallas_call(
        functools.partial(_conformer_kernel, ts=ts),
        grid=(b, s // ts),
        in_specs=[pl.BlockSpec((None, ts, c), lambda bi, si: (bi, si, 0)),
                  pl.BlockSpec((None, ts, c), lambda bi, si: (bi, si, 1)),
                  full((CONV_HALO, c)), full((1, c)), full((1, c)), full((1, c))],
        out_specs=pl.BlockSpec((None, ts, c), lambda bi, si: (bi, si, 0)),
        out_shape=jax.ShapeDtypeStruct((b, s, c), BF16),
        scratch_shapes=[pltpu.VMEM((CONV_HALO + ts, c), F32), pltpu.VMEM((ts, c), F32)],
        compiler_params=_cparams("parallel", "arbitrary"),
        name="conformer_conv",
    )(u3, u3, w_pad, row(conv_b), row(ln_g), row(ln_b))


def _short_conv_kernel(bg_ref, cg_ref, xv_ref, w_ref, o_ref, abuf, cbuf, *, ts):
    _carry_halo(abuf, ts=ts, halo=SC_HALO)
    abuf[SC_HALO:SC_HALO + ts, :] = cg_ref[...] * xv_ref[...]
    _conv_tile(abuf, w_ref, None, cbuf, ts=ts, halo=SC_HALO, width=SC_WIDTH, rows=64)
    o_ref[...] = (bg_ref[...] * cbuf[...]).astype(BF16)


def short_conv(u3, sc_w, *, col0, ts):
    b, s, _ = u3.shape
    c = sc_w.shape[1]
    j0 = col0 // c
    w_pad = jnp.zeros((SUBLANES, c), F32).at[:SC_WIDTH].set(sc_w)
    return pl.pallas_call(
        functools.partial(_short_conv_kernel, ts=ts),
        grid=(b, s // ts),
        in_specs=[pl.BlockSpec((None, ts, c), lambda bi, si: (bi, si, j0)),
                  pl.BlockSpec((None, ts, c), lambda bi, si: (bi, si, j0 + 1)),
                  pl.BlockSpec((None, ts, c), lambda bi, si: (bi, si, j0 + 2)),
                  pl.BlockSpec((SUBLANES, c), lambda bi, si: (0, 0))],
        out_specs=pl.BlockSpec((None, ts, c), lambda bi, si: (bi, si, 0)),
        out_shape=jax.ShapeDtypeStruct((b, s, c), BF16),
        scratch_shapes=[pltpu.VMEM((SC_HALO + ts, c), F32), pltpu.VMEM((ts, c), F32)],
        compiler_params=_cparams("parallel", "arbitrary"),
        name="short_conv",
    )(u3, u3, u3, w_pad)


ATTN_HEADS = 4
ATTN_T = 256


def _head(e):
    return slice(e * HEAD_DIM, (e + 1) * HEAD_DIM)


def _prep_kv(k_ref, v_ref, kg_ref, kn_ref, vb_ref):
    @pl.when(pl.program_id(2) == 0)
    def _():
        for e in range(ATTN_HEADS):
            kn_ref[:, _head(e)] = _rms(k_ref[:, _head(e)], kg_ref[...]).astype(BF16)
        vb_ref[...] = v_ref[...].astype(BF16)


def _prep_q(q_ref, qg_ref, e):
    return (_rms(q_ref[:, _head(e)], qg_ref[...]) * (HEAD_DIM ** -0.5)).astype(BF16)


def _qk(qn, kb):
    return lax.dot_general(qn, kb, (((1,), (1,)), ((), ())), preferred_element_type=F32)


SB_SUB = 128
SB_CUTOFF = -110.0


def _sb_kernel(q_ref, k_ref, v_ref, qg_ref, kg_ref, o_ref, kn_ref, vb_ref, acc_ref, r_ref):
    t = ATTN_T
    i = pl.program_id(2)
    _prep_kv(k_ref, v_ref, kg_ref, kn_ref, vb_ref)
    qn = [_prep_q(q_ref, qg_ref, e) for e in range(ATTN_HEADS)]
    acc_ref[...] = jnp.zeros(acc_ref.shape, F32)
    r_ref[...] = jnp.zeros(r_ref.shape, F32)

    jr = lax.broadcasted_iota(jnp.int32, (t, t), 0)
    sc = lax.broadcasted_iota(jnp.int32, (t, t), 1)
    later = jnp.where((jr > sc) & (jr // SB_SUB == sc // SB_SUB), 1.0, 0.0).astype(BF16)
    strict = sc < jr

    def local_terms(c, masked):
        ks = pl.multiple_of(c * t, t)
        out = []
        for e in range(ATTN_HEADS):
            z = _qk(qn[e], kn_ref[pl.ds(ks, t), _head(e)])
            log_beta = jnp.minimum(z, 0.0) - _softplus_neg_abs(z)
            log_om = log_beta - z
            if masked:
                log_om = jnp.where(strict, log_om, 0.0)
            hi, lo = _split_bf16(log_om, 2)
            suffix = (jnp.dot(hi, later, preferred_element_type=F32)
                      + jnp.dot(lo, later, preferred_element_type=F32))
            out.append((log_beta + suffix,
                        jnp.sum(log_om[:, SB_SUB:], axis=1, keepdims=True),
                        jnp.sum(log_om[:, :SB_SUB], axis=1, keepdims=True)))
        return tuple(out)

    def accumulate(c, masked, terms):
        ks = pl.multiple_of(c * t, t)
        for e in range(ATTN_HEADS):
            base, tot_near, tot_far = terms[e]
            r = r_ref[:, _head(e)]
            w = jnp.exp(base + jnp.concatenate([r + tot_near, r], axis=1))
            if masked:
                w = jnp.where(strict, w, 0.0)
            acc_ref[:, _head(e)] += jnp.dot(w.astype(BF16), vb_ref[pl.ds(ks, t), _head(e)],
                                            preferred_element_type=F32)
            r_ref[:, _head(e)] = r + (tot_near + tot_far)

    accumulate(i, True, local_terms(i, True))

    def more(carry):
        n, r_max, _ = carry
        return (n < i) & (r_max > SB_CUTOFF)

    def body(carry):
        n, _, terms = carry
        c = i - 1 - n
        ahead = local_terms(jnp.maximum(c - 1, 0), False)
        accumulate(c, False, terms)
        return n + 1, jnp.max(r_ref[...]), ahead

    first = local_terms(jnp.maximum(i - 1, 0), False)
    lax.while_loop(more, body, (jnp.int32(0), jnp.max(r_ref[...]), first))
    o_ref[...] = acc_ref[...].astype(BF16)


def _attn_specs(s, j0, heads):
    w = ATTN_HEADS * HEAD_DIM
    g0 = j0 // ATTN_HEADS
    gh = heads // ATTN_HEADS
    q_spec = pl.BlockSpec((None, ATTN_T, w), lambda bi, hi, qi: (bi, qi, g0 + hi))
    k_spec = pl.BlockSpec((None, s, w), lambda bi, hi, qi: (bi, 0, g0 + gh + hi),
                          pipeline_mode=pl.Buffered(1))
    v_spec = pl.BlockSpec((None, s, w), lambda bi, hi, qi: (bi, 0, g0 + 2 * gh + hi),
                          pipeline_mode=pl.Buffered(1))
    o_spec = pl.BlockSpec((None, ATTN_T, w), lambda bi, hi, qi: (bi, qi, hi))
    return q_spec, k_spec, v_spec, o_spec


def stick_breaking_attention(u3, q_gain, k_gain, *, col0, heads):
    b, s, _ = u3.shape
    w = ATTN_HEADS * HEAD_DIM
    q_spec, k_spec, v_spec, o_spec = _attn_specs(s, col0 // HEAD_DIM, heads)
    gain = lambda g: g.reshape(1, HEAD_DIM)
    full = lambda shape: pl.BlockSpec(shape, lambda bi, hi, qi: (0, 0))
    return pl.pallas_call(
        _sb_kernel,
        grid=(b, heads // ATTN_HEADS, s // ATTN_T),
        in_specs=[q_spec, k_spec, v_spec, full((1, HEAD_DIM)), full((1, HEAD_DIM))],
        out_specs=o_spec,
        out_shape=jax.ShapeDtypeStruct((b, s, heads * HEAD_DIM), BF16),
        scratch_shapes=[pltpu.VMEM((s, w), BF16), pltpu.VMEM((s, w), BF16),
                        pltpu.VMEM((ATTN_T, w), F32), pltpu.VMEM((ATTN_T, w), F32)],
        compiler_params=_cparams("parallel", "parallel", "arbitrary"),
        name="stick_breaking_attention",
    )(u3, u3, u3, gain(q_gain), gain(k_gain))


FOX_PREP = 512
FOX_K = 2 * HEAD_DIM


def _bias_lanes(c_rep, lane, first, sign):
    other = 3 - first
    tile = jnp.where((lane >= other) & (lane < other + 3), 1.0, 0.0)
    for k, part in enumerate(_split_bf16(sign * c_rep, 3)):
        tile = jnp.where(lane == first + k, part.astype(F32), tile)
    return tile.astype(BF16)


def _fox_kernel(q_ref, k_ref, v_ref, cc_ref, qg_ref, kg_ref, o_ref, kn_ref, vt_ref, cs_ref, acc_ref):
    t = ATTN_T
    hg = pl.program_id(1)
    i = pl.program_id(2)
    s = k_ref.shape[0]
    rows = lambda e: slice(e * HEAD_DIM, (e + 1) * HEAD_DIM)
    kcols = lambda e: slice(e * FOX_K, (e + 1) * FOX_K)

    @pl.when(i == 0)
    def _():
        lane = lax.broadcasted_iota(jnp.int32, (FOX_PREP, LANES), 1)
        for e in range(ATTN_HEADS):
            for c0 in range(0, s, FOX_PREP):
                blk = slice(c0, c0 + FOX_PREP)
                vt_ref[rows(e), blk] = v_ref[blk, _head(e)].T.astype(BF16)
                col = jnp.sum(jnp.where(lane == hg * ATTN_HEADS + e, cc_ref[blk, :], 0.0),
                              axis=1, keepdims=True)
                c_rep = jnp.broadcast_to(col, (FOX_PREP, LANES))
                cs_ref[blk, _head(e)] = c_rep
                kn = _rms(k_ref[blk, _head(e)], kg_ref[...]).astype(BF16)
                kn_ref[blk, kcols(e)] = jnp.concatenate(
                    [kn, _bias_lanes(c_rep, lane, 0, -1.0)], axis=1)

    qlane = lax.broadcasted_iota(jnp.int32, (t, LANES), 1)
    qrows = pl.ds(pl.multiple_of(i * t, t), t)
    qn = [jnp.concatenate([_prep_q(q_ref, qg_ref, e),
                           _bias_lanes(cs_ref[qrows, _head(e)], qlane, 3, 1.0)], axis=1)
          for e in range(ATTN_HEADS)]
    acc_ref[...] = jnp.zeros(acc_ref.shape, F32)
    kpos = lax.broadcasted_iota(jnp.int32, (t, t), 0)
    qpos = lax.broadcasted_iota(jnp.int32, (t, t), 1)

    def scores(j):
        ks = pl.multiple_of(j * t, t)
        return tuple(_qk(kn_ref[pl.ds(ks, t), kcols(e)], qn[e])
                     for e in range(ATTN_HEADS))

    def chunk(j, masked, stats, qk):
        ks = pl.multiple_of(j * t, t)
        out = []
        for e in range(ATTN_HEADS):
            m_prev, l_prev = stats[e]
            zt = qk[e]
            if masked:
                zt = jnp.where(kpos <= qpos, zt, -jnp.inf)
            m_new = jnp.maximum(m_prev, jnp.max(zt, axis=0, keepdims=True))
            alpha = jnp.exp(m_prev - m_new)
            pt = jnp.exp(zt - m_new)
            l_new = alpha * l_prev + jnp.sum(pt, axis=0, keepdims=True)
            acc_ref[rows(e), :] = alpha * acc_ref[rows(e), :] + jnp.dot(
                vt_ref[rows(e), pl.ds(ks, t)], pt.astype(BF16), preferred_element_type=F32)
            out.append((m_new, l_new))
        return tuple(out)

    init = tuple((jnp.full((1, t), -jnp.inf, F32), jnp.zeros((1, t), F32))
                 for _ in range(ATTN_HEADS))

    def body(j, carry):
        stats, qk = carry
        nxt = scores(j + 1)
        return chunk(j, False, stats, qk), nxt

    stats, qk = lax.fori_loop(0, i, body, (init, scores(0)))
    stats = chunk(i, True, stats, qk)
    for e in range(ATTN_HEADS):
        o_ref[:, _head(e)] = (acc_ref[rows(e), :] / stats[e][1]).T.astype(BF16)


def forgetting_attention(u3, c_col, q_gain, k_gain, *, heads):
    b, s, _ = u3.shape
    w = ATTN_HEADS * HEAD_DIM
    q_spec, k_spec, v_spec, o_spec = _attn_specs(s, 0, heads)
    gain = lambda g: g.reshape(1, HEAD_DIM)
    full = lambda shape: pl.BlockSpec(shape, lambda bi, hi, qi: (0, 0))
    return pl.pallas_call(
        _fox_kernel,
        grid=(b, heads // ATTN_HEADS, s // ATTN_T),
        in_specs=[q_spec, k_spec, v_spec,
                  pl.BlockSpec((None, s, LANES), lambda bi, hi, qi: (bi, 0, 0)),
                  full((1, HEAD_DIM)), full((1, HEAD_DIM))],
        out_specs=o_spec,
        out_shape=jax.ShapeDtypeStruct((b, s, heads * HEAD_DIM), BF16),
        scratch_shapes=[pltpu.VMEM((s, ATTN_HEADS * FOX_K), BF16), pltpu.VMEM((w, s), BF16),
                        pltpu.VMEM((s, w), F32), pltpu.VMEM((w, ATTN_T), F32)],
        compiler_params=_cparams("parallel", "parallel", "arbitrary"),
        name="forgetting_attention",
    )(u3, u3, u3, c_col, gain(q_gain), gain(k_gain))


def _forget_cumsum_kernel(f_ref, b_ref, o_ref, carry_ref, *, tc):
    @pl.when(pl.program_id(1) == 0)
    def _():
        carry_ref[...] = jnp.zeros((1, LANES), F32)

    x = f_ref[...] + b_ref[...]
    log_f = jnp.minimum(x, 0.0) - _softplus_neg_abs(x)
    r = lax.broadcasted_iota(jnp.int32, (tc, tc), 0)
    c = lax.broadcasted_iota(jnp.int32, (tc, tc), 1)
    tri = jnp.where(r >= c, 1.0, 0.0).astype(BF16)
    cs = carry_ref[...]
    for part in _split_bf16(log_f, 3):
        cs = cs + jnp.dot(tri, part, preferred_element_type=F32)
    o_ref[...] = cs
    carry_ref[...] = cs[tc - 1:tc, :]


def forget_cumsum(u3, forget_b, *, col0, tc):
    b, s, _ = u3.shape
    j0 = col0 // LANES
    b_pad = jnp.zeros((1, LANES), F32).at[0, :forget_b.shape[0]].set(forget_b)
    return pl.pallas_call(
        functools.partial(_forget_cumsum_kernel, tc=tc),
        grid=(b, s // tc),
        in_specs=[pl.BlockSpec((None, tc, LANES), lambda bi, si: (bi, si, j0)),
                  pl.BlockSpec((1, LANES), lambda bi, si: (0, 0))],
        out_specs=pl.BlockSpec((None, tc, LANES), lambda bi, si: (bi, si, 0)),
        out_shape=jax.ShapeDtypeStruct((b, s, LANES), F32),
        scratch_shapes=[pltpu.VMEM((1, LANES), F32)],
        compiler_params=_cparams("parallel", "arbitrary"),
        name="forget_cumsum",
    )(u3, b_pad)


def _out_proj_kernel(a_ref, b_ref, w1_ref, w2_ref, h_ref, o_ref):
    o_ref[...] = (h_ref[...]
                  + jnp.dot(a_ref[...], w1_ref[...], preferred_element_type=F32)
                  + jnp.dot(b_ref[...], w2_ref[...], preferred_element_type=F32))


def out_proj_residual(a, b, w, h, *, tm):
    n, ka = a.shape
    kb = b.shape[1]
    d = w.shape[1]
    return pl.pallas_call(
        _out_proj_kernel,
        grid=(n // tm,),
        in_specs=[pl.BlockSpec((tm, ka), lambda i: (i, 0)),
                  pl.BlockSpec((tm, kb), lambda i: (i, 0)),
                  pl.BlockSpec((ka, d), lambda i: (0, 0)),
                  pl.BlockSpec((kb, d), lambda i: (0, 0)),
                  pl.BlockSpec((tm, d), lambda i: (i, 0))],
        out_specs=pl.BlockSpec((tm, d), lambda i: (i, 0)),
        out_shape=jax.ShapeDtypeStruct((n, d), F32),
        compiler_params=_cparams("parallel"),
        name="out_proj_residual",
    )(a, b, w[:ka], w[ka:], h)


INFO_E, INFO_GATE, INFO_RANK = 0, 2, 4


def _lane_pick(x, lane, idx):
    return jnp.sum(jnp.where(lane == idx, x, 0.0), axis=1, keepdims=True)


def _router_kernel(h_ref, g_ref, wh_ref, wl_ref, b_ref, xn_ref, info_ref, cnt_ref, carry_ref, *, tm):
    @pl.when(pl.program_id(0) == 0)
    def _():
        carry_ref[...] = jnp.zeros((1, LANES), F32)

    xn = _rms(h_ref[...], g_ref[...])
    xn_ref[...] = xn

    xh, xl = _split_bf16(xn, 2)
    wh = wh_ref[...]
    logits = (jnp.dot(xh, wh, preferred_element_type=F32)
              + jnp.dot(xl, wh, preferred_element_type=F32)
              + jnp.dot(xh, wl_ref[...], preferred_element_type=F32)) + b_ref[...]

    lane = lax.broadcasted_iota(jnp.int32, (tm, LANES), 1).astype(F32)
    neg = -jnp.inf
    big = float(LANES)
    gl = jnp.where(lane < N_GROUPS, logits, neg)
    gmax = jnp.max(gl, axis=1, keepdims=True)
    g_top_p = 1.0 / jnp.sum(jnp.exp(gl - gmax), axis=1, keepdims=True)
    g_idx = jnp.min(jnp.where(gl == gmax, lane, big), axis=1, keepdims=True)

    lo = N_GROUPS + EXPERTS_PER_GROUP * g_idx
    el = jnp.where((lane >= lo) & (lane < lo + EXPERTS_PER_GROUP), logits, neg)
    m1 = jnp.max(el, axis=1, keepdims=True)
    i1 = jnp.min(jnp.where(el == m1, lane, big), axis=1, keepdims=True)
    el2 = jnp.where(lane == i1, neg, el)
    m2 = jnp.max(el2, axis=1, keepdims=True)
    i2 = jnp.min(jnp.where(el2 == m2, lane, big), axis=1, keepdims=True)
    ratio = jnp.exp(m2 - m1)
    p1 = 1.0 / (1.0 + ratio)
    p2 = ratio * p1
    e1 = i1 - N_GROUPS
    e2 = i2 - N_GROUPS

    onehot = jnp.where((lane == e1) | (lane == e2), 1.0, 0.0)
    r = lax.broadcasted_iota(jnp.int32, (tm, tm), 0)
    c = lax.broadcasted_iota(jnp.int32, (tm, tm), 1)
    before = jnp.where(r > c, 1.0, 0.0).astype(BF16)
    cnt = jnp.dot(before, onehot.astype(BF16), preferred_element_type=F32) + carry_ref[...]
    rank1 = _lane_pick(cnt, lane, e1)
    rank2 = _lane_pick(cnt, lane, e2)
    total = carry_ref[...] + jnp.sum(onehot, axis=0, keepdims=True)
    carry_ref[...] = total
    cnt_ref[...] = jnp.broadcast_to(total, (SUBLANES, LANES))

    info = jnp.zeros((tm, LANES), F32)
    for k, val in ((INFO_E, e1), (INFO_E + 1, e2), (INFO_GATE, g_top_p * p1),
                   (INFO_GATE + 1, g_top_p * p2), (INFO_RANK, rank1), (INFO_RANK + 1, rank2)):
        info = jnp.where(lane == k, val, info)
    info_ref[...] = info


def moe_router(h, gain, group_w, group_b, expert_w, expert_b, *, tm):
    n, d = h.shape
    n_logits = N_GROUPS + N_EXPERTS
    w = jnp.zeros((d, LANES), F32).at[:, :N_GROUPS].set(group_w).at[:, N_GROUPS:n_logits].set(expert_w)
    bias = jnp.zeros((1, LANES), F32).at[0, :N_GROUPS].set(group_b).at[0, N_GROUPS:n_logits].set(expert_b)
    w_hi = w.astype(BF16)
    w_lo = (w - w_hi.astype(F32)).astype(BF16)
    full = lambda shape: pl.BlockSpec(shape, lambda i: (0, 0))
    return pl.pallas_call(
        functools.partial(_router_kernel, tm=tm),
        grid=(n // tm,),
        in_specs=[pl.BlockSpec((tm, d), lambda i: (i, 0)), full((1, d)),
                  full((d, LANES)), full((d, LANES)), full((1, LANES))],
        out_specs=[pl.BlockSpec((tm, d), lambda i: (i, 0)),
                   pl.BlockSpec((tm, LANES), lambda i: (i, 0)),
                   full((SUBLANES, LANES))],
        out_shape=[jax.ShapeDtypeStruct((n, d), F32),
                   jax.ShapeDtypeStruct((n, LANES), F32),
                   jax.ShapeDtypeStruct((SUBLANES, LANES), F32)],
        scratch_shapes=[pltpu.VMEM((1, LANES), F32)],
        compiler_params=_cparams("arbitrary"),
        name="moe_router",
    )(h, gain.reshape(1, d), w_hi, w_lo, bias)


def _row(ref, i):
    return ref.at[pl.ds(i, 1), :]


def _row_copy(src, dst, sem):
    return pltpu.make_async_copy(src, dst, sem)


ROW_DMA_UNROLL = 8


def _dispatch_kernel(zstart_ref, nu_ref, dest_ref, xn_ref, xs_ref, zero_ref, sem, *, tm, n_experts):
    tile = pl.program_id(0)
    n_blocks = xs_ref.shape[0] // MOE_BLOCK

    @pl.when(tile == 0)
    def _():
        zero_ref[...] = jnp.zeros(zero_ref.shape, F32)
        block = lambda start: _row_copy(
            zero_ref, xs_ref.at[pl.ds(pl.multiple_of(start, MOE_BLOCK), MOE_BLOCK), :], sem)
        for e in range(n_experts):
            block(zstart_ref[e]).start()
        for e in range(n_experts):
            block(0).wait()

        def tail(i, carry):
            cp = block(i * MOE_BLOCK)
            cp.start()
            cp.wait()
            return carry

        lax.fori_loop(nu_ref[0], n_blocks, tail, 0)

    def issue(t, carry):
        _row_copy(_row(xn_ref, t), _row(xs_ref, dest_ref[0, 2 * t]), sem).start()
        _row_copy(_row(xn_ref, t), _row(xs_ref, dest_ref[0, 2 * t + 1]), sem).start()
        return carry

    lax.fori_loop(0, tm, issue, 0, unroll=ROW_DMA_UNROLL)

    def drain(t, carry):
        _row_copy(_row(xn_ref, 0), _row(xs_ref, 0), sem).wait()
        _row_copy(_row(xn_ref, 0), _row(xs_ref, 0), sem).wait()
        return carry

    lax.fori_loop(0, tm, drain, 0)


def moe_dispatch(xn, dest, zstart, n_used, *, n_slots, tm):
    n, d = xn.shape
    dest3 = dest.reshape(n // tm, 1, 2 * tm)
    return pl.pallas_call(
        functools.partial(_dispatch_kernel, tm=tm, n_experts=zstart.shape[0]),
        grid_spec=pltpu.PrefetchScalarGridSpec(
            num_scalar_prefetch=2,
            grid=(n // tm,),
            in_specs=[pl.BlockSpec((None, 1, 2 * tm), lambda i, z, nu: (i, 0, 0),
                                   memory_space=pltpu.SMEM),
                      pl.BlockSpec((tm, d), lambda i, z, nu: (i, 0))],
            out_specs=pl.BlockSpec(memory_space=pl.ANY),
            scratch_shapes=[pltpu.VMEM((MOE_BLOCK, d), F32),
                            pltpu.SemaphoreType.DMA(())]),
        out_shape=jax.ShapeDtypeStruct((n_slots, d), F32),
        compiler_params=_cparams("arbitrary"),
        name="moe_dispatch",
    )(zstart, n_used, dest3, xn)


EXPERT_FF_CHUNK = 512


def _expert_kernel(plan_ref, nu_ref, x_ref, wg_hbm, wu_hbm, wd_hbm, y_ref,
                   wg_buf, wu_buf, wd_buf, sems, *, layer):
    i = pl.program_id(0)
    slot = plan_ref[1, i]

    def copies(expert, s):
        return [pltpu.make_async_copy(hbm.at[layer, expert], buf.at[s], sems.at[s])
                for hbm, buf in ((wg_hbm, wg_buf), (wu_hbm, wu_buf), (wd_hbm, wd_buf))]

    @pl.when(i == 0)
    def _():
        for cp in copies(plan_ref[0, 0], 0):
            cp.start()

    @pl.when(plan_ref[2, i] == 1)
    def _():
        for cp in copies(plan_ref[0, i], slot):
            cp.wait()

        @pl.when(plan_ref[3, i] >= 0)
        def _():
            for cp in copies(plan_ref[3, i], 1 - slot):
                cp.start()

    @pl.when(i < nu_ref[0])
    def _():
        x = x_ref[...].astype(BF16)
        wg, wu, wd = wg_buf.at[slot], wu_buf.at[slot], wd_buf.at[slot]
        y = None
        for c0 in range(0, wg.shape[1], EXPERT_FF_CHUNK):
            cols = slice(c0, c0 + EXPERT_FF_CHUNK)
            gate = jnp.dot(x, wg[:, cols].astype(BF16), preferred_element_type=F32)
            up = jnp.dot(x, wu[:, cols].astype(BF16), preferred_element_type=F32)
            hidden = (gate * _sigmoid(gate) * up).astype(BF16)
            part = jnp.dot(hidden, wd[cols, :].astype(BF16), preferred_element_type=F32)
            y = part if y is None else y + part
        y_ref[...] = y

    @pl.when(i >= nu_ref[0])
    def _():
        y_ref[...] = jnp.zeros(y_ref.shape, F32)


def moe_experts(xs, plan, n_used, w_gate, w_up, w_down, *, layer):
    n_slots = xs.shape[0]
    n_blocks = n_slots // MOE_BLOCK
    _, _, d, ff = w_gate.shape
    x_map = lambda i, plan, nu: (jnp.minimum(i, nu[0] - 1), 0)
    hbm = pl.BlockSpec(memory_space=pl.ANY)
    return pl.pallas_call(
        functools.partial(_expert_kernel, layer=layer),
        grid_spec=pltpu.PrefetchScalarGridSpec(
            num_scalar_prefetch=2,
            grid=(n_blocks,),
            in_specs=[pl.BlockSpec((MOE_BLOCK, d), x_map), hbm, hbm, hbm],
            out_specs=pl.BlockSpec((MOE_BLOCK, d), lambda i, plan, nu: (i, 0)),
            scratch_shapes=[pltpu.VMEM((2, d, ff), F32), pltpu.VMEM((2, d, ff), F32),
                            pltpu.VMEM((2, ff, d), F32), pltpu.SemaphoreType.DMA((2,))]),
        out_shape=jax.ShapeDtypeStruct((n_slots, d), F32),
        compiler_params=_cparams("arbitrary", vmem_limit=VMEM_LIMIT_EXPERTS),
        name="moe_experts",
    )(plan, n_used, xs, w_gate, w_up, w_down)


def _combine_kernel(d0_ref, d1_ref, d2_ref, h_ref, info_ref, y_ref, o_ref, buf1, buf2, sems, *, tm):
    p = pl.program_id(0)

    def request(d_ref, s):
        def issue(t, carry):
            _row_copy(_row(y_ref, d_ref[0, 2 * t]), _row(buf1.at[s], t), sems.at[s]).start()
            _row_copy(_row(y_ref, d_ref[0, 2 * t + 1]), _row(buf2.at[s], t), sems.at[s]).start()
            return carry

        lax.fori_loop(0, tm, issue, 0, unroll=ROW_DMA_UNROLL)

    def combine(s):
        def drain(t, carry):
            _row_copy(_row(y_ref, 0), _row(buf1.at[s], 0), sems.at[s]).wait()
            _row_copy(_row(y_ref, 0), _row(buf2.at[s], 0), sems.at[s]).wait()
            return carry

        lax.fori_loop(0, tm, drain, 0)
        tile = slice(s * tm, (s + 1) * tm)
        info = info_ref[tile, :]
        lane = lax.broadcasted_iota(jnp.int32, (tm, LANES), 1)
        g1 = jnp.sum(jnp.where(lane == INFO_GATE, info, 0.0), axis=1, keepdims=True)
        g2 = jnp.sum(jnp.where(lane == INFO_GATE + 1, info, 0.0), axis=1, keepdims=True)
        o_ref[tile, :] = h_ref[tile, :] + g1 * buf1[s] + g2 * buf2[s]

    @pl.when(p == 0)
    def _():
        request(d0_ref, 0)

    request(d1_ref, 1)
    combine(0)

    @pl.when(p + 1 < pl.num_programs(0))
    def _():
        request(d2_ref, 0)

    combine(1)


def moe_combine(h, info, y, dest, *, tm):
    n, d = h.shape
    tiles = n // tm
    dest3 = dest.reshape(tiles, 1, 2 * tm)
    dest_spec = lambda index_map: pl.BlockSpec((None, 1, 2 * tm), index_map, memory_space=pltpu.SMEM)
    buf = pltpu.VMEM((2, tm, d), F32)
    return pl.pallas_call(
        functools.partial(_combine_kernel, tm=tm),
        grid=(tiles // 2,),
        in_specs=[dest_spec(lambda p: (2 * p, 0, 0)),
                  dest_spec(lambda p: (2 * p + 1, 0, 0)),
                  dest_spec(lambda p: (jnp.minimum(2 * p + 2, tiles - 1), 0, 0)),
                  pl.BlockSpec((2 * tm, d), lambda p: (p, 0)),
                  pl.BlockSpec((2 * tm, LANES), lambda p: (p, 0)),
                  pl.BlockSpec(memory_space=pl.ANY)],
        out_specs=pl.BlockSpec((2 * tm, d), lambda p: (p, 0)),
        out_shape=jax.ShapeDtypeStruct((n, d), F32),
        scratch_shapes=[buf, buf, pltpu.SemaphoreType.DMA((2,))],
        compiler_params=_cparams("arbitrary"),
        name="moe_combine",
    )(dest3, dest3, dest3, h, info, y)


def hierarchical_moe(h, gain, group_w, group_b, expert_w, expert_b, w_gate, w_up, w_down,
                     *, layer, router_tm, dispatch_tm, combine_tm):
    n, d = h.shape
    xn, info, cnt = moe_router(h, gain, group_w, group_b, expert_w, expert_b, tm=router_tm)

    n_experts = w_gate.shape[1]
    n_blocks = -(-2 * n // MOE_BLOCK) + n_experts
    n_slots = n_blocks * MOE_BLOCK
    experts = info[:, INFO_E:INFO_E + 2].astype(jnp.int32)
    ranks = info[:, INFO_RANK:INFO_RANK + 2].astype(jnp.int32)
    counts = cnt[0, :n_experts].astype(jnp.int32)
    padded = (counts + MOE_BLOCK - 1) // MOE_BLOCK * MOE_BLOCK
    pends = jnp.cumsum(padded)
    pstarts = pends - padded
    dest = pstarts[experts] + ranks
    n_used = pends[-1] // MOE_BLOCK
    blk = jnp.arange(n_blocks, dtype=jnp.int32)
    block_e = jnp.sum((pends[None, :] <= (blk * MOE_BLOCK)[:, None]).astype(jnp.int32), axis=1)
    block_e = jnp.minimum(block_e, n_experts - 1)
    block_e = jnp.where(blk < n_used, block_e, block_e[n_used - 1]).astype(jnp.int32)
    eid = jnp.arange(n_experts, dtype=jnp.int32)
    nonempty = padded > 0
    slot_e = (jnp.cumsum(nonempty.astype(jnp.int32)) - 1) % 2
    later = jnp.where(nonempty[None, :] & (eid[None, :] > eid[:, None]), eid[None, :], n_experts)
    next_e = jnp.min(later, axis=1)
    next_e = jnp.where(next_e < n_experts, next_e, -1)
    first = (blk < n_used) & ((blk == 0) | (block_e != jnp.roll(block_e, 1)))
    plan = jnp.stack([block_e, slot_e[block_e], first.astype(jnp.int32),
                      next_e[block_e]]).astype(jnp.int32)
    zstart = jnp.maximum(pends - MOE_BLOCK, 0).astype(jnp.int32)

    n_used = n_used.reshape(1).astype(jnp.int32)
    xs = moe_dispatch(xn, dest, zstart, n_used, n_slots=n_slots, tm=dispatch_tm)
    y = moe_experts(xs, plan, n_used, w_gate, w_up, w_down, layer=layer)
    return moe_combine(h, info, y, dest, tm=combine_tm)


def kernel(x, even_norm, even_w_in, conv_w, conv_b, conv_norm_g, conv_norm_b, sb_q_norm, sb_k_norm,
           even_w_out, odd_norm, odd_w_in, fox_forget_b, fox_q_norm, fox_k_norm, sc_w, odd_w_out,
           moe_norm, router_group_w, router_group_b, router_expert_w, router_expert_b,
           expert_w_gate, expert_w_up, expert_w_down):
    b, s, d = x.shape
    n = b * s
    h = x.reshape(n, d)
    sb_width = SB_HEADS * HEAD_DIM
    fox_width = FOX_HEADS * HEAD_DIM
    moe_tiles = dict(router_tm=512, dispatch_tm=256, combine_tm=128)

    def moe(h, layer):
        return hierarchical_moe(
            h, moe_norm[layer], router_group_w[layer], router_group_b[layer],
            router_expert_w[layer], router_expert_b[layer],
            expert_w_gate, expert_w_up, expert_w_down, layer=layer, **moe_tiles)

    u = norm_matmul(h, even_norm[0], even_w_in[0].astype(BF16), tm=1024, tn=1024)
    u3 = u.reshape(b, s, -1)
    a = conformer_conv(u3, conv_w[0], conv_b[0], conv_norm_g[0], conv_norm_b[0], ts=256)
    o = stick_breaking_attention(u3, sb_q_norm[0], sb_k_norm[0], col0=2 * CONV_CH, heads=SB_HEADS)
    h = out_proj_residual(a.reshape(n, CONV_CH), o.reshape(n, sb_width),
                          even_w_out[0].astype(BF16), h, tm=512)
    h = moe(h, 0)

    n_qkv = 3 * fox_width
    w_in = odd_w_in[0]
    w_in = jnp.concatenate(
        [w_in[:, :n_qkv], w_in[:, n_qkv + FOX_HEADS:], w_in[:, n_qkv:n_qkv + FOX_HEADS],
         jnp.zeros((d, LANES - FOX_HEADS), F32)], axis=1).astype(BF16)
    u = norm_matmul(h, odd_norm[0], w_in, tm=1024, tn=896)
    u3 = u.reshape(b, s, -1)
    c_col = forget_cumsum(u3, fox_forget_b[0], col0=n_qkv + 3 * SC_CH, tc=512)
    o = forgetting_attention(u3, c_col, fox_q_norm[0], fox_k_norm[0], heads=FOX_HEADS)
    y = short_conv(u3, sc_w[0], col0=n_qkv, ts=256)
    h = out_proj_residual(o.reshape(n, fox_width), y.reshape(n, SC_CH),
                          odd_w_out[0].astype(BF16), h, tm=512)
    h = moe(h, 1)
    return h.reshape(b, s, d)
```

```python
import functools

import jax
import jax.numpy as jnp
from jax import lax
from jax.experimental import pallas as pl
from jax.experimental.pallas import tpu as pltpu

F32 = jnp.float32
BF16 = jnp.bfloat16

HEAD_DIM = 128
CONV_CH = 1024
CONV_WIDTH = 31
SB_HEADS = 8
FOX_HEADS = 8
SC_CH = 1024
SC_WIDTH = 3
N_GROUPS = 4
EXPERTS_PER_GROUP = 8
N_EXPERTS = N_GROUPS * EXPERTS_PER_GROUP
MOE_BLOCK = 256
RMS_EPS = 1e-6
LN_EPS = 1e-5

LANES = 128
SUBLANES = 8
VMEM_CAPACITY = 64 * 1024 * 1024
VMEM_LIMIT = VMEM_CAPACITY * 7 // 8
VMEM_LIMIT_EXPERTS = VMEM_CAPACITY - 2 * 1024 * 1024


def _cparams(*sem, vmem_limit=VMEM_LIMIT):
    return pltpu.CompilerParams(dimension_semantics=sem, vmem_limit_bytes=vmem_limit)


def _sigmoid(x):
    return 1.0 / (1.0 + jnp.exp(-x))


def _softplus_neg_abs(z):
    return jnp.log(1.0 + jnp.exp(-jnp.abs(z)))


def _rms(x, g):
    ms = jnp.mean(x * x, axis=-1, keepdims=True)
    return x * lax.rsqrt(ms + RMS_EPS) * g


def _split_bf16(x, parts):
    out = []
    r = x
    for _ in range(parts - 1):
        p = r.astype(BF16)
        out.append(p)
        r = r - p.astype(F32)
    out.append(r.astype(BF16))
    return out


def _norm_matmul_kernel(x_ref, g_ref, w_ref, o_ref, xn_ref):
    @pl.when(pl.program_id(1) == 0)
    def _():
        xn_ref[...] = _rms(x_ref[...], g_ref[...]).astype(BF16)

    o_ref[...] = jnp.dot(xn_ref[...], w_ref[...], preferred_element_type=F32)


def norm_matmul(x, gain, w, *, tm, tn):
    n, d = x.shape
    f = w.shape[1]
    return pl.pallas_call(
        _norm_matmul_kernel,
        grid=(n // tm, f // tn),
        in_specs=[pl.BlockSpec((tm, d), lambda i, j: (i, 0)),
                  pl.BlockSpec((1, d), lambda i, j: (0, 0)),
                  pl.BlockSpec((d, tn), lambda i, j: (0, j))],
        out_specs=pl.BlockSpec((tm, tn), lambda i, j: (i, j)),
        out_shape=jax.ShapeDtypeStruct((n, f), F32),
        scratch_shapes=[pltpu.VMEM((tm, d), BF16)],
        compiler_params=_cparams("parallel", "arbitrary"),
        name="norm_matmul",
    )(x, gain.reshape(1, d), w)


def _conv_tile(abuf, w_ref, bias_ref, cbuf, *, ts, halo, width, rows):
    n_ch = cbuf.shape[1]
    off = halo - (width - 1)

    def chan_body(c, carry):
        lanes = pl.ds(pl.multiple_of(c * LANES, LANES), LANES)
        wc = w_ref[:, lanes]
        for r0 in range(0, ts, rows):
            if bias_ref is None:
                acc = jnp.zeros((rows, LANES), F32)
            else:
                acc = jnp.broadcast_to(bias_ref[:, lanes], (rows, LANES))
            win = abuf[pl.ds(r0, rows + halo), lanes]
            for shift in range(SUBLANES):
                taps = [k for k in range(width) if (off + k) % SUBLANES == shift]
                if not taps:
                    continue
                shifted = win if shift == 0 else pltpu.roll(win, rows + halo - shift, axis=0)
                for k in taps:
                    base = off + k - shift
                    acc = acc + wc[k:k + 1, :] * shifted[base:base + rows]
            cbuf[pl.ds(r0, rows), lanes] = acc
        return carry

    lax.fori_loop(0, n_ch // LANES, chan_body, 0)


def _carry_halo(abuf, *, ts, halo):
    s = pl.program_id(1)

    @pl.when(s == 0)
    def _():
        abuf[0:halo, :] = jnp.zeros((halo, abuf.shape[1]), F32)

    @pl.when(s > 0)
    def _():
        abuf[0:halo, :] = abuf[ts:ts + halo, :]


CONV_HALO = 32
SC_HALO = 8


def _conformer_kernel(av_ref, ag_ref, w_ref, cb_ref, lg_ref, lb_ref, o_ref, abuf, cbuf, *, ts):
    _carry_halo(abuf, ts=ts, halo=CONV_HALO)
    abuf[CONV_HALO:CONV_HALO + ts, :] = av_ref[...] * _sigmoid(ag_ref[...])
    _conv_tile(abuf, w_ref, cb_ref, cbuf, ts=ts, halo=CONV_HALO, width=CONV_WIDTH, rows=64)
    y = cbuf[...]
    mu = jnp.mean(y, axis=-1, keepdims=True)
    yc = y - mu
    var = jnp.mean(yc * yc, axis=-1, keepdims=True)
    yn = yc * lax.rsqrt(var + LN_EPS) * lg_ref[...] + lb_ref[...]
    o_ref[...] = (yn * _sigmoid(yn)).astype(BF16)


def conformer_conv(u3, conv_w, conv_b, ln_g, ln_b, *, ts):
    b, s, _ = u3.shape
    c = conv_w.shape[1]
    w_pad = jnp.zeros((CONV_HALO, c), F32).at[:CONV_WIDTH].set(conv_w)
    row = lambda a: a.reshape(1, c)
    full = lambda shape: pl.BlockSpec(shape, lambda bi, si: (0, 0))
    return pl.pallas_call(
        functools.partial(_conformer_kernel, ts=ts),
        grid=(b, s // ts),
        in_specs=[pl.BlockSpec((None, ts, c), lambda bi, si: (bi, si, 0)),
                  pl.BlockSpec((None, ts, c), lambda bi, si: (bi, si, 1)),
                  full((CONV_HALO, c)), full((1, c)), full((1, c)), full((1, c))],
        out_specs=pl.BlockSpec((None, ts, c), lambda bi, si: (bi, si, 0)),
        out_shape=jax.ShapeDtypeStruct((b, s, c), BF16),
        scratch_shapes=[pltpu.VMEM((CONV_HALO + ts, c), F32), pltpu.VMEM((ts, c), F32)],
        compiler_params=_cparams("parallel", "arbitrary"),
        name="conformer_conv",
    )(u3, u3, w_pad, row(conv_b), row(ln_g), row(ln_b))


def _short_conv_kernel(bg_ref, cg_ref, xv_ref, w_ref, o_ref, abuf, cbuf, *, ts):
    _carry_halo(abuf, ts=ts, halo=SC_HALO)
    abuf[SC_HALO:SC_HALO + ts, :] = cg_ref[...] * xv_ref[...]
    _conv_tile(abuf, w_ref, None, cbuf, ts=ts, halo=SC_HALO, width=SC_WIDTH, rows=64)
    o_ref[...] = (bg_ref[...] * cbuf[...]).astype(BF16)


def short_conv(u3, sc_w, *, col0, ts):
    b, s, _ = u3.shape
    c = sc_w.shape[1]
    j0 = col0 // c
    w_pad = jnp.zeros((SUBLANES, c), F32).at[:SC_WIDTH].set(sc_w)
    return pl.pallas_call(
        functools.partial(_short_conv_kernel, ts=ts),
        grid=(b, s // ts),
        in_specs=[pl.BlockSpec((None, ts, c), lambda bi, si: (bi, si, j0)),
                  pl.BlockSpec((None, ts, c), lambda bi, si: (bi, si, j0 + 1)),
                  pl.BlockSpec((None, ts, c), lambda bi, si: (bi, si, j0 + 2)),
                  pl.BlockSpec((SUBLANES, c), lambda bi, si: (0, 0))],
        out_specs=pl.BlockSpec((None, ts, c), lambda bi, si: (bi, si, 0)),
        out_shape=jax.ShapeDtypeStruct((b, s, c), BF16),
        scratch_shapes=[pltpu.VMEM((SC_HALO + ts, c), F32), pltpu.VMEM((ts, c), F32)],
        compiler_params=_cparams("parallel", "arbitrary"),
        name="short_conv",
    )(u3, u3, u3, w_pad)


ATTN_HEADS = 4
ATTN_T = 256


def _head(e):
    return slice(e * HEAD_DIM, (e + 1) * HEAD_DIM)


def _prep_kv(k_ref, v_ref, kg_ref, kn_ref, vb_ref):
    @pl.when(pl.program_id(2) == 0)
    def _():
        for e in range(ATTN_HEADS):
            kn_ref[:, _head(e)] = _rms(k_ref[:, _head(e)], kg_ref[...]).astype(BF16)
        vb_ref[...] = v_ref[...].astype(BF16)


def _prep_q(q_ref, qg_ref, e):
    return (_rms(q_ref[:, _head(e)], qg_ref[...]) * (HEAD_DIM ** -0.5)).astype(BF16)


def _qk(qn, kb):
    return lax.dot_general(qn, kb, (((1,), (1,)), ((), ())), preferred_element_type=F32)


SB_SUB = 128
SB_CUTOFF = -110.0


def _sb_kernel(q_ref, k_ref, v_ref, qg_ref, kg_ref, o_ref, kn_ref, vb_ref, acc_ref, r_ref):
    t = ATTN_T
    i = pl.program_id(2)
    _prep_kv(k_ref, v_ref, kg_ref, kn_ref, vb_ref)
    qn = [_prep_q(q_ref, qg_ref, e) for e in range(ATTN_HEADS)]
    acc_ref[...] = jnp.zeros(acc_ref.shape, F32)
    r_ref[...] = jnp.zeros(r_ref.shape, F32)

    jr = lax.broadcasted_iota(jnp.int32, (t, t), 0)
    sc = lax.broadcasted_iota(jnp.int32, (t, t), 1)
    later = jnp.where((jr > sc) & (jr // SB_SUB == sc // SB_SUB), 1.0, 0.0).astype(BF16)
    strict = sc < jr

    def local_terms(c, masked):
        ks = pl.multiple_of(c * t, t)
        out = []
        for e in range(ATTN_HEADS):
            z = _qk(qn[e], kn_ref[pl.ds(ks, t), _head(e)])
            log_beta = jnp.minimum(z, 0.0) - _softplus_neg_abs(z)
            log_om = log_beta - z
            if masked:
                log_om = jnp.where(strict, log_om, 0.0)
            hi, lo = _split_bf16(log_om, 2)
            suffix = (jnp.dot(hi, later, preferred_element_type=F32)
                      + jnp.dot(lo, later, preferred_element_type=F32))
            out.append((log_beta + suffix,
                        jnp.sum(log_om[:, SB_SUB:], axis=1, keepdims=True),
                        jnp.sum(log_om[:, :SB_SUB], axis=1, keepdims=True)))
        return tuple(out)

    def accumulate(c, masked, terms):
        ks = pl.multiple_of(c * t, t)
        for e in range(ATTN_HEADS):
            base, tot_near, tot_far = terms[e]
            r = r_ref[:, _head(e)]
            w = jnp.exp(base + jnp.concatenate([r + tot_near, r], axis=1))
            if masked:
                w = jnp.where(strict, w, 0.0)
            acc_ref[:, _head(e)] += jnp.dot(w.astype(BF16), vb_ref[pl.ds(ks, t), _head(e)],
                                            preferred_element_type=F32)
            r_ref[:, _head(e)] = r + (tot_near + tot_far)

    accumulate(i, True, local_terms(i, True))

    def more(carry):
        n, r_max, _ = carry
        return (n < i) & (r_max > SB_CUTOFF)

    def body(carry):
        n, _, terms = carry
        c = i - 1 - n
        ahead = local_terms(jnp.maximum(c - 1, 0), False)
        accumulate(c, False, terms)
        return n + 1, jnp.max(r_ref[...]), ahead

    first = local_terms(jnp.maximum(i - 1, 0), False)
    lax.while_loop(more, body, (jnp.int32(0), jnp.max(r_ref[...]), first))
    o_ref[...] = acc_ref[...].astype(BF16)


def _attn_specs(s, j0, heads):
    w = ATTN_HEADS * HEAD_DIM
    g0 = j0 // ATTN_HEADS
    gh = heads // ATTN_HEADS
    q_spec = pl.BlockSpec((None, ATTN_T, w), lambda bi, hi, qi: (bi, qi, g0 + hi))
    k_spec = pl.BlockSpec((None, s, w), lambda bi, hi, qi: (bi, 0, g0 + gh + hi),
                          pipeline_mode=pl.Buffered(1))
    v_spec = pl.BlockSpec((None, s, w), lambda bi, hi, qi: (bi, 0, g0 + 2 * gh + hi),
                          pipeline_mode=pl.Buffered(1))
    o_spec = pl.BlockSpec((None, ATTN_T, w), lambda bi, hi, qi: (bi, qi, hi))
    return q_spec, k_spec, v_spec, o_spec


def stick_breaking_attention(u3, q_gain, k_gain, *, col0, heads):
    b, s, _ = u3.shape
    w = ATTN_HEADS * HEAD_DIM
    q_spec, k_spec, v_spec, o_spec = _attn_specs(s, col0 // HEAD_DIM, heads)
    gain = lambda g: g.reshape(1, HEAD_DIM)
    full = lambda shape: pl.BlockSpec(shape, lambda bi, hi, qi: (0, 0))
    return pl.pallas_call(
        _sb_kernel,
        grid=(b, heads // ATTN_HEADS, s // ATTN_T),
        in_specs=[q_spec, k_spec, v_spec, full((1, HEAD_DIM)), full((1, HEAD_DIM))],
        out_specs=o_spec,
        out_shape=jax.ShapeDtypeStruct((b, s, heads * HEAD_DIM), BF16),
        scratch_shapes=[pltpu.VMEM((s, w), BF16), pltpu.VMEM((s, w), BF16),
                        pltpu.VMEM((ATTN_T, w), F32), pltpu.VMEM((ATTN_T, w), F32)],
        compiler_params=_cparams("parallel", "parallel", "arbitrary"),
        name="stick_breaking_attention",
    )(u3, u3, u3, gain(q_gain), gain(k_gain))


FOX_PREP = 512
FOX_K = 2 * HEAD_DIM


def _bias_lanes(c_rep, lane, first, sign):
    other = 3 - first
    tile = jnp.where((lane >= other) & (lane < other + 3), 1.0, 0.0)
    for k, part in enumerate(_split_bf16(sign * c_rep, 3)):
        tile = jnp.where(lane == first + k, part.astype(F32), tile)
    return tile.astype(BF16)


def _fox_kernel(q_ref, k_ref, v_ref, cc_ref, qg_ref, kg_ref, o_ref, kn_ref, vt_ref, cs_ref, acc_ref):
    t = ATTN_T
    hg = pl.program_id(1)
    i = pl.program_id(2)
    s = k_ref.shape[0]
    rows = lambda e: slice(e * HEAD_DIM, (e + 1) * HEAD_DIM)
    kcols = lambda e: slice(e * FOX_K, (e + 1) * FOX_K)

    @pl.when(i == 0)
    def _():
        lane = lax.broadcasted_iota(jnp.int32, (FOX_PREP, LANES), 1)
        for e in range(ATTN_HEADS):
            for c0 in range(0, s, FOX_PREP):
                blk = slice(c0, c0 + FOX_PREP)
                vt_ref[rows(e), blk] = v_ref[blk, _head(e)].T.astype(BF16)
                col = jnp.sum(jnp.where(lane == hg * ATTN_HEADS + e, cc_ref[blk, :], 0.0),
                              axis=1, keepdims=True)
                c_rep = jnp.broadcast_to(col, (FOX_PREP, LANES))
                cs_ref[blk, _head(e)] = c_rep
                kn = _rms(k_ref[blk, _head(e)], kg_ref[...]).astype(BF16)
                kn_ref[blk, kcols(e)] = jnp.concatenate(
                    [kn, _bias_lanes(c_rep, lane, 0, -1.0)], axis=1)

    qlane = lax.broadcasted_iota(jnp.int32, (t, LANES), 1)
    qrows = pl.ds(pl.multiple_of(i * t, t), t)
    qn = [jnp.concatenate([_prep_q(q_ref, qg_ref, e),
                           _bias_lanes(cs_ref[qrows, _head(e)], qlane, 3, 1.0)], axis=1)
          for e in range(ATTN_HEADS)]
    acc_ref[...] = jnp.zeros(acc_ref.shape, F32)
    kpos = lax.broadcasted_iota(jnp.int32, (t, t), 0)
    qpos = lax.broadcasted_iota(jnp.int32, (t, t), 1)

    def scores(j):
        ks = pl.multiple_of(j * t, t)
        return tuple(_qk(kn_ref[pl.ds(ks, t), kcols(e)], qn[e])
                     for e in range(ATTN_HEADS))

    def chunk(j, masked, stats, qk):
        ks = pl.multiple_of(j * t, t)
        out = []
        for e in range(ATTN_HEADS):
            m_prev, l_prev = stats[e]
            zt = qk[e]
            if masked:
                zt = jnp.where(kpos <= qpos, zt, -jnp.inf)
            m_new = jnp.maximum(m_prev, jnp.max(zt, axis=0, keepdims=True))
            alpha = jnp.exp(m_prev - m_new)
            pt = jnp.exp(zt - m_new)
            l_new = alpha * l_prev + jnp.sum(pt, axis=0, keepdims=True)
            acc_ref[rows(e), :] = alpha * acc_ref[rows(e), :] + jnp.dot(
                vt_ref[rows(e), pl.ds(ks, t)], pt.astype(BF16), preferred_element_type=F32)
            out.append((m_new, l_new))
        return tuple(out)

    init = tuple((jnp.full((1, t), -jnp.inf, F32), jnp.zeros((1, t), F32))
                 for _ in range(ATTN_HEADS))

    def body(j, carry):
        stats, qk = carry
        nxt = scores(j + 1)
        return chunk(j, False, stats, qk), nxt

    stats, qk = lax.fori_loop(0, i, body, (init, scores(0)))
    stats = chunk(i, True, stats, qk)
    for e in range(ATTN_HEADS):
        o_ref[:, _head(e)] = (acc_ref[rows(e), :] / stats[e][1]).T.astype(BF16)


def forgetting_attention(u3, c_col, q_gain, k_gain, *, heads):
    b, s, _ = u3.shape
    w = ATTN_HEADS * HEAD_DIM
    q_spec, k_spec, v_spec, o_spec = _attn_specs(s, 0, heads)
    gain = lambda g: g.reshape(1, HEAD_DIM)
    full = lambda shape: pl.BlockSpec(shape, lambda bi, hi, qi: (0, 0))
    return pl.pallas_call(
        _fox_kernel,
        grid=(b, heads // ATTN_HEADS, s // ATTN_T),
        in_specs=[q_spec, k_spec, v_spec,
                  pl.BlockSpec((None, s, LANES), lambda bi, hi, qi: (bi, 0, 0)),
                  full((1, HEAD_DIM)), full((1, HEAD_DIM))],
        out_specs=o_spec,
        out_shape=jax.ShapeDtypeStruct((b, s, heads * HEAD_DIM), BF16),
        scratch_shapes=[pltpu.VMEM((s, ATTN_HEADS * FOX_K), BF16), pltpu.VMEM((w, s), BF16),
                        pltpu.VMEM((s, w), F32), pltpu.VMEM((w, ATTN_T), F32)],
        compiler_params=_cparams("parallel", "parallel", "arbitrary"),
        name="forgetting_attention",
    )(u3, u3, u3, c_col, gain(q_gain), gain(k_gain))


def _forget_cumsum_kernel(f_ref, b_ref, o_ref, carry_ref, *, tc):
    @pl.when(pl.program_id(1) == 0)
    def _():
        carry_ref[...] = jnp.zeros((1, LANES), F32)

    x = f_ref[...] + b_ref[...]
    log_f = jnp.minimum(x, 0.0) - _softplus_neg_abs(x)
    r = lax.broadcasted_iota(jnp.int32, (tc, tc), 0)
    c = lax.broadcasted_iota(jnp.int32, (tc, tc), 1)
    tri = jnp.where(r >= c, 1.0, 0.0).astype(BF16)
    cs = carry_ref[...]
    for part in _split_bf16(log_f, 3):
        cs = cs + jnp.dot(tri, part, preferred_element_type=F32)
    o_ref[...] = cs
    carry_ref[...] = cs[tc - 1:tc, :]


def forget_cumsum(u3, forget_b, *, col0, tc):
    b, s, _ = u3.shape
    j0 = col0 // LANES
    b_pad = jnp.zeros((1, LANES), F32).at[0, :forget_b.shape[0]].set(forget_b)
    return pl.pallas_call(
        functools.partial(_forget_cumsum_kernel, tc=tc),
        grid=(b, s // tc),
        in_specs=[pl.BlockSpec((None, tc, LANES), lambda bi, si: (bi, si, j0)),
                  pl.BlockSpec((1, LANES), lambda bi, si: (0, 0))],
        out_specs=pl.BlockSpec((None, tc, LANES), lambda bi, si: (bi, si, 0)),
        out_shape=jax.ShapeDtypeStruct((b, s, LANES), F32),
        scratch_shapes=[pltpu.VMEM((1, LANES), F32)],
        compiler_params=_cparams("parallel", "arbitrary"),
        name="forget_cumsum",
    )(u3, b_pad)


def _out_proj_kernel(a_ref, b_ref, w1_ref, w2_ref, h_ref, o_ref):
    o_ref[...] = (h_ref[...]
                  + jnp.dot(a_ref[...], w1_ref[...], preferred_element_type=F32)
                  + jnp.dot(b_ref[...], w2_ref[...], preferred_element_type=F32))


def out_proj_residual(a, b, w, h, *, tm):
    n, ka = a.shape
    kb = b.shape[1]
    d = w.shape[1]
    return pl.pallas_call(
        _out_proj_kernel,
        grid=(n // tm,),
        in_specs=[pl.BlockSpec((tm, ka), lambda i: (i, 0)),
                  pl.BlockSpec((tm, kb), lambda i: (i, 0)),
                  pl.BlockSpec((ka, d), lambda i: (0, 0)),
                  pl.BlockSpec((kb, d), lambda i: (0, 0)),
                  pl.BlockSpec((tm, d), lambda i: (i, 0))],
        out_specs=pl.BlockSpec((tm, d), lambda i: (i, 0)),
        out_shape=jax.ShapeDtypeStruct((n, d), F32),
        compiler_params=_cparams("parallel"),
        name="out_proj_residual",
    )(a, b, w[:ka], w[ka:], h)


INFO_E, INFO_GATE, INFO_RANK = 0, 2, 4


def _lane_pick(x, lane, idx):
    return jnp.sum(jnp.where(lane == idx, x, 0.0), axis=1, keepdims=True)


def _router_kernel(h_ref, g_ref, wh_ref, wl_ref, b_ref, xn_ref, info_ref, cnt_ref, carry_ref, *, tm):
    @pl.when(pl.program_id(0) == 0)
    def _():
        carry_ref[...] = jnp.zeros((1, LANES), F32)

    xn = _rms(h_ref[...], g_ref[...])
    xn_ref[...] = xn

    xh, xl = _split_bf16(xn, 2)
    wh = wh_ref[...]
    logits = (jnp.dot(xh, wh, preferred_element_type=F32)
              + jnp.dot(xl, wh, preferred_element_type=F32)
              + jnp.dot(xh, wl_ref[...], preferred_element_type=F32)) + b_ref[...]

    lane = lax.broadcasted_iota(jnp.int32, (tm, LANES), 1).astype(F32)
    neg = -jnp.inf
    big = float(LANES)
    gl = jnp.where(lane < N_GROUPS, logits, neg)
    gmax = jnp.max(gl, axis=1, keepdims=True)
    g_top_p = 1.0 / jnp.sum(jnp.exp(gl - gmax), axis=1, keepdims=True)
    g_idx = jnp.min(jnp.where(gl == gmax, lane, big), axis=1, keepdims=True)

    lo = N_GROUPS + EXPERTS_PER_GROUP * g_idx
    el = jnp.where((lane >= lo) & (lane < lo + EXPERTS_PER_GROUP), logits, neg)
    m1 = jnp.max(el, axis=1, keepdims=True)
    i1 = jnp.min(jnp.where(el == m1, lane, big), axis=1, keepdims=True)
    el2 = jnp.where(lane == i1, neg, el)
    m2 = jnp.max(el2, axis=1, keepdims=True)
    i2 = jnp.min(jnp.where(el2 == m2, lane, big), axis=1, keepdims=True)
    ratio = jnp.exp(m2 - m1)
    p1 = 1.0 / (1.0 + ratio)
    p2 = ratio * p1
    e1 = i1 - N_GROUPS
    e2 = i2 - N_GROUPS

    onehot = jnp.where((lane == e1) | (lane == e2), 1.0, 0.0)
    r = lax.broadcasted_iota(jnp.int32, (tm, tm), 0)
    c = lax.broadcasted_iota(jnp.int32, (tm, tm), 1)
    before = jnp.where(r > c, 1.0, 0.0).astype(BF16)
    cnt = jnp.dot(before, onehot.astype(BF16), preferred_element_type=F32) + carry_ref[...]
    rank1 = _lane_pick(cnt, lane, e1)
    rank2 = _lane_pick(cnt, lane, e2)
    total = carry_ref[...] + jnp.sum(onehot, axis=0, keepdims=True)
    carry_ref[...] = total
    cnt_ref[...] = jnp.broadcast_to(total, (SUBLANES, LANES))

    info = jnp.zeros((tm, LANES), F32)
    for k, val in ((INFO_E, e1), (INFO_E + 1, e2), (INFO_GATE, g_top_p * p1),
                   (INFO_GATE + 1, g_top_p * p2), (INFO_RANK, rank1), (INFO_RANK + 1, rank2)):
        info = jnp.where(lane == k, val, info)
    info_ref[...] = info


def moe_router(h, gain, group_w, group_b, expert_w, expert_b, *, tm):
    n, d = h.shape
    n_logits = N_GROUPS + N_EXPERTS
    w = jnp.zeros((d, LANES), F32).at[:, :N_GROUPS].set(group_w).at[:, N_GROUPS:n_logits].set(expert_w)
    bias = jnp.zeros((1, LANES), F32).at[0, :N_GROUPS].set(group_b).at[0, N_GROUPS:n_logits].set(expert_b)
    w_hi = w.astype(BF16)
    w_lo = (w - w_hi.astype(F32)).astype(BF16)
    full = lambda shape: pl.BlockSpec(shape, lambda i: (0, 0))
    return pl.pallas_call(
        functools.partial(_router_kernel, tm=tm),
        grid=(n // tm,),
        in_specs=[pl.BlockSpec((tm, d), lambda i: (i, 0)), full((1, d)),
                  full((d, LANES)), full((d, LANES)), full((1, LANES))],
        out_specs=[pl.BlockSpec((tm, d), lambda i: (i, 0)),
                   pl.BlockSpec((tm, LANES), lambda i: (i, 0)),
                   full((SUBLANES, LANES))],
        out_shape=[jax.ShapeDtypeStruct((n, d), F32),
                   jax.ShapeDtypeStruct((n, LANES), F32),
                   jax.ShapeDtypeStruct((SUBLANES, LANES), F32)],
        scratch_shapes=[pltpu.VMEM((1, LANES), F32)],
        compiler_params=_cparams("arbitrary"),
        name="moe_router",
    )(h, gain.reshape(1, d), w_hi, w_lo, bias)


def _row(ref, i):
    return ref.at[pl.ds(i, 1), :]


def _row_copy(src, dst, sem):
    return pltpu.make_async_copy(src, dst, sem)


ROW_DMA_UNROLL = 8


def _dispatch_kernel(zstart_ref, nu_ref, dest_ref, xn_ref, xs_ref, zero_ref, sem, *, tm, n_experts):
    tile = pl.program_id(0)
    n_blocks = xs_ref.shape[0] // MOE_BLOCK

    @pl.when(tile == 0)
    def _():
        zero_ref[...] = jnp.zeros(zero_ref.shape, F32)
        block = lambda start: _row_copy(
            zero_ref, xs_ref.at[pl.ds(pl.multiple_of(start, MOE_BLOCK), MOE_BLOCK), :], sem)
        for e in range(n_experts):
            block(zstart_ref[e]).start()
        for e in range(n_experts):
            block(0).wait()

        def tail(i, carry):
            cp = block(i * MOE_BLOCK)
            cp.start()
            cp.wait()
            return carry

        lax.fori_loop(nu_ref[0], n_blocks, tail, 0)

    def issue(t, carry):
        _row_copy(_row(xn_ref, t), _row(xs_ref, dest_ref[0, 2 * t]), sem).start()
        _row_copy(_row(xn_ref, t), _row(xs_ref, dest_ref[0, 2 * t + 1]), sem).start()
        return carry

    lax.fori_loop(0, tm, issue, 0, unroll=ROW_DMA_UNROLL)

    for _ in range(2):
        _row_copy(xn_ref, xs_ref.at[pl.ds(0, tm), :], sem).wait()


def moe_dispatch(xn, dest, zstart, n_used, *, n_slots, tm):
    n, d = xn.shape
    dest3 = dest.reshape(n // tm, 1, 2 * tm)
    return pl.pallas_call(
        functools.partial(_dispatch_kernel, tm=tm, n_experts=zstart.shape[0]),
        grid_spec=pltpu.PrefetchScalarGridSpec(
            num_scalar_prefetch=2,
            grid=(n // tm,),
            in_specs=[pl.BlockSpec((None, 1, 2 * tm), lambda i, z, nu: (i, 0, 0),
                                   memory_space=pltpu.SMEM),
                      pl.BlockSpec((tm, d), lambda i, z, nu: (i, 0))],
            out_specs=pl.BlockSpec(memory_space=pl.ANY),
            scratch_shapes=[pltpu.VMEM((MOE_BLOCK, d), F32),
                            pltpu.SemaphoreType.DMA(())]),
        out_shape=jax.ShapeDtypeStruct((n_slots, d), F32),
        compiler_params=_cparams("arbitrary"),
        name="moe_dispatch",
    )(zstart, n_used, dest3, xn)


EXPERT_FF_CHUNK = 512


def _expert_kernel(plan_ref, nu_ref, x_ref, wg_hbm, wu_hbm, wd_hbm, y_ref,
                   wg_buf, wu_buf, wd_buf, sems, *, layer):
    i = pl.program_id(0)
    slot = plan_ref[1, i]

    def copies(expert, s):
        return [pltpu.make_async_copy(hbm.at[layer, expert], buf.at[s], sems.at[s])
                for hbm, buf in ((wg_hbm, wg_buf), (wu_hbm, wu_buf), (wd_hbm, wd_buf))]

    @pl.when(i == 0)
    def _():
        for cp in copies(plan_ref[0, 0], 0):
            cp.start()

    @pl.when(plan_ref[2, i] == 1)
    def _():
        for cp in copies(plan_ref[0, i], slot):
            cp.wait()

        @pl.when(plan_ref[3, i] >= 0)
        def _():
            for cp in copies(plan_ref[3, i], 1 - slot):
                cp.start()

    @pl.when(i < nu_ref[0])
    def _():
        x = x_ref[...].astype(BF16)
        wg, wu, wd = wg_buf.at[slot], wu_buf.at[slot], wd_buf.at[slot]
        y = None
        for c0 in range(0, wg.shape[1], EXPERT_FF_CHUNK):
            cols = slice(c0, c0 + EXPERT_FF_CHUNK)
            gate = jnp.dot(x, wg[:, cols].astype(BF16), preferred_element_type=F32)
            up = jnp.dot(x, wu[:, cols].astype(BF16), preferred_element_type=F32)
            hidden = (gate * _sigmoid(gate) * up).astype(BF16)
            part = jnp.dot(hidden, wd[cols, :].astype(BF16), preferred_element_type=F32)
            y = part if y is None else y + part
        y_ref[...] = y

    @pl.when(i >= nu_ref[0])
    def _():
        y_ref[...] = jnp.zeros(y_ref.shape, F32)


def moe_experts(xs, plan, n_used, w_gate, w_up, w_down, *, layer):
    n_slots = xs.shape[0]
    n_blocks = n_slots // MOE_BLOCK
    _, _, d, ff = w_gate.shape
    x_map = lambda i, plan, nu: (jnp.minimum(i, nu[0] - 1), 0)
    hbm = pl.BlockSpec(memory_space=pl.ANY)
    return pl.pallas_call(
        functools.partial(_expert_kernel, layer=layer),
        grid_spec=pltpu.PrefetchScalarGridSpec(
            num_scalar_prefetch=2,
            grid=(n_blocks,),
            in_specs=[pl.BlockSpec((MOE_BLOCK, d), x_map), hbm, hbm, hbm],
            out_specs=pl.BlockSpec((MOE_BLOCK, d), lambda i, plan, nu: (i, 0)),
            scratch_shapes=[pltpu.VMEM((2, d, ff), F32), pltpu.VMEM((2, d, ff), F32),
                            pltpu.VMEM((2, ff, d), F32), pltpu.SemaphoreType.DMA((2,))]),
        out_shape=jax.ShapeDtypeStruct((n_slots, d), F32),
        compiler_params=_cparams("arbitrary", vmem_limit=VMEM_LIMIT_EXPERTS),
        name="moe_experts",
    )(plan, n_used, xs, w_gate, w_up, w_down)


def _combine_kernel(d0_ref, d1_ref, d2_ref, h_ref, info_ref, y_ref, o_ref, buf1, buf2, sems, *, tm):
    p = pl.program_id(0)

    def request(d_ref, s):
        def issue(t, carry):
            _row_copy(_row(y_ref, d_ref[0, 2 * t]), _row(buf1.at[s], t), sems.at[s]).start()
            _row_copy(_row(y_ref, d_ref[0, 2 * t + 1]), _row(buf2.at[s], t), sems.at[s]).start()
            return carry

        lax.fori_loop(0, tm, issue, 0, unroll=ROW_DMA_UNROLL)

    def combine(s):
        for buf in (buf1, buf2):
            _row_copy(y_ref.at[pl.ds(0, tm), :], buf.at[s], sems.at[s]).wait()
        tile = slice(s * tm, (s + 1) * tm)
        info = info_ref[tile, :]
        lane = lax.broadcasted_iota(jnp.int32, (tm, LANES), 1)
        g1 = jnp.sum(jnp.where(lane == INFO_GATE, info, 0.0), axis=1, keepdims=True)
        g2 = jnp.sum(jnp.where(lane == INFO_GATE + 1, info, 0.0), axis=1, keepdims=True)
        o_ref[tile, :] = h_ref[tile, :] + g1 * buf1[s] + g2 * buf2[s]

    @pl.when(p == 0)
    def _():
        request(d0_ref, 0)

    request(d1_ref, 1)
    combine(0)

    @pl.when(p + 1 < pl.num_programs(0))
    def _():
        request(d2_ref, 0)

    combine(1)


def moe_combine(h, info, y, dest, *, tm):
    n, d = h.shape
    tiles = n // tm
    dest3 = dest.reshape(tiles, 1, 2 * tm)
    dest_spec = lambda index_map: pl.BlockSpec((None, 1, 2 * tm), index_map, memory_space=pltpu.SMEM)
    buf = pltpu.VMEM((2, tm, d), F32)
    return pl.pallas_call(
        functools.partial(_combine_kernel, tm=tm),
        grid=(tiles // 2,),
        in_specs=[dest_spec(lambda p: (2 * p, 0, 0)),
                  dest_spec(lambda p: (2 * p + 1, 0, 0)),
                  dest_spec(lambda p: (jnp.minimum(2 * p + 2, tiles - 1), 0, 0)),
                  pl.BlockSpec((2 * tm, d), lambda p: (p, 0)),
                  pl.BlockSpec((2 * tm, LANES), lambda p: (p, 0)),
                  pl.BlockSpec(memory_space=pl.ANY)],
        out_specs=pl.BlockSpec((2 * tm, d), lambda p: (p, 0)),
        out_shape=jax.ShapeDtypeStruct((n, d), F32),
        scratch_shapes=[buf, buf, pltpu.SemaphoreType.DMA((2,))],
        compiler_params=_cparams("arbitrary"),
        name="moe_combine",
    )(dest3, dest3, dest3, h, info, y)


def hierarchical_moe(h, gain, group_w, group_b, expert_w, expert_b, w_gate, w_up, w_down,
                     *, layer, router_tm, dispatch_tm, combine_tm):
    n, d = h.shape
    xn, info, cnt = moe_router(h, gain, group_w, group_b, expert_w, expert_b, tm=router_tm)

    n_experts = w_gate.shape[1]
    n_blocks = -(-2 * n // MOE_BLOCK) + n_experts
    n_slots = n_blocks * MOE_BLOCK
    experts = info[:, INFO_E:INFO_E + 2].astype(jnp.int32)
    ranks = info[:, INFO_RANK:INFO_RANK + 2].astype(jnp.int32)
    counts = cnt[0, :n_experts].astype(jnp.int32)
    padded = (counts + MOE_BLOCK - 1) // MOE_BLOCK * MOE_BLOCK
    pends = jnp.cumsum(padded)
    pstarts = pends - padded
    dest = pstarts[experts] + ranks
    n_used = pends[-1] // MOE_BLOCK
    blk = jnp.arange(n_blocks, dtype=jnp.int32)
    block_e = jnp.sum((pends[None, :] <= (blk * MOE_BLOCK)[:, None]).astype(jnp.int32), axis=1)
    block_e = jnp.minimum(block_e, n_experts - 1)
    block_e = jnp.where(blk < n_used, block_e, block_e[n_used - 1]).astype(jnp.int32)
    eid = jnp.arange(n_experts, dtype=jnp.int32)
    nonempty = padded > 0
    slot_e = (jnp.cumsum(nonempty.astype(jnp.int32)) - 1) % 2
    later = jnp.where(nonempty[None, :] & (eid[None, :] > eid[:, None]), eid[None, :], n_experts)
    next_e = jnp.min(later, axis=1)
    next_e = jnp.where(next_e < n_experts, next_e, -1)
    first = (blk < n_used) & ((blk == 0) | (block_e != jnp.roll(block_e, 1)))
    plan = jnp.stack([block_e, slot_e[block_e], first.astype(jnp.int32),
                      next_e[block_e]]).astype(jnp.int32)
    zstart = jnp.maximum(pends - MOE_BLOCK, 0).astype(jnp.int32)

    n_used = n_used.reshape(1).astype(jnp.int32)
    xs = moe_dispatch(xn, dest, zstart, n_used, n_slots=n_slots, tm=dispatch_tm)
    y = moe_experts(xs, plan, n_used, w_gate, w_up, w_down, layer=layer)
    return moe_combine(h, info, y, dest, tm=combine_tm)


def kernel(x, even_norm, even_w_in, conv_w, conv_b, conv_norm_g, conv_norm_b, sb_q_norm, sb_k_norm,
           even_w_out, odd_norm, odd_w_in, fox_forget_b, fox_q_norm, fox_k_norm, sc_w, odd_w_out,
           moe_norm, router_group_w, router_group_b, router_expert_w, router_expert_b,
           expert_w_gate, expert_w_up, expert_w_down):
    b, s, d = x.shape
    n = b * s
    h = x.reshape(n, d)
    sb_width = SB_HEADS * HEAD_DIM
    fox_width = FOX_HEADS * HEAD_DIM
    moe_tiles = dict(router_tm=512, dispatch_tm=512, combine_tm=128)

    def moe(h, layer):
        return hierarchical_moe(
            h, moe_norm[layer], router_group_w[layer], router_group_b[layer],
            router_expert_w[layer], router_expert_b[layer],
            expert_w_gate, expert_w_up, expert_w_down, layer=layer, **moe_tiles)

    u = norm_matmul(h, even_norm[0], even_w_in[0].astype(BF16), tm=1024, tn=1024)
    u3 = u.reshape(b, s, -1)
    a = conformer_conv(u3, conv_w[0], conv_b[0], conv_norm_g[0], conv_norm_b[0], ts=256)
    o = stick_breaking_attention(u3, sb_q_norm[0], sb_k_norm[0], col0=2 * CONV_CH, heads=SB_HEADS)
    h = out_proj_residual(a.reshape(n, CONV_CH), o.reshape(n, sb_width),
                          even_w_out[0].astype(BF16), h, tm=512)
    h = moe(h, 0)

    n_qkv = 3 * fox_width
    w_in = odd_w_in[0]
    w_in = jnp.concatenate(
        [w_in[:, :n_qkv], w_in[:, n_qkv + FOX_HEADS:], w_in[:, n_qkv:n_qkv + FOX_HEADS],
         jnp.zeros((d, LANES - FOX_HEADS), F32)], axis=1).astype(BF16)
    u = norm_matmul(h, odd_norm[0], w_in, tm=1024, tn=896)
    u3 = u.reshape(b, s, -1)
    c_col = forget_cumsum(u3, fox_forget_b[0], col0=n_qkv + 3 * SC_CH, tc=512)
    o = forgetting_attention(u3, c_col, fox_q_norm[0], fox_k_norm[0], heads=FOX_HEADS)
    y = short_conv(u3, sc_w[0], col0=n_qkv, ts=256)
    h = out_proj_residual(o.reshape(n, fox_width), y.reshape(n, SC_CH),
                          odd_w_out[0].astype(BF16), h, tm=512)
    h = moe(h, 1)
    return h.reshape(b, s, d)
```

```python
import functools

import jax
import jax.numpy as jnp
from jax import lax
from jax.experimental import pallas as pl
from jax.experimental.pallas import tpu as pltpu

F32 = jnp.float32
BF16 = jnp.bfloat16

HEAD_DIM = 128
CONV_CH = 1024
CONV_WIDTH = 31
SB_HEADS = 8
FOX_HEADS = 8
SC_CH = 1024
SC_WIDTH = 3
N_GROUPS = 4
EXPERTS_PER_GROUP = 8
N_EXPERTS = N_GROUPS * EXPERTS_PER_GROUP
MOE_BLOCK = 256
RMS_EPS = 1e-6
LN_EPS = 1e-5

LANES = 128
SUBLANES = 8
MXU_WIDTH = 256
VMEM_CAPACITY = 64 * 1024 * 1024
VMEM_LIMIT = VMEM_CAPACITY * 7 // 8
VMEM_LIMIT_EXPERTS = VMEM_CAPACITY - 2 * 1024 * 1024


def _cparams(*sem, vmem_limit=VMEM_LIMIT):
    return pltpu.CompilerParams(dimension_semantics=sem, vmem_limit_bytes=vmem_limit)


def _sigmoid(x):
    return 1.0 / (1.0 + jnp.exp(-x))


def _softplus_neg_abs(z):
    return jnp.log(1.0 + jnp.exp(-jnp.abs(z)))


def _rms(x, g):
    ms = jnp.mean(x * x, axis=-1, keepdims=True)
    return x * lax.rsqrt(ms + RMS_EPS) * g


def _split_bf16(x, parts):
    out = []
    r = x
    for _ in range(parts - 1):
        p = r.astype(BF16)
        out.append(p)
        r = r - p.astype(F32)
    out.append(r.astype(BF16))
    return out


def _norm_matmul_kernel(x_ref, g_ref, w_ref, o_ref, xn_ref):
    @pl.when(pl.program_id(1) == 0)
    def _():
        xn_ref[...] = _rms(x_ref[...], g_ref[...]).astype(BF16)

    o_ref[...] = jnp.dot(xn_ref[...], w_ref[...], preferred_element_type=F32)


def norm_matmul(x, gain, w, *, tm, tn):
    n, d = x.shape
    f = w.shape[1]
    return pl.pallas_call(
        _norm_matmul_kernel,
        grid=(n // tm, f // tn),
        in_specs=[pl.BlockSpec((tm, d), lambda i, j: (i, 0)),
                  pl.BlockSpec((1, d), lambda i, j: (0, 0)),
                  pl.BlockSpec((d, tn), lambda i, j: (0, j))],
        out_specs=pl.BlockSpec((tm, tn), lambda i, j: (i, j)),
        out_shape=jax.ShapeDtypeStruct((n, f), F32),
        scratch_shapes=[pltpu.VMEM((tm, d), BF16)],
        compiler_params=_cparams("parallel", "arbitrary"),
        name="norm_matmul",
    )(x, gain.reshape(1, d), w)


def _conv_tile(abuf, w_ref, bias_ref, cbuf, *, ts, halo, width, rows):
    n_ch = cbuf.shape[1]
    off = halo - (width - 1)

    def chan_body(c, carry):
        lanes = pl.ds(pl.multiple_of(c * LANES, LANES), LANES)
        wc = w_ref[:, lanes]
        for r0 in range(0, ts, rows):
            if bias_ref is None:
                acc = jnp.zeros((rows, LANES), F32)
            else:
                acc = jnp.broadcast_to(bias_ref[:, lanes], (rows, LANES))
            win = abuf[pl.ds(r0, rows + halo), lanes]
            for shift in range(SUBLANES):
                taps = [k for k in range(width) if (off + k) % SUBLANES == shift]
                if not taps:
                    continue
                shifted = win if shift == 0 else pltpu.roll(win, rows + halo - shift, axis=0)
                for k in taps:
                    base = off + k - shift
                    acc = acc + wc[k:k + 1, :] * shifted[base:base + rows]
            cbuf[pl.ds(r0, rows), lanes] = acc
        return carry

    lax.fori_loop(0, n_ch // LANES, chan_body, 0)


def _carry_halo(abuf, *, ts, halo):
    s = pl.program_id(1)

    @pl.when(s == 0)
    def _():
        abuf[0:halo, :] = jnp.zeros((halo, abuf.shape[1]), F32)

    @pl.when(s > 0)
    def _():
        abuf[0:halo, :] = abuf[ts:ts + halo, :]


CONV_HALO = 32
SC_HALO = 8


def _conformer_kernel(av_ref, ag_ref, w_ref, cb_ref, lg_ref, lb_ref, o_ref, abuf, cbuf, *, ts):
    _carry_halo(abuf, ts=ts, halo=CONV_HALO)
    abuf[CONV_HALO:CONV_HALO + ts, :] = av_ref[...] * _sigmoid(ag_ref[...])
    _conv_tile(abuf, w_ref, cb_ref, cbuf, ts=ts, halo=CONV_HALO, width=CONV_WIDTH, rows=64)
    y = cbuf[...]
    mu = jnp.mean(y, axis=-1, keepdims=True)
    yc = y - mu
    var = jnp.mean(yc * yc, axis=-1, keepdims=True)
    yn = yc * lax.rsqrt(var + LN_EPS) * lg_ref[...] + lb_ref[...]
    o_ref[...] = (yn * _sigmoid(yn)).astype(BF16)


def conformer_conv(u3, conv_w, conv_b, ln_g, ln_b, *, ts):
    b, s, _ = u3.shape
    c = conv_w.shape[1]
    w_pad = jnp.zeros((CONV_HALO, c), F32).at[:CONV_WIDTH].set(conv_w)
    row = lambda a: a.reshape(1, c)
    full = lambda shape: pl.BlockSpec(shape, lambda bi, si: (0, 0))
    return pl.pallas_call(
        functools.partial(_conformer_kernel, ts=ts),
        grid=(b, s // ts),
        in_specs=[pl.BlockSpec((None, ts, c), lambda bi, si: (bi, si, 0)),
                  pl.BlockSpec((None, ts, c), lambda bi, si: (bi, si, 1)),
                  full((CONV_HALO, c)), full((1, c)), full((1, c)), full((1, c))],
        out_specs=pl.BlockSpec((None, ts, c), lambda bi, si: (bi, si, 0)),
        out_shape=jax.ShapeDtypeStruct((b, s, c), BF16),
        scratch_shapes=[pltpu.VMEM((CONV_HALO + ts, c), F32), pltpu.VMEM((ts, c), F32)],
        compiler_params=_cparams("parallel", "arbitrary"),
        name="conformer_conv",
    )(u3, u3, w_pad, row(conv_b), row(ln_g), row(ln_b))


def _short_conv_kernel(bg_ref, cg_ref, xv_ref, w_ref, o_ref, abuf, cbuf, *, ts):
    _carry_halo(abuf, ts=ts, halo=SC_HALO)
    abuf[SC_HALO:SC_HALO + ts, :] = cg_ref[...] * xv_ref[...]
    _conv_tile(abuf, w_ref, None, cbuf, ts=ts, halo=SC_HALO, width=SC_WIDTH, rows=64)
    o_ref[...] = (bg_ref[...] * cbuf[...]).astype(BF16)


def short_conv(u3, sc_w, *, col0, ts):
    b, s, _ = u3.shape
    c = sc_w.shape[1]
    j0 = col0 // c
    w_pad = jnp.zeros((SUBLANES, c), F32).at[:SC_WIDTH].set(sc_w)
    return pl.pallas_call(
        functools.partial(_short_conv_kernel, ts=ts),
        grid=(b, s // ts),
        in_specs=[pl.BlockSpec((None, ts, c), lambda bi, si: (bi, si, j0)),
                  pl.BlockSpec((None, ts, c), lambda bi, si: (bi, si, j0 + 1)),
                  pl.BlockSpec((None, ts, c), lambda bi, si: (bi, si, j0 + 2)),
                  pl.BlockSpec((SUBLANES, c), lambda bi, si: (0, 0))],
        out_specs=pl.BlockSpec((None, ts, c), lambda bi, si: (bi, si, 0)),
        out_shape=jax.ShapeDtypeStruct((b, s, c), BF16),
        scratch_shapes=[pltpu.VMEM((SC_HALO + ts, c), F32), pltpu.VMEM((ts, c), F32)],
        compiler_params=_cparams("parallel", "arbitrary"),
        name="short_conv",
    )(u3, u3, u3, w_pad)


ATTN_HEADS = 4
ATTN_T = 256


def _head(e):
    return slice(e * HEAD_DIM, (e + 1) * HEAD_DIM)


def _prep_kv(k_ref, v_ref, kg_ref, kn_ref, vb_ref):
    @pl.when(pl.program_id(2) == 0)
    def _():
        for e in range(ATTN_HEADS):
            kn_ref[:, _head(e)] = _rms(k_ref[:, _head(e)], kg_ref[...]).astype(BF16)
        vb_ref[...] = v_ref[...].astype(BF16)


def _prep_q(q_ref, qg_ref, e):
    return (_rms(q_ref[:, _head(e)], qg_ref[...]) * (HEAD_DIM ** -0.5)).astype(BF16)


def _qk(qn, kb):
    return lax.dot_general(qn, kb, (((1,), (1,)), ((), ())), preferred_element_type=F32)


SB_SUB = 128
SB_CUTOFF = -110.0


def _sb_kernel(q_ref, k_ref, v_ref, qg_ref, kg_ref, o_ref, kn_ref, vb_ref, acc_ref, r_ref):
    t = ATTN_T
    i = pl.program_id(2)
    _prep_kv(k_ref, v_ref, kg_ref, kn_ref, vb_ref)
    qn = [_prep_q(q_ref, qg_ref, e) for e in range(ATTN_HEADS)]
    acc_ref[...] = jnp.zeros(acc_ref.shape, F32)
    r_ref[...] = jnp.zeros(r_ref.shape, F32)

    jr = lax.broadcasted_iota(jnp.int32, (t, t), 0)
    sc = lax.broadcasted_iota(jnp.int32, (t, t), 1)
    later = jnp.where((jr > sc) & (jr // SB_SUB == sc // SB_SUB), 1.0, 0.0).astype(BF16)
    strict = sc < jr

    def local_terms(c, masked):
        ks = pl.multiple_of(c * t, t)
        out = []
        for e in range(ATTN_HEADS):
            z = _qk(qn[e], kn_ref[pl.ds(ks, t), _head(e)])
            log_beta = jnp.minimum(z, 0.0) - _softplus_neg_abs(z)
            log_om = log_beta - z
            if masked:
                log_om = jnp.where(strict, log_om, 0.0)
            hi, lo = _split_bf16(log_om, 2)
            suffix = (jnp.dot(hi, later, preferred_element_type=F32)
                      + jnp.dot(lo, later, preferred_element_type=F32))
            out.append((log_beta + suffix,
                        jnp.sum(log_om[:, SB_SUB:], axis=1, keepdims=True),
                        jnp.sum(log_om[:, :SB_SUB], axis=1, keepdims=True)))
        return tuple(out)

    def accumulate(c, masked, terms):
        ks = pl.multiple_of(c * t, t)
        for e in range(ATTN_HEADS):
            base, tot_near, tot_far = terms[e]
            r = r_ref[:, _head(e)]
            w = jnp.exp(base + jnp.concatenate([r + tot_near, r], axis=1))
            if masked:
                w = jnp.where(strict, w, 0.0)
            acc_ref[:, _head(e)] += jnp.dot(w.astype(BF16), vb_ref[pl.ds(ks, t), _head(e)],
                                            preferred_element_type=F32)
            r_ref[:, _head(e)] = r + (tot_near + tot_far)

    accumulate(i, True, local_terms(i, True))

    def more(carry):
        n, r_max, _ = carry
        return (n < i) & (r_max > SB_CUTOFF)

    def body(carry):
        n, _, terms = carry
        c = i - 1 - n
        ahead = local_terms(jnp.maximum(c - 1, 0), False)
        accumulate(c, False, terms)
        return n + 1, jnp.max(r_ref[...]), ahead

    first = local_terms(jnp.maximum(i - 1, 0), False)
    lax.while_loop(more, body, (jnp.int32(0), jnp.max(r_ref[...]), first))
    o_ref[...] = acc_ref[...].astype(BF16)


def _attn_specs(s, j0, heads):
    w = ATTN_HEADS * HEAD_DIM
    g0 = j0 // ATTN_HEADS
    gh = heads // ATTN_HEADS
    q_spec = pl.BlockSpec((None, ATTN_T, w), lambda bi, hi, qi: (bi, qi, g0 + hi))
    k_spec = pl.BlockSpec((None, s, w), lambda bi, hi, qi: (bi, 0, g0 + gh + hi),
                          pipeline_mode=pl.Buffered(1))
    v_spec = pl.BlockSpec((None, s, w), lambda bi, hi, qi: (bi, 0, g0 + 2 * gh + hi),
                          pipeline_mode=pl.Buffered(1))
    o_spec = pl.BlockSpec((None, ATTN_T, w), lambda bi, hi, qi: (bi, qi, hi))
    return q_spec, k_spec, v_spec, o_spec


def stick_breaking_attention(u3, q_gain, k_gain, *, col0, heads):
    b, s, _ = u3.shape
    w = ATTN_HEADS * HEAD_DIM
    q_spec, k_spec, v_spec, o_spec = _attn_specs(s, col0 // HEAD_DIM, heads)
    gain = lambda g: g.reshape(1, HEAD_DIM)
    full = lambda shape: pl.BlockSpec(shape, lambda bi, hi, qi: (0, 0))
    return pl.pallas_call(
        _sb_kernel,
        grid=(b, heads // ATTN_HEADS, s // ATTN_T),
        in_specs=[q_spec, k_spec, v_spec, full((1, HEAD_DIM)), full((1, HEAD_DIM))],
        out_specs=o_spec,
        out_shape=jax.ShapeDtypeStruct((b, s, heads * HEAD_DIM), BF16),
        scratch_shapes=[pltpu.VMEM((s, w), BF16), pltpu.VMEM((s, w), BF16),
                        pltpu.VMEM((ATTN_T, w), F32), pltpu.VMEM((ATTN_T, w), F32)],
        compiler_params=_cparams("parallel", "parallel", "arbitrary"),
        name="stick_breaking_attention",
    )(u3, u3, u3, gain(q_gain), gain(k_gain))


FOX_PREP = 512
FOX_K = 2 * HEAD_DIM


def _bias_lanes(c_rep, lane, first, sign):
    other = 3 - first
    tile = jnp.where((lane >= other) & (lane < other + 3), 1.0, 0.0)
    for k, part in enumerate(_split_bf16(sign * c_rep, 3)):
        tile = jnp.where(lane == first + k, part.astype(F32), tile)
    return tile.astype(BF16)


def _fox_kernel(q_ref, k_ref, v_ref, cc_ref, qg_ref, kg_ref, o_ref, kn_ref, vt_ref, cs_ref, acc_ref):
    t = ATTN_T
    hg = pl.program_id(1)
    i = pl.program_id(2)
    s = k_ref.shape[0]
    rows = lambda e: slice(e * HEAD_DIM, (e + 1) * HEAD_DIM)
    kcols = lambda e: slice(e * FOX_K, (e + 1) * FOX_K)

    @pl.when(i == 0)
    def _():
        lane = lax.broadcasted_iota(jnp.int32, (FOX_PREP, LANES), 1)
        for e in range(ATTN_HEADS):
            for c0 in range(0, s, FOX_PREP):
                blk = slice(c0, c0 + FOX_PREP)
                vt_ref[rows(e), blk] = v_ref[blk, _head(e)].T.astype(BF16)
                col = jnp.sum(jnp.where(lane == hg * ATTN_HEADS + e, cc_ref[blk, :], 0.0),
                              axis=1, keepdims=True)
                c_rep = jnp.broadcast_to(col, (FOX_PREP, LANES))
                cs_ref[blk, _head(e)] = c_rep
                kn = _rms(k_ref[blk, _head(e)], kg_ref[...]).astype(BF16)
                kn_ref[blk, kcols(e)] = jnp.concatenate(
                    [kn, _bias_lanes(c_rep, lane, 0, -1.0)], axis=1)

    qlane = lax.broadcasted_iota(jnp.int32, (t, LANES), 1)
    qrows = pl.ds(pl.multiple_of(i * t, t), t)
    qn = [jnp.concatenate([_prep_q(q_ref, qg_ref, e),
                           _bias_lanes(cs_ref[qrows, _head(e)], qlane, 3, 1.0)], axis=1)
          for e in range(ATTN_HEADS)]
    acc_ref[...] = jnp.zeros(acc_ref.shape, F32)
    kpos = lax.broadcasted_iota(jnp.int32, (t, t), 0)
    qpos = lax.broadcasted_iota(jnp.int32, (t, t), 1)

    def scores(j):
        ks = pl.multiple_of(j * t, t)
        return tuple(_qk(kn_ref[pl.ds(ks, t), kcols(e)], qn[e])
                     for e in range(ATTN_HEADS))

    def chunk(j, masked, stats, qk):
        ks = pl.multiple_of(j * t, t)
        out = []
        for e in range(ATTN_HEADS):
            m_prev, l_prev = stats[e]
            zt = qk[e]
            if masked:
                zt = jnp.where(kpos <= qpos, zt, -jnp.inf)
            m_new = jnp.maximum(m_prev, jnp.max(zt, axis=0, keepdims=True))
            alpha = jnp.exp(m_prev - m_new)
            pt = jnp.exp(zt - m_new)
            l_new = alpha * l_prev + jnp.sum(pt, axis=0, keepdims=True)
            acc_ref[rows(e), :] = alpha * acc_ref[rows(e), :] + jnp.dot(
                vt_ref[rows(e), pl.ds(ks, t)], pt.astype(BF16), preferred_element_type=F32)
            out.append((m_new, l_new))
        return tuple(out)

    init = tuple((jnp.full((1, t), -jnp.inf, F32), jnp.zeros((1, t), F32))
                 for _ in range(ATTN_HEADS))

    def body(j, carry):
        stats, qk = carry
        nxt = scores(j + 1)
        return chunk(j, False, stats, qk), nxt

    stats, qk = lax.fori_loop(0, i, body, (init, scores(0)))
    stats = chunk(i, True, stats, qk)
    for e in range(ATTN_HEADS):
        o_ref[:, _head(e)] = (acc_ref[rows(e), :] / stats[e][1]).T.astype(BF16)


def forgetting_attention(u3, c_col, q_gain, k_gain, *, heads):
    b, s, _ = u3.shape
    w = ATTN_HEADS * HEAD_DIM
    q_spec, k_spec, v_spec, o_spec = _attn_specs(s, 0, heads)
    gain = lambda g: g.reshape(1, HEAD_DIM)
    full = lambda shape: pl.BlockSpec(shape, lambda bi, hi, qi: (0, 0))
    return pl.pallas_call(
        _fox_kernel,
        grid=(b, heads // ATTN_HEADS, s // ATTN_T),
        in_specs=[q_spec, k_spec, v_spec,
                  pl.BlockSpec((None, s, LANES), lambda bi, hi, qi: (bi, 0, 0)),
                  full((1, HEAD_DIM)), full((1, HEAD_DIM))],
        out_specs=o_spec,
        out_shape=jax.ShapeDtypeStruct((b, s, heads * HEAD_DIM), BF16),
        scratch_shapes=[pltpu.VMEM((s, ATTN_HEADS * FOX_K), BF16), pltpu.VMEM((w, s), BF16),
                        pltpu.VMEM((s, w), F32), pltpu.VMEM((w, ATTN_T), F32)],
        compiler_params=_cparams("parallel", "parallel", "arbitrary"),
        name="forgetting_attention",
    )(u3, u3, u3, c_col, gain(q_gain), gain(k_gain))


def _forget_cumsum_kernel(f_ref, b_ref, o_ref, carry_ref, *, tc):
    @pl.when(pl.program_id(1) == 0)
    def _():
        carry_ref[...] = jnp.zeros((1, LANES), F32)

    x = f_ref[...] + b_ref[...]
    log_f = jnp.minimum(x, 0.0) - _softplus_neg_abs(x)
    r = lax.broadcasted_iota(jnp.int32, (tc, tc), 0)
    c = lax.broadcasted_iota(jnp.int32, (tc, tc), 1)
    tri = jnp.where(r >= c, 1.0, 0.0).astype(BF16)
    cs = carry_ref[...]
    for part in _split_bf16(log_f, 3):
        cs = cs + jnp.dot(tri, part, preferred_element_type=F32)
    o_ref[...] = cs
    carry_ref[...] = cs[tc - 1:tc, :]


def forget_cumsum(u3, forget_b, *, col0, tc):
    b, s, _ = u3.shape
    j0 = col0 // LANES
    b_pad = jnp.zeros((1, LANES), F32).at[0, :forget_b.shape[0]].set(forget_b)
    return pl.pallas_call(
        functools.partial(_forget_cumsum_kernel, tc=tc),
        grid=(b, s // tc),
        in_specs=[pl.BlockSpec((None, tc, LANES), lambda bi, si: (bi, si, j0)),
                  pl.BlockSpec((1, LANES), lambda bi, si: (0, 0))],
        out_specs=pl.BlockSpec((None, tc, LANES), lambda bi, si: (bi, si, 0)),
        out_shape=jax.ShapeDtypeStruct((b, s, LANES), F32),
        scratch_shapes=[pltpu.VMEM((1, LANES), F32)],
        compiler_params=_cparams("parallel", "arbitrary"),
        name="forget_cumsum",
    )(u3, b_pad)


def _out_proj_kernel(a_ref, b_ref, w1_ref, w2_ref, h_ref, o_ref):
    o_ref[...] = (h_ref[...]
                  + jnp.dot(a_ref[...], w1_ref[...], preferred_element_type=F32)
                  + jnp.dot(b_ref[...], w2_ref[...], preferred_element_type=F32))


def out_proj_residual(a, b, w, h, *, tm):
    n, ka = a.shape
    kb = b.shape[1]
    d = w.shape[1]
    return pl.pallas_call(
        _out_proj_kernel,
        grid=(n // tm,),
        in_specs=[pl.BlockSpec((tm, ka), lambda i: (i, 0)),
                  pl.BlockSpec((tm, kb), lambda i: (i, 0)),
                  pl.BlockSpec((ka, d), lambda i: (0, 0)),
                  pl.BlockSpec((kb, d), lambda i: (0, 0)),
                  pl.BlockSpec((tm, d), lambda i: (i, 0))],
        out_specs=pl.BlockSpec((tm, d), lambda i: (i, 0)),
        out_shape=jax.ShapeDtypeStruct((n, d), F32),
        compiler_params=_cparams("parallel"),
        name="out_proj_residual",
    )(a, b, w[:ka], w[ka:], h)


INFO_E, INFO_GATE, INFO_RANK = 0, 2, 4


def _lane_pick(x, lane, idx):
    return jnp.sum(jnp.where(lane == idx, x, 0.0), axis=1, keepdims=True)


def _router_kernel(h_ref, g_ref, wh_ref, wl_ref, b_ref, xn_ref, info_ref, cnt_ref, carry_ref, *, tm):
    @pl.when(pl.program_id(0) == 0)
    def _():
        carry_ref[...] = jnp.zeros((1, LANES), F32)

    xn = _rms(h_ref[...], g_ref[...])
    xn_ref[...] = xn

    xh, xl = _split_bf16(xn, 2)
    wh = wh_ref[...]
    logits = (jnp.dot(xh, wh, preferred_element_type=F32)
              + jnp.dot(xl, wh, preferred_element_type=F32)
              + jnp.dot(xh, wl_ref[...], preferred_element_type=F32)) + b_ref[...]

    lane = lax.broadcasted_iota(jnp.int32, (tm, LANES), 1).astype(F32)
    neg = -jnp.inf
    big = float(LANES)
    gl = jnp.where(lane < N_GROUPS, logits, neg)
    gmax = jnp.max(gl, axis=1, keepdims=True)
    g_top_p = 1.0 / jnp.sum(jnp.exp(gl - gmax), axis=1, keepdims=True)
    g_idx = jnp.min(jnp.where(gl == gmax, lane, big), axis=1, keepdims=True)

    lo = N_GROUPS + EXPERTS_PER_GROUP * g_idx
    el = jnp.where((lane >= lo) & (lane < lo + EXPERTS_PER_GROUP), logits, neg)
    m1 = jnp.max(el, axis=1, keepdims=True)
    i1 = jnp.min(jnp.where(el == m1, lane, big), axis=1, keepdims=True)
    el2 = jnp.where(lane == i1, neg, el)
    m2 = jnp.max(el2, axis=1, keepdims=True)
    i2 = jnp.min(jnp.where(el2 == m2, lane, big), axis=1, keepdims=True)
    ratio = jnp.exp(m2 - m1)
    p1 = 1.0 / (1.0 + ratio)
    p2 = ratio * p1
    e1 = i1 - N_GROUPS
    e2 = i2 - N_GROUPS

    onehot = jnp.where((lane == e1) | (lane == e2), 1.0, 0.0)
    r = lax.broadcasted_iota(jnp.int32, (tm, tm), 0)
    c = lax.broadcasted_iota(jnp.int32, (tm, tm), 1)
    before = jnp.where(r > c, 1.0, 0.0).astype(BF16)
    cnt = jnp.dot(before, onehot.astype(BF16), preferred_element_type=F32) + carry_ref[...]
    rank1 = _lane_pick(cnt, lane, e1)
    rank2 = _lane_pick(cnt, lane, e2)
    total = carry_ref[...] + jnp.sum(onehot, axis=0, keepdims=True)
    carry_ref[...] = total
    cnt_ref[...] = jnp.broadcast_to(total, (SUBLANES, LANES))

    info = jnp.zeros((tm, LANES), F32)
    for k, val in ((INFO_E, e1), (INFO_E + 1, e2), (INFO_GATE, g_top_p * p1),
                   (INFO_GATE + 1, g_top_p * p2), (INFO_RANK, rank1), (INFO_RANK + 1, rank2)):
        info = jnp.where(lane == k, val, info)
    info_ref[...] = info


def moe_router(h, gain, group_w, group_b, expert_w, expert_b, *, tm):
    n, d = h.shape
    n_logits = N_GROUPS + N_EXPERTS
    w = jnp.zeros((d, LANES), F32).at[:, :N_GROUPS].set(group_w).at[:, N_GROUPS:n_logits].set(expert_w)
    bias = jnp.zeros((1, LANES), F32).at[0, :N_GROUPS].set(group_b).at[0, N_GROUPS:n_logits].set(expert_b)
    w_hi = w.astype(BF16)
    w_lo = (w - w_hi.astype(F32)).astype(BF16)
    full = lambda shape: pl.BlockSpec(shape, lambda i: (0, 0))
    return pl.pallas_call(
        functools.partial(_router_kernel, tm=tm),
        grid=(n // tm,),
        in_specs=[pl.BlockSpec((tm, d), lambda i: (i, 0)), full((1, d)),
                  full((d, LANES)), full((d, LANES)), full((1, LANES))],
        out_specs=[pl.BlockSpec((tm, d), lambda i: (i, 0)),
                   pl.BlockSpec((tm, LANES), lambda i: (i, 0)),
                   full((SUBLANES, LANES))],
        out_shape=[jax.ShapeDtypeStruct((n, d), F32),
                   jax.ShapeDtypeStruct((n, LANES), F32),
                   jax.ShapeDtypeStruct((SUBLANES, LANES), F32)],
        scratch_shapes=[pltpu.VMEM((1, LANES), F32)],
        compiler_params=_cparams("arbitrary"),
        name="moe_router",
    )(h, gain.reshape(1, d), w_hi, w_lo, bias)


def _row(ref, i):
    return ref.at[pl.ds(i, 1), :]


def _row_copy(src, dst, sem):
    return pltpu.make_async_copy(src, dst, sem)


ROW_DMA_UNROLL = 8


def _dispatch_kernel(zstart_ref, nu_ref, dest_ref, xn_ref, xs_ref, zero_ref, sem, *, tm, n_experts):
    tile = pl.program_id(0)
    n_blocks = xs_ref.shape[0] // MOE_BLOCK

    @pl.when(tile == 0)
    def _():
        zero_ref[...] = jnp.zeros(zero_ref.shape, F32)
        block = lambda start: _row_copy(
            zero_ref, xs_ref.at[pl.ds(pl.multiple_of(start, MOE_BLOCK), MOE_BLOCK), :], sem)
        for e in range(n_experts):
            block(zstart_ref[e]).start()
        for e in range(n_experts):
            block(0).wait()

        def tail(i, carry):
            cp = block(i * MOE_BLOCK)
            cp.start()
            cp.wait()
            return carry

        lax.fori_loop(nu_ref[0], n_blocks, tail, 0)

    def issue(t, carry):
        _row_copy(_row(xn_ref, t), _row(xs_ref, dest_ref[0, 2 * t]), sem).start()
        _row_copy(_row(xn_ref, t), _row(xs_ref, dest_ref[0, 2 * t + 1]), sem).start()
        return carry

    lax.fori_loop(0, tm, issue, 0, unroll=ROW_DMA_UNROLL)

    for _ in range(2):
        _row_copy(xn_ref, xs_ref.at[pl.ds(0, tm), :], sem).wait()


def moe_dispatch(xn, dest, zstart, n_used, *, n_slots, tm):
    n, d = xn.shape
    dest3 = dest.reshape(n // tm, 1, 2 * tm)
    return pl.pallas_call(
        functools.partial(_dispatch_kernel, tm=tm, n_experts=zstart.shape[0]),
        grid_spec=pltpu.PrefetchScalarGridSpec(
            num_scalar_prefetch=2,
            grid=(n // tm,),
            in_specs=[pl.BlockSpec((None, 1, 2 * tm), lambda i, z, nu: (i, 0, 0),
                                   memory_space=pltpu.SMEM),
                      pl.BlockSpec((tm, d), lambda i, z, nu: (i, 0))],
            out_specs=pl.BlockSpec(memory_space=pl.ANY),
            scratch_shapes=[pltpu.VMEM((MOE_BLOCK, d), F32),
                            pltpu.SemaphoreType.DMA(())]),
        out_shape=jax.ShapeDtypeStruct((n_slots, d), F32),
        compiler_params=_cparams("arbitrary"),
        name="moe_dispatch",
    )(zstart, n_used, dest3, xn)


EXPERT_FF_CHUNK = 512


def _expert_kernel(plan_ref, nu_ref, x_ref, wg_hbm, wu_hbm, wd_hbm, y_ref,
                   wg_buf, wu_buf, wd_buf, sems, *, layer):
    i = pl.program_id(0)
    slot = plan_ref[1, i]

    def copies(expert, s):
        return [pltpu.make_async_copy(hbm.at[layer, expert], buf.at[s], sems.at[s])
                for hbm, buf in ((wg_hbm, wg_buf), (wu_hbm, wu_buf), (wd_hbm, wd_buf))]

    @pl.when(i == 0)
    def _():
        for cp in copies(plan_ref[0, 0], 0):
            cp.start()

    @pl.when(plan_ref[2, i] == 1)
    def _():
        for cp in copies(plan_ref[0, i], slot):
            cp.wait()

        @pl.when(plan_ref[3, i] >= 0)
        def _():
            for cp in copies(plan_ref[3, i], 1 - slot):
                cp.start()

    @pl.when(i < nu_ref[0])
    def _():
        x = x_ref[...].astype(BF16)
        wg, wu, wd = wg_buf.at[slot], wu_buf.at[slot], wd_buf.at[slot]
        y = None
        for c0 in range(0, wg.shape[1], EXPERT_FF_CHUNK):
            cols = slice(c0, c0 + EXPERT_FF_CHUNK)
            gate = jnp.dot(x, wg[:, cols].astype(BF16), preferred_element_type=F32)
            up = jnp.dot(x, wu[:, cols].astype(BF16), preferred_element_type=F32)
            hidden = (gate * _sigmoid(gate) * up).astype(BF16)
            part = jnp.dot(hidden, wd[cols, :].astype(BF16), preferred_element_type=F32)
            y = part if y is None else y + part
        y_ref[...] = y

    @pl.when(i >= nu_ref[0])
    def _():
        y_ref[...] = jnp.zeros(y_ref.shape, F32)


def moe_experts(xs, plan, n_used, w_gate, w_up, w_down, *, layer):
    n_slots = xs.shape[0]
    n_blocks = n_slots // MOE_BLOCK
    _, _, d, ff = w_gate.shape
    x_map = lambda i, plan, nu: (jnp.minimum(i, nu[0] - 1), 0)
    hbm = pl.BlockSpec(memory_space=pl.ANY)
    return pl.pallas_call(
        functools.partial(_expert_kernel, layer=layer),
        grid_spec=pltpu.PrefetchScalarGridSpec(
            num_scalar_prefetch=2,
            grid=(n_blocks,),
            in_specs=[pl.BlockSpec((MOE_BLOCK, d), x_map), hbm, hbm, hbm],
            out_specs=pl.BlockSpec((MOE_BLOCK, d), lambda i, plan, nu: (i, 0)),
            scratch_shapes=[pltpu.VMEM((2, d, ff), F32), pltpu.VMEM((2, d, ff), F32),
                            pltpu.VMEM((2, ff, d), F32), pltpu.SemaphoreType.DMA((2,))]),
        out_shape=jax.ShapeDtypeStruct((n_slots, d), F32),
        compiler_params=_cparams("arbitrary", vmem_limit=VMEM_LIMIT_EXPERTS),
        name="moe_experts",
    )(plan, n_used, xs, w_gate, w_up, w_down)


def _combine_kernel(d0_ref, d1_ref, d2_ref, h_ref, info_ref, y_ref, o_ref, buf1, buf2, sems, *, tm):
    p = pl.program_id(0)

    def request(d_ref, s):
        def issue(t, carry):
            _row_copy(_row(y_ref, d_ref[0, 2 * t]), _row(buf1.at[s], t), sems.at[s]).start()
            _row_copy(_row(y_ref, d_ref[0, 2 * t + 1]), _row(buf2.at[s], t), sems.at[s]).start()
            return carry

        lax.fori_loop(0, tm, issue, 0, unroll=ROW_DMA_UNROLL)

    def combine(s):
        for buf in (buf1, buf2):
            _row_copy(y_ref.at[pl.ds(0, tm), :], buf.at[s], sems.at[s]).wait()
        tile = slice(s * tm, (s + 1) * tm)
        info = info_ref[tile, :]
        lane = lax.broadcasted_iota(jnp.int32, (tm, LANES), 1)
        g1 = jnp.sum(jnp.where(lane == INFO_GATE, info, 0.0), axis=1, keepdims=True)
        g2 = jnp.sum(jnp.where(lane == INFO_GATE + 1, info, 0.0), axis=1, keepdims=True)
        o_ref[tile, :] = h_ref[tile, :] + g1 * buf1[s] + g2 * buf2[s]

    @pl.when(p == 0)
    def _():
        request(d0_ref, 0)

    request(d1_ref, 1)
    combine(0)

    @pl.when(p + 1 < pl.num_programs(0))
    def _():
        request(d2_ref, 0)

    combine(1)


def moe_combine(h, info, y, dest, *, tm):
    n, d = h.shape
    tiles = n // tm
    dest3 = dest.reshape(tiles, 1, 2 * tm)
    dest_spec = lambda index_map: pl.BlockSpec((None, 1, 2 * tm), index_map, memory_space=pltpu.SMEM)
    buf = pltpu.VMEM((2, tm, d), F32)
    return pl.pallas_call(
        functools.partial(_combine_kernel, tm=tm),
        grid=(tiles // 2,),
        in_specs=[dest_spec(lambda p: (2 * p, 0, 0)),
                  dest_spec(lambda p: (2 * p + 1, 0, 0)),
                  dest_spec(lambda p: (jnp.minimum(2 * p + 2, tiles - 1), 0, 0)),
                  pl.BlockSpec((2 * tm, d), lambda p: (p, 0)),
                  pl.BlockSpec((2 * tm, LANES), lambda p: (p, 0)),
                  pl.BlockSpec(memory_space=pl.ANY)],
        out_specs=pl.BlockSpec((2 * tm, d), lambda p: (p, 0)),
        out_shape=jax.ShapeDtypeStruct((n, d), F32),
        scratch_shapes=[buf, buf, pltpu.SemaphoreType.DMA((2,))],
        compiler_params=_cparams("arbitrary"),
        name="moe_combine",
    )(dest3, dest3, dest3, h, info, y)


def hierarchical_moe(h, gain, group_w, group_b, expert_w, expert_b, w_gate, w_up, w_down,
                     *, layer, router_tm, dispatch_tm, combine_tm):
    n, d = h.shape
    xn, info, cnt = moe_router(h, gain, group_w, group_b, expert_w, expert_b, tm=router_tm)

    n_experts = w_gate.shape[1]
    n_blocks = -(-2 * n // MOE_BLOCK) + n_experts
    n_slots = n_blocks * MOE_BLOCK
    experts = info[:, INFO_E:INFO_E + 2].astype(jnp.int32)
    ranks = info[:, INFO_RANK:INFO_RANK + 2].astype(jnp.int32)
    counts = cnt[0, :n_experts].astype(jnp.int32)
    padded = (counts + MOE_BLOCK - 1) // MOE_BLOCK * MOE_BLOCK
    pends = jnp.cumsum(padded)
    pstarts = pends - padded
    dest = pstarts[experts] + ranks
    n_used = pends[-1] // MOE_BLOCK
    blk = jnp.arange(n_blocks, dtype=jnp.int32)
    block_e = jnp.sum((pends[None, :] <= (blk * MOE_BLOCK)[:, None]).astype(jnp.int32), axis=1)
    block_e = jnp.minimum(block_e, n_experts - 1)
    block_e = jnp.where(blk < n_used, block_e, block_e[n_used - 1]).astype(jnp.int32)
    eid = jnp.arange(n_experts, dtype=jnp.int32)
    nonempty = padded > 0
    slot_e = (jnp.cumsum(nonempty.astype(jnp.int32)) - 1) % 2
    later = jnp.where(nonempty[None, :] & (eid[None, :] > eid[:, None]), eid[None, :], n_experts)
    next_e = jnp.min(later, axis=1)
    next_e = jnp.where(next_e < n_experts, next_e, -1)
    first = (blk < n_used) & ((blk == 0) | (block_e != jnp.roll(block_e, 1)))
    plan = jnp.stack([block_e, slot_e[block_e], first.astype(jnp.int32),
                      next_e[block_e]]).astype(jnp.int32)
    zstart = jnp.maximum(pends - MOE_BLOCK, 0).astype(jnp.int32)

    n_used = n_used.reshape(1).astype(jnp.int32)
    xs = moe_dispatch(xn, dest, zstart, n_used, n_slots=n_slots, tm=dispatch_tm)
    y = moe_experts(xs, plan, n_used, w_gate, w_up, w_down, layer=layer)
    return moe_combine(h, info, y, dest, tm=combine_tm)


def kernel(x, even_norm, even_w_in, conv_w, conv_b, conv_norm_g, conv_norm_b, sb_q_norm, sb_k_norm,
           even_w_out, odd_norm, odd_w_in, fox_forget_b, fox_q_norm, fox_k_norm, sc_w, odd_w_out,
           moe_norm, router_group_w, router_group_b, router_expert_w, router_expert_b,
           expert_w_gate, expert_w_up, expert_w_down):
    b, s, d = x.shape
    n = b * s
    h = x.reshape(n, d)
    sb_width = SB_HEADS * HEAD_DIM
    fox_width = FOX_HEADS * HEAD_DIM
    moe_tiles = dict(router_tm=512, dispatch_tm=512, combine_tm=256)

    def moe(h, layer):
        return hierarchical_moe(
            h, moe_norm[layer], router_group_w[layer], router_group_b[layer],
            router_expert_w[layer], router_expert_b[layer],
            expert_w_gate, expert_w_up, expert_w_down, layer=layer, **moe_tiles)

    u = norm_matmul(h, even_norm[0], even_w_in[0].astype(BF16), tm=1024, tn=1024)
    u3 = u.reshape(b, s, -1)
    a = conformer_conv(u3, conv_w[0], conv_b[0], conv_norm_g[0], conv_norm_b[0], ts=256)
    o = stick_breaking_attention(u3, sb_q_norm[0], sb_k_norm[0], col0=2 * CONV_CH, heads=SB_HEADS)
    h = out_proj_residual(a.reshape(n, CONV_CH), o.reshape(n, sb_width),
                          even_w_out[0].astype(BF16), h, tm=512)
    h = moe(h, 0)

    n_qkv = 3 * fox_width
    w_in = odd_w_in[0]
    w_in = jnp.concatenate(
        [w_in[:, :n_qkv], w_in[:, n_qkv + FOX_HEADS:], w_in[:, n_qkv:n_qkv + FOX_HEADS],
         jnp.zeros((d, MXU_WIDTH - FOX_HEADS), F32)], axis=1).astype(BF16)
    u = norm_matmul(h, odd_norm[0], w_in, tm=1024, tn=5 * MXU_WIDTH)
    u3 = u.reshape(b, s, -1)
    c_col = forget_cumsum(u3, fox_forget_b[0], col0=n_qkv + 3 * SC_CH, tc=512)
    o = forgetting_attention(u3, c_col, fox_q_norm[0], fox_k_norm[0], heads=FOX_HEADS)
    y = short_conv(u3, sc_w[0], col0=n_qkv, ts=256)
    h = out_proj_residual(o.reshape(n, fox_width), y.reshape(n, SC_CH),
                          odd_w_out[0].astype(BF16), h, tm=512)
    h = moe(h, 1)
    return h.reshape(b, s, d)
```

```python
import functools

import jax
import jax.numpy as jnp
from jax import lax
from jax.experimental import pallas as pl
from jax.experimental.pallas import tpu as pltpu

F32 = jnp.float32
BF16 = jnp.bfloat16

HEAD_DIM = 128
CONV_CH = 1024
CONV_WIDTH = 31
SB_HEADS = 8
FOX_HEADS = 8
SC_CH = 1024
SC_WIDTH = 3
N_GROUPS = 4
EXPERTS_PER_GROUP = 8
N_EXPERTS = N_GROUPS * EXPERTS_PER_GROUP
MOE_BLOCK = 256
RMS_EPS = 1e-6
LN_EPS = 1e-5

LANES = 128
SUBLANES = 8
MXU_WIDTH = 256
VMEM_CAPACITY = 64 * 1024 * 1024
VMEM_LIMIT = VMEM_CAPACITY * 7 // 8
VMEM_LIMIT_EXPERTS = VMEM_CAPACITY - 2 * 1024 * 1024


def _cparams(*sem, vmem_limit=VMEM_LIMIT):
    return pltpu.CompilerParams(dimension_semantics=sem, vmem_limit_bytes=vmem_limit)


def _sigmoid(x):
    return 1.0 / (1.0 + jnp.exp(-x))


def _softplus_neg_abs(z):
    return jnp.log(1.0 + jnp.exp(-jnp.abs(z)))


def _rms(x, g):
    ms = jnp.mean(x * x, axis=-1, keepdims=True)
    return x * lax.rsqrt(ms + RMS_EPS) * g


def _split_bf16(x, parts):
    out = []
    r = x
    for _ in range(parts - 1):
        p = r.astype(BF16)
        out.append(p)
        r = r - p.astype(F32)
    out.append(r.astype(BF16))
    return out


def _norm_matmul_kernel(x_ref, g_ref, w_ref, o_ref, xn_ref):
    @pl.when(pl.program_id(1) == 0)
    def _():
        xn_ref[...] = _rms(x_ref[...], g_ref[...]).astype(BF16)

    o_ref[...] = jnp.dot(xn_ref[...], w_ref[...], preferred_element_type=F32)


def norm_matmul(x, gain, w, *, tm, tn):
    n, d = x.shape
    f = w.shape[1]
    return pl.pallas_call(
        _norm_matmul_kernel,
        grid=(n // tm, f // tn),
        in_specs=[pl.BlockSpec((tm, d), lambda i, j: (i, 0)),
                  pl.BlockSpec((1, d), lambda i, j: (0, 0)),
                  pl.BlockSpec((d, tn), lambda i, j: (0, j))],
        out_specs=pl.BlockSpec((tm, tn), lambda i, j: (i, j)),
        out_shape=jax.ShapeDtypeStruct((n, f), F32),
        scratch_shapes=[pltpu.VMEM((tm, d), BF16)],
        compiler_params=_cparams("parallel", "arbitrary"),
        name="norm_matmul",
    )(x, gain.reshape(1, d), w)


def _conv_tile(abuf, w_ref, bias_ref, cbuf, *, ts, halo, width, rows):
    n_ch = cbuf.shape[1]
    off = halo - (width - 1)

    def chan_body(c, carry):
        lanes = pl.ds(pl.multiple_of(c * LANES, LANES), LANES)
        wc = w_ref[:, lanes]
        for r0 in range(0, ts, rows):
            if bias_ref is None:
                acc = jnp.zeros((rows, LANES), F32)
            else:
                acc = jnp.broadcast_to(bias_ref[:, lanes], (rows, LANES))
            win = abuf[pl.ds(r0, rows + halo), lanes]
            for shift in range(SUBLANES):
                taps = [k for k in range(width) if (off + k) % SUBLANES == shift]
                if not taps:
                    continue
                shifted = win if shift == 0 else pltpu.roll(win, rows + halo - shift, axis=0)
                for k in taps:
                    base = off + k - shift
                    acc = acc + wc[k:k + 1, :] * shifted[base:base + rows]
            cbuf[pl.ds(r0, rows), lanes] = acc
        return carry

    lax.fori_loop(0, n_ch // LANES, chan_body, 0)


def _carry_halo(abuf, *, ts, halo):
    s = pl.program_id(1)

    @pl.when(s == 0)
    def _():
        abuf[0:halo, :] = jnp.zeros((halo, abuf.shape[1]), F32)

    @pl.when(s > 0)
    def _():
        abuf[0:halo, :] = abuf[ts:ts + halo, :]


CONV_HALO = 32
SC_HALO = 8


def _conformer_kernel(av_ref, ag_ref, w_ref, cb_ref, lg_ref, lb_ref, o_ref, abuf, cbuf, *, ts):
    _carry_halo(abuf, ts=ts, halo=CONV_HALO)
    abuf[CONV_HALO:CONV_HALO + ts, :] = av_ref[...] * _sigmoid(ag_ref[...])
    _conv_tile(abuf, w_ref, cb_ref, cbuf, ts=ts, halo=CONV_HALO, width=CONV_WIDTH, rows=64)
    y = cbuf[...]
    mu = jnp.mean(y, axis=-1, keepdims=True)
    yc = y - mu
    var = jnp.mean(yc * yc, axis=-1, keepdims=True)
    yn = yc * lax.rsqrt(var + LN_EPS) * lg_ref[...] + lb_ref[...]
    o_ref[...] = (yn * _sigmoid(yn)).astype(BF16)


def conformer_conv(u3, conv_w, conv_b, ln_g, ln_b, *, ts):
    b, s, _ = u3.shape
    c = conv_w.shape[1]
    w_pad = jnp.zeros((CONV_HALO, c), F32).at[:CONV_WIDTH].set(conv_w)
    row = lambda a: a.reshape(1, c)
    full = lambda shape: pl.BlockSpec(shape, lambda bi, si: (0, 0))
    return pl.pallas_call(
        functools.partial(_conformer_kernel, ts=ts),
        grid=(b, s // ts),
        in_specs=[pl.BlockSpec((None, ts, c), lambda bi, si: (bi, si, 0)),
                  pl.BlockSpec((None, ts, c), lambda bi, si: (bi, si, 1)),
                  full((CONV_HALO, c)), full((1, c)), full((1, c)), full((1, c))],
        out_specs=pl.BlockSpec((None, ts, c), lambda bi, si: (bi, si, 0)),
        out_shape=jax.ShapeDtypeStruct((b, s, c), BF16),
        scratch_shapes=[pltpu.VMEM((CONV_HALO + ts, c), F32), pltpu.VMEM((ts, c), F32)],
        compiler_params=_cparams("parallel", "arbitrary"),
        name="conformer_conv",
    )(u3, u3, w_pad, row(conv_b), row(ln_g), row(ln_b))


def _short_conv_kernel(bg_ref, cg_ref, xv_ref, w_ref, o_ref, abuf, cbuf, *, ts):
    _carry_halo(abuf, ts=ts, halo=SC_HALO)
    abuf[SC_HALO:SC_HALO + ts, :] = cg_ref[...] * xv_ref[...]
    _conv_tile(abuf, w_ref, None, cbuf, ts=ts, halo=SC_HALO, width=SC_WIDTH, rows=64)
    o_ref[...] = (bg_ref[...] * cbuf[...]).astype(BF16)


def short_conv(u3, sc_w, *, col0, ts):
    b, s, _ = u3.shape
    c = sc_w.shape[1]
    j0 = col0 // c
    w_pad = jnp.zeros((SUBLANES, c), F32).at[:SC_WIDTH].set(sc_w)
    return pl.pallas_call(
        functools.partial(_short_conv_kernel, ts=ts),
        grid=(b, s // ts),
        in_specs=[pl.BlockSpec((None, ts, c), lambda bi, si: (bi, si, j0)),
                  pl.BlockSpec((None, ts, c), lambda bi, si: (bi, si, j0 + 1)),
                  pl.BlockSpec((None, ts, c), lambda bi, si: (bi, si, j0 + 2)),
                  pl.BlockSpec((SUBLANES, c), lambda bi, si: (0, 0))],
        out_specs=pl.BlockSpec((None, ts, c), lambda bi, si: (bi, si, 0)),
        out_shape=jax.ShapeDtypeStruct((b, s, c), BF16),
        scratch_shapes=[pltpu.VMEM((SC_HALO + ts, c), F32), pltpu.VMEM((ts, c), F32)],
        compiler_params=_cparams("parallel", "arbitrary"),
        name="short_conv",
    )(u3, u3, u3, w_pad)


ATTN_HEADS = 4
ATTN_T = 256


def _head(e):
    return slice(e * HEAD_DIM, (e + 1) * HEAD_DIM)


def _prep_kv(k_ref, v_ref, kg_ref, kn_ref, vb_ref):
    @pl.when(pl.program_id(2) == 0)
    def _():
        for e in range(ATTN_HEADS):
            kn_ref[:, _head(e)] = _rms(k_ref[:, _head(e)], kg_ref[...]).astype(BF16)
        vb_ref[...] = v_ref[...].astype(BF16)


def _prep_q(q_ref, qg_ref, e):
    return (_rms(q_ref[:, _head(e)], qg_ref[...]) * (HEAD_DIM ** -0.5)).astype(BF16)


def _qk(qn, kb):
    return lax.dot_general(qn, kb, (((1,), (1,)), ((), ())), preferred_element_type=F32)


SB_SUB = 128
SB_CUTOFF = -110.0


def _sb_kernel(q_ref, k_ref, v_ref, qg_ref, kg_ref, o_ref, kn_ref, vb_ref, acc_ref, r_ref):
    t = ATTN_T
    i = pl.program_id(2)
    _prep_kv(k_ref, v_ref, kg_ref, kn_ref, vb_ref)
    qn = [_prep_q(q_ref, qg_ref, e) for e in range(ATTN_HEADS)]
    acc_ref[...] = jnp.zeros(acc_ref.shape, F32)
    r_ref[...] = jnp.zeros(r_ref.shape, F32)

    jr = lax.broadcasted_iota(jnp.int32, (t, t), 0)
    sc = lax.broadcasted_iota(jnp.int32, (t, t), 1)
    later = jnp.where((jr > sc) & (jr // SB_SUB == sc // SB_SUB), 1.0, 0.0).astype(BF16)
    strict = sc < jr

    def local_terms(c, masked):
        ks = pl.multiple_of(c * t, t)
        out = []
        for e in range(ATTN_HEADS):
            z = _qk(qn[e], kn_ref[pl.ds(ks, t), _head(e)])
            log_beta = jnp.minimum(z, 0.0) - _softplus_neg_abs(z)
            log_om = log_beta - z
            if masked:
                log_om = jnp.where(strict, log_om, 0.0)
            hi, lo = _split_bf16(log_om, 2)
            suffix = (jnp.dot(hi, later, preferred_element_type=F32)
                      + jnp.dot(lo, later, preferred_element_type=F32))
            out.append((log_beta + suffix,
                        jnp.sum(log_om[:, SB_SUB:], axis=1, keepdims=True),
                        jnp.sum(log_om[:, :SB_SUB], axis=1, keepdims=True)))
        return tuple(out)

    def accumulate(c, masked, terms):
        ks = pl.multiple_of(c * t, t)
        for e in range(ATTN_HEADS):
            base, tot_near, tot_far = terms[e]
            r = r_ref[:, _head(e)]
            w = jnp.exp(base + jnp.concatenate([r + tot_near, r], axis=1))
            if masked:
                w = jnp.where(strict, w, 0.0)
            acc_ref[:, _head(e)] += jnp.dot(w.astype(BF16), vb_ref[pl.ds(ks, t), _head(e)],
                                            preferred_element_type=F32)
            r_ref[:, _head(e)] = r + (tot_near + tot_far)

    accumulate(i, True, local_terms(i, True))

    def more(carry):
        n, r_max, _ = carry
        return (n < i) & (r_max > SB_CUTOFF)

    def body(carry):
        n, _, terms = carry
        c = i - 1 - n
        ahead = local_terms(jnp.maximum(c - 1, 0), False)
        accumulate(c, False, terms)
        return n + 1, jnp.max(r_ref[...]), ahead

    first = local_terms(jnp.maximum(i - 1, 0), False)
    lax.while_loop(more, body, (jnp.int32(0), jnp.max(r_ref[...]), first))
    o_ref[...] = acc_ref[...].astype(BF16)


def _attn_specs(s, j0, heads):
    w = ATTN_HEADS * HEAD_DIM
    g0 = j0 // ATTN_HEADS
    gh = heads // ATTN_HEADS
    q_spec = pl.BlockSpec((None, ATTN_T, w), lambda bi, hi, qi: (bi, qi, g0 + hi))
    k_spec = pl.BlockSpec((None, s, w), lambda bi, hi, qi: (bi, 0, g0 + gh + hi),
                          pipeline_mode=pl.Buffered(1))
    v_spec = pl.BlockSpec((None, s, w), lambda bi, hi, qi: (bi, 0, g0 + 2 * gh + hi),
                          pipeline_mode=pl.Buffered(1))
    o_spec = pl.BlockSpec((None, ATTN_T, w), lambda bi, hi, qi: (bi, qi, hi))
    return q_spec, k_spec, v_spec, o_spec


def stick_breaking_attention(u3, q_gain, k_gain, *, col0, heads):
    b, s, _ = u3.shape
    w = ATTN_HEADS * HEAD_DIM
    q_spec, k_spec, v_spec, o_spec = _attn_specs(s, col0 // HEAD_DIM, heads)
    gain = lambda g: g.reshape(1, HEAD_DIM)
    full = lambda shape: pl.BlockSpec(shape, lambda bi, hi, qi: (0, 0))
    return pl.pallas_call(
        _sb_kernel,
        grid=(b, heads // ATTN_HEADS, s // ATTN_T),
        in_specs=[q_spec, k_spec, v_spec, full((1, HEAD_DIM)), full((1, HEAD_DIM))],
        out_specs=o_spec,
        out_shape=jax.ShapeDtypeStruct((b, s, heads * HEAD_DIM), BF16),
        scratch_shapes=[pltpu.VMEM((s, w), BF16), pltpu.VMEM((s, w), BF16),
                        pltpu.VMEM((ATTN_T, w), F32), pltpu.VMEM((ATTN_T, w), F32)],
        compiler_params=_cparams("parallel", "parallel", "arbitrary"),
        name="stick_breaking_attention",
    )(u3, u3, u3, gain(q_gain), gain(k_gain))


FOX_PREP = 512
FOX_K = 2 * HEAD_DIM


def _bias_lanes(c_rep, lane, first, sign):
    other = 3 - first
    tile = jnp.where((lane >= other) & (lane < other + 3), 1.0, 0.0)
    for k, part in enumerate(_split_bf16(sign * c_rep, 3)):
        tile = jnp.where(lane == first + k, part.astype(F32), tile)
    return tile.astype(BF16)


def _fox_kernel(q_ref, k_ref, v_ref, cc_ref, qg_ref, kg_ref, o_ref, kn_ref, vt_ref, cs_ref, acc_ref):
    t = ATTN_T
    hg = pl.program_id(1)
    i = pl.program_id(2)
    s = k_ref.shape[0]
    rows = lambda e: slice(e * HEAD_DIM, (e + 1) * HEAD_DIM)
    kcols = lambda e: slice(e * FOX_K, (e + 1) * FOX_K)

    @pl.when(i == 0)
    def _():
        lane = lax.broadcasted_iota(jnp.int32, (FOX_PREP, LANES), 1)
        for e in range(ATTN_HEADS):
            for c0 in range(0, s, FOX_PREP):
                blk = slice(c0, c0 + FOX_PREP)
                vt_ref[rows(e), blk] = v_ref[blk, _head(e)].T.astype(BF16)
                col = jnp.sum(jnp.where(lane == hg * ATTN_HEADS + e, cc_ref[blk, :], 0.0),
                              axis=1, keepdims=True)
                c_rep = jnp.broadcast_to(col, (FOX_PREP, LANES))
                cs_ref[blk, _head(e)] = c_rep
                kn = _rms(k_ref[blk, _head(e)], kg_ref[...]).astype(BF16)
                kn_ref[blk, kcols(e)] = jnp.concatenate(
                    [kn, _bias_lanes(c_rep, lane, 0, -1.0)], axis=1)

    qlane = lax.broadcasted_iota(jnp.int32, (t, LANES), 1)
    qrows = pl.ds(pl.multiple_of(i * t, t), t)
    qn = [jnp.concatenate([_prep_q(q_ref, qg_ref, e),
                           _bias_lanes(cs_ref[qrows, _head(e)], qlane, 3, 1.0)], axis=1)
          for e in range(ATTN_HEADS)]
    acc_ref[...] = jnp.zeros(acc_ref.shape, F32)
    kpos = lax.broadcasted_iota(jnp.int32, (t, t), 0)
    qpos = lax.broadcasted_iota(jnp.int32, (t, t), 1)

    def scores(j):
        ks = pl.multiple_of(j * t, t)
        return tuple(_qk(kn_ref[pl.ds(ks, t), kcols(e)], qn[e])
                     for e in range(ATTN_HEADS))

    def chunk(j, masked, stats, qk):
        ks = pl.multiple_of(j * t, t)
        out = []
        for e in range(ATTN_HEADS):
            m_prev, l_prev = stats[e]
            zt = qk[e]
            if masked:
                zt = jnp.where(kpos <= qpos, zt, -jnp.inf)
            m_new = jnp.maximum(m_prev, jnp.max(zt, axis=0, keepdims=True))
            alpha = jnp.exp(m_prev - m_new)
            pt = jnp.exp(zt - m_new)
            l_new = alpha * l_prev + jnp.sum(pt, axis=0, keepdims=True)
            acc_ref[rows(e), :] = alpha * acc_ref[rows(e), :] + jnp.dot(
                vt_ref[rows(e), pl.ds(ks, t)], pt.astype(BF16), preferred_element_type=F32)
            out.append((m_new, l_new))
        return tuple(out)

    init = tuple((jnp.full((1, t), -jnp.inf, F32), jnp.zeros((1, t), F32))
                 for _ in range(ATTN_HEADS))

    def body(j, carry):
        stats, qk = carry
        nxt = scores(j + 1)
        return chunk(j, False, stats, qk), nxt

    stats, qk = lax.fori_loop(0, i, body, (init, scores(0)))
    stats = chunk(i, True, stats, qk)
    for e in range(ATTN_HEADS):
        o_ref[:, _head(e)] = (acc_ref[rows(e), :] / stats[e][1]).T.astype(BF16)


def forgetting_attention(u3, c_col, q_gain, k_gain, *, heads):
    b, s, _ = u3.shape
    w = ATTN_HEADS * HEAD_DIM
    q_spec, k_spec, v_spec, o_spec = _attn_specs(s, 0, heads)
    gain = lambda g: g.reshape(1, HEAD_DIM)
    full = lambda shape: pl.BlockSpec(shape, lambda bi, hi, qi: (0, 0))
    return pl.pallas_call(
        _fox_kernel,
        grid=(b, heads // ATTN_HEADS, s // ATTN_T),
        in_specs=[q_spec, k_spec, v_spec,
                  pl.BlockSpec((None, s, LANES), lambda bi, hi, qi: (bi, 0, 0)),
                  full((1, HEAD_DIM)), full((1, HEAD_DIM))],
        out_specs=o_spec,
        out_shape=jax.ShapeDtypeStruct((b, s, heads * HEAD_DIM), BF16),
        scratch_shapes=[pltpu.VMEM((s, ATTN_HEADS * FOX_K), BF16), pltpu.VMEM((w, s), BF16),
                        pltpu.VMEM((s, w), F32), pltpu.VMEM((w, ATTN_T), F32)],
        compiler_params=_cparams("parallel", "parallel", "arbitrary"),
        name="forgetting_attention",
    )(u3, u3, u3, c_col, gain(q_gain), gain(k_gain))


def _forget_cumsum_kernel(f_ref, b_ref, o_ref, carry_ref, *, tc):
    @pl.when(pl.program_id(1) == 0)
    def _():
        carry_ref[...] = jnp.zeros((1, LANES), F32)

    x = f_ref[...] + b_ref[...]
    log_f = jnp.minimum(x, 0.0) - _softplus_neg_abs(x)
    r = lax.broadcasted_iota(jnp.int32, (tc, tc), 0)
    c = lax.broadcasted_iota(jnp.int32, (tc, tc), 1)
    tri = jnp.where(r >= c, 1.0, 0.0).astype(BF16)
    cs = carry_ref[...]
    for part in _split_bf16(log_f, 3):
        cs = cs + jnp.dot(tri, part, preferred_element_type=F32)
    o_ref[...] = cs
    carry_ref[...] = cs[tc - 1:tc, :]


def forget_cumsum(u3, forget_b, *, col0, tc):
    b, s, _ = u3.shape
    j0 = col0 // LANES
    b_pad = jnp.zeros((1, LANES), F32).at[0, :forget_b.shape[0]].set(forget_b)
    return pl.pallas_call(
        functools.partial(_forget_cumsum_kernel, tc=tc),
        grid=(b, s // tc),
        in_specs=[pl.BlockSpec((None, tc, LANES), lambda bi, si: (bi, si, j0)),
                  pl.BlockSpec((1, LANES), lambda bi, si: (0, 0))],
        out_specs=pl.BlockSpec((None, tc, LANES), lambda bi, si: (bi, si, 0)),
        out_shape=jax.ShapeDtypeStruct((b, s, LANES), F32),
        scratch_shapes=[pltpu.VMEM((1, LANES), F32)],
        compiler_params=_cparams("parallel", "arbitrary"),
        name="forget_cumsum",
    )(u3, b_pad)


def _out_proj_kernel(a_ref, b_ref, w1_ref, w2_ref, h_ref, o_ref):
    o_ref[...] = (h_ref[...]
                  + jnp.dot(a_ref[...], w1_ref[...], preferred_element_type=F32)
                  + jnp.dot(b_ref[...], w2_ref[...], preferred_element_type=F32))


def out_proj_residual(a, b, w, h, *, tm):
    n, ka = a.shape
    kb = b.shape[1]
    d = w.shape[1]
    return pl.pallas_call(
        _out_proj_kernel,
        grid=(n // tm,),
        in_specs=[pl.BlockSpec((tm, ka), lambda i: (i, 0)),
                  pl.BlockSpec((tm, kb), lambda i: (i, 0)),
                  pl.BlockSpec((ka, d), lambda i: (0, 0)),
                  pl.BlockSpec((kb, d), lambda i: (0, 0)),
                  pl.BlockSpec((tm, d), lambda i: (i, 0))],
        out_specs=pl.BlockSpec((tm, d), lambda i: (i, 0)),
        out_shape=jax.ShapeDtypeStruct((n, d), F32),
        compiler_params=_cparams("parallel"),
        name="out_proj_residual",
    )(a, b, w[:ka], w[ka:], h)


INFO_E, INFO_GATE, INFO_RANK = 0, 2, 4


def _lane_pick(x, lane, idx):
    return jnp.sum(jnp.where(lane == idx, x, 0.0), axis=1, keepdims=True)


def _router_kernel(h_ref, g_ref, wh_ref, wl_ref, b_ref, xn_ref, info_ref, info_t_ref, cnt_ref,
                   carry_ref, *, tm):
    @pl.when(pl.program_id(0) == 0)
    def _():
        carry_ref[...] = jnp.zeros((1, LANES), F32)

    xn = _rms(h_ref[...], g_ref[...])
    xn_ref[...] = xn

    xh, xl = _split_bf16(xn, 2)
    wh = wh_ref[...]
    logits = (jnp.dot(xh, wh, preferred_element_type=F32)
              + jnp.dot(xl, wh, preferred_element_type=F32)
              + jnp.dot(xh, wl_ref[...], preferred_element_type=F32)) + b_ref[...]

    lane = lax.broadcasted_iota(jnp.int32, (tm, LANES), 1).astype(F32)
    neg = -jnp.inf
    big = float(LANES)
    gl = jnp.where(lane < N_GROUPS, logits, neg)
    gmax = jnp.max(gl, axis=1, keepdims=True)
    g_top_p = 1.0 / jnp.sum(jnp.exp(gl - gmax), axis=1, keepdims=True)
    g_idx = jnp.min(jnp.where(gl == gmax, lane, big), axis=1, keepdims=True)

    lo = N_GROUPS + EXPERTS_PER_GROUP * g_idx
    el = jnp.where((lane >= lo) & (lane < lo + EXPERTS_PER_GROUP), logits, neg)
    m1 = jnp.max(el, axis=1, keepdims=True)
    i1 = jnp.min(jnp.where(el == m1, lane, big), axis=1, keepdims=True)
    el2 = jnp.where(lane == i1, neg, el)
    m2 = jnp.max(el2, axis=1, keepdims=True)
    i2 = jnp.min(jnp.where(el2 == m2, lane, big), axis=1, keepdims=True)
    ratio = jnp.exp(m2 - m1)
    p1 = 1.0 / (1.0 + ratio)
    p2 = ratio * p1
    e1 = i1 - N_GROUPS
    e2 = i2 - N_GROUPS

    onehot = jnp.where((lane == e1) | (lane == e2), 1.0, 0.0)
    r = lax.broadcasted_iota(jnp.int32, (tm, tm), 0)
    c = lax.broadcasted_iota(jnp.int32, (tm, tm), 1)
    before = jnp.where(r > c, 1.0, 0.0).astype(BF16)
    cnt = jnp.dot(before, onehot.astype(BF16), preferred_element_type=F32) + carry_ref[...]
    rank1 = _lane_pick(cnt, lane, e1)
    rank2 = _lane_pick(cnt, lane, e2)
    total = carry_ref[...] + jnp.sum(onehot, axis=0, keepdims=True)
    carry_ref[...] = total
    cnt_ref[...] = jnp.broadcast_to(total, (SUBLANES, LANES))

    info = jnp.zeros((tm, LANES), F32)
    for k, val in ((INFO_E, e1), (INFO_E + 1, e2), (INFO_GATE, g_top_p * p1),
                   (INFO_GATE + 1, g_top_p * p2), (INFO_RANK, rank1), (INFO_RANK + 1, rank2)):
        info = jnp.where(lane == k, val, info)
    info_ref[...] = info
    info_t_ref[...] = info.T[:SUBLANES, :]


def moe_router(h, gain, group_w, group_b, expert_w, expert_b, *, tm):
    n, d = h.shape
    n_logits = N_GROUPS + N_EXPERTS
    w = jnp.zeros((d, LANES), F32).at[:, :N_GROUPS].set(group_w).at[:, N_GROUPS:n_logits].set(expert_w)
    bias = jnp.zeros((1, LANES), F32).at[0, :N_GROUPS].set(group_b).at[0, N_GROUPS:n_logits].set(expert_b)
    w_hi = w.astype(BF16)
    w_lo = (w - w_hi.astype(F32)).astype(BF16)
    full = lambda shape: pl.BlockSpec(shape, lambda i: (0, 0))
    return pl.pallas_call(
        functools.partial(_router_kernel, tm=tm),
        grid=(n // tm,),
        in_specs=[pl.BlockSpec((tm, d), lambda i: (i, 0)), full((1, d)),
                  full((d, LANES)), full((d, LANES)), full((1, LANES))],
        out_specs=[pl.BlockSpec((tm, d), lambda i: (i, 0)),
                   pl.BlockSpec((tm, LANES), lambda i: (i, 0)),
                   pl.BlockSpec((SUBLANES, tm), lambda i: (0, i)),
                   full((SUBLANES, LANES))],
        out_shape=[jax.ShapeDtypeStruct((n, d), F32),
                   jax.ShapeDtypeStruct((n, LANES), F32),
                   jax.ShapeDtypeStruct((SUBLANES, n), F32),
                   jax.ShapeDtypeStruct((SUBLANES, LANES), F32)],
        scratch_shapes=[pltpu.VMEM((1, LANES), F32)],
        compiler_params=_cparams("arbitrary"),
        name="moe_router",
    )(h, gain.reshape(1, d), w_hi, w_lo, bias)


def _tile_dest(dest, tm):
    return [row.reshape(-1, 1, tm) for row in dest]


def _row(ref, i):
    return ref.at[pl.ds(i, 1), :]


def _row_copy(src, dst, sem):
    return pltpu.make_async_copy(src, dst, sem)


ROW_DMA_UNROLL = 8


def _dispatch_kernel(zstart_ref, nu_ref, dest1_ref, dest2_ref, xn_ref, xs_ref, zero_ref, sem, *, tm,
                     n_experts):
    tile = pl.program_id(0)
    n_blocks = xs_ref.shape[0] // MOE_BLOCK

    @pl.when(tile == 0)
    def _():
        zero_ref[...] = jnp.zeros(zero_ref.shape, F32)
        block = lambda start: _row_copy(
            zero_ref, xs_ref.at[pl.ds(pl.multiple_of(start, MOE_BLOCK), MOE_BLOCK), :], sem)
        for e in range(n_experts):
            block(zstart_ref[e]).start()
        for e in range(n_experts):
            block(0).wait()

        def tail(i, carry):
            cp = block(i * MOE_BLOCK)
            cp.start()
            cp.wait()
            return carry

        lax.fori_loop(nu_ref[0], n_blocks, tail, 0)

    def issue(t, carry):
        _row_copy(_row(xn_ref, t), _row(xs_ref, dest1_ref[0, t]), sem).start()
        _row_copy(_row(xn_ref, t), _row(xs_ref, dest2_ref[0, t]), sem).start()
        return carry

    lax.fori_loop(0, tm, issue, 0, unroll=ROW_DMA_UNROLL)

    for _ in range(2):
        _row_copy(xn_ref, xs_ref.at[pl.ds(0, tm), :], sem).wait()


def moe_dispatch(xn, dest, zstart, n_used, *, n_slots, tm):
    n, d = xn.shape
    dest_spec = pl.BlockSpec((None, 1, tm), lambda i, z, nu: (i, 0, 0), memory_space=pltpu.SMEM)
    return pl.pallas_call(
        functools.partial(_dispatch_kernel, tm=tm, n_experts=zstart.shape[0]),
        grid_spec=pltpu.PrefetchScalarGridSpec(
            num_scalar_prefetch=2,
            grid=(n // tm,),
            in_specs=[dest_spec, dest_spec, pl.BlockSpec((tm, d), lambda i, z, nu: (i, 0))],
            out_specs=pl.BlockSpec(memory_space=pl.ANY),
            scratch_shapes=[pltpu.VMEM((MOE_BLOCK, d), F32),
                            pltpu.SemaphoreType.DMA(())]),
        out_shape=jax.ShapeDtypeStruct((n_slots, d), F32),
        compiler_params=_cparams("arbitrary"),
        name="moe_dispatch",
    )(zstart, n_used, *_tile_dest(dest, tm), xn)


EXPERT_FF_CHUNK = 512


def _expert_kernel(plan_ref, nu_ref, x_ref, wg_hbm, wu_hbm, wd_hbm, y_ref,
                   wg_buf, wu_buf, wd_buf, sems, *, layer):
    i = pl.program_id(0)
    slot = plan_ref[1, i]

    def copies(expert, s):
        return [pltpu.make_async_copy(hbm.at[layer, expert], buf.at[s], sems.at[s])
                for hbm, buf in ((wg_hbm, wg_buf), (wu_hbm, wu_buf), (wd_hbm, wd_buf))]

    @pl.when(i == 0)
    def _():
        for cp in copies(plan_ref[0, 0], 0):
            cp.start()

    @pl.when(plan_ref[2, i] == 1)
    def _():
        for cp in copies(plan_ref[0, i], slot):
            cp.wait()

        @pl.when(plan_ref[3, i] >= 0)
        def _():
            for cp in copies(plan_ref[3, i], 1 - slot):
                cp.start()

    @pl.when(i < nu_ref[0])
    def _():
        x = x_ref[...].astype(BF16)
        wg, wu, wd = wg_buf.at[slot], wu_buf.at[slot], wd_buf.at[slot]
        y = None
        for c0 in range(0, wg.shape[1], EXPERT_FF_CHUNK):
            cols = slice(c0, c0 + EXPERT_FF_CHUNK)
            gate = jnp.dot(x, wg[:, cols].astype(BF16), preferred_element_type=F32)
            up = jnp.dot(x, wu[:, cols].astype(BF16), preferred_element_type=F32)
            hidden = (gate * _sigmoid(gate) * up).astype(BF16)
            part = jnp.dot(hidden, wd[cols, :].astype(BF16), preferred_element_type=F32)
            y = part if y is None else y + part
        y_ref[...] = y

    @pl.when(i >= nu_ref[0])
    def _():
        y_ref[...] = jnp.zeros(y_ref.shape, F32)


def moe_experts(xs, plan, n_used, w_gate, w_up, w_down, *, layer):
    n_slots = xs.shape[0]
    n_blocks = n_slots // MOE_BLOCK
    _, _, d, ff = w_gate.shape
    x_map = lambda i, plan, nu: (jnp.minimum(i, nu[0] - 1), 0)
    hbm = pl.BlockSpec(memory_space=pl.ANY)
    return pl.pallas_call(
        functools.partial(_expert_kernel, layer=layer),
        grid_spec=pltpu.PrefetchScalarGridSpec(
            num_scalar_prefetch=2,
            grid=(n_blocks,),
            in_specs=[pl.BlockSpec((MOE_BLOCK, d), x_map), hbm, hbm, hbm],
            out_specs=pl.BlockSpec((MOE_BLOCK, d), lambda i, plan, nu: (i, 0)),
            scratch_shapes=[pltpu.VMEM((2, d, ff), F32), pltpu.VMEM((2, d, ff), F32),
                            pltpu.VMEM((2, ff, d), F32), pltpu.SemaphoreType.DMA((2,))]),
        out_shape=jax.ShapeDtypeStruct((n_slots, d), F32),
        compiler_params=_cparams("arbitrary", vmem_limit=VMEM_LIMIT_EXPERTS),
        name="moe_experts",
    )(plan, n_used, xs, w_gate, w_up, w_down)


def _combine_kernel(a1_ref, a2_ref, b1_ref, b2_ref, c1_ref, c2_ref, h_ref, info_ref, y_ref, o_ref,
                    buf1, buf2, sems, *, tm):
    p = pl.program_id(0)

    def request(d1_ref, d2_ref, s):
        def issue(t, carry):
            _row_copy(_row(y_ref, d1_ref[0, t]), _row(buf1.at[s], t), sems.at[s]).start()
            _row_copy(_row(y_ref, d2_ref[0, t]), _row(buf2.at[s], t), sems.at[s]).start()
            return carry

        lax.fori_loop(0, tm, issue, 0, unroll=ROW_DMA_UNROLL)

    def combine(s):
        for buf in (buf1, buf2):
            _row_copy(y_ref.at[pl.ds(0, tm), :], buf.at[s], sems.at[s]).wait()
        tile = slice(s * tm, (s + 1) * tm)
        info = info_ref[tile, :]
        lane = lax.broadcasted_iota(jnp.int32, (tm, LANES), 1)
        g1 = jnp.sum(jnp.where(lane == INFO_GATE, info, 0.0), axis=1, keepdims=True)
        g2 = jnp.sum(jnp.where(lane == INFO_GATE + 1, info, 0.0), axis=1, keepdims=True)
        o_ref[tile, :] = h_ref[tile, :] + g1 * buf1[s] + g2 * buf2[s]

    @pl.when(p == 0)
    def _():
        request(a1_ref, a2_ref, 0)

    request(b1_ref, b2_ref, 1)
    combine(0)

    @pl.when(p + 1 < pl.num_programs(0))
    def _():
        request(c1_ref, c2_ref, 0)

    combine(1)


def moe_combine(h, info, y, dest, *, tm):
    n, d = h.shape
    tiles = n // tm
    dest_spec = lambda index_map: [pl.BlockSpec((None, 1, tm), index_map, memory_space=pltpu.SMEM)] * 2
    buf = pltpu.VMEM((2, tm, d), F32)
    return pl.pallas_call(
        functools.partial(_combine_kernel, tm=tm),
        grid=(tiles // 2,),
        in_specs=[*dest_spec(lambda p: (2 * p, 0, 0)),
                  *dest_spec(lambda p: (2 * p + 1, 0, 0)),
                  *dest_spec(lambda p: (jnp.minimum(2 * p + 2, tiles - 1), 0, 0)),
                  pl.BlockSpec((2 * tm, d), lambda p: (p, 0)),
                  pl.BlockSpec((2 * tm, LANES), lambda p: (p, 0)),
                  pl.BlockSpec(memory_space=pl.ANY)],
        out_specs=pl.BlockSpec((2 * tm, d), lambda p: (p, 0)),
        out_shape=jax.ShapeDtypeStruct((n, d), F32),
        scratch_shapes=[buf, buf, pltpu.SemaphoreType.DMA((2,))],
        compiler_params=_cparams("arbitrary"),
        name="moe_combine",
    )(*(3 * _tile_dest(dest, tm)), h, info, y)


def hierarchical_moe(h, gain, group_w, group_b, expert_w, expert_b, w_gate, w_up, w_down,
                     *, layer, router_tm, dispatch_tm, combine_tm):
    n, d = h.shape
    xn, info, info_t, cnt = moe_router(h, gain, group_w, group_b, expert_w, expert_b, tm=router_tm)

    n_experts = w_gate.shape[1]
    n_blocks = -(-2 * n // MOE_BLOCK) + n_experts
    n_slots = n_blocks * MOE_BLOCK
    experts = info_t[INFO_E:INFO_E + 2].astype(jnp.int32)
    ranks = info_t[INFO_RANK:INFO_RANK + 2].astype(jnp.int32)
    counts = cnt[0, :n_experts].astype(jnp.int32)
    padded = (counts + MOE_BLOCK - 1) // MOE_BLOCK * MOE_BLOCK
    pends = jnp.cumsum(padded)
    pstarts = pends - padded
    dest = ranks
    for e in range(n_experts):
        dest = dest + jnp.where(experts == e, pstarts[e], 0)
    n_used = pends[-1] // MOE_BLOCK
    blk = jnp.arange(n_blocks, dtype=jnp.int32)
    block_e = jnp.sum((pends[None, :] <= (blk * MOE_BLOCK)[:, None]).astype(jnp.int32), axis=1)
    block_e = jnp.minimum(block_e, n_experts - 1)
    block_e = jnp.where(blk < n_used, block_e, block_e[n_used - 1]).astype(jnp.int32)
    eid = jnp.arange(n_experts, dtype=jnp.int32)
    nonempty = padded > 0
    slot_e = (jnp.cumsum(nonempty.astype(jnp.int32)) - 1) % 2
    later = jnp.where(nonempty[None, :] & (eid[None, :] > eid[:, None]), eid[None, :], n_experts)
    next_e = jnp.min(later, axis=1)
    next_e = jnp.where(next_e < n_experts, next_e, -1)
    first = (blk < n_used) & ((blk == 0) | (block_e != jnp.roll(block_e, 1)))
    plan = jnp.stack([block_e, slot_e[block_e], first.astype(jnp.int32),
                      next_e[block_e]]).astype(jnp.int32)
    zstart = jnp.maximum(pends - MOE_BLOCK, 0).astype(jnp.int32)

    n_used = n_used.reshape(1).astype(jnp.int32)
    xs = moe_dispatch(xn, dest, zstart, n_used, n_slots=n_slots, tm=dispatch_tm)
    y = moe_experts(xs, plan, n_used, w_gate, w_up, w_down, layer=layer)
    return moe_combine(h, info, y, dest, tm=combine_tm)


def kernel(x, even_norm, even_w_in, conv_w, conv_b, conv_norm_g, conv_norm_b, sb_q_norm, sb_k_norm,
           even_w_out, odd_norm, odd_w_in, fox_forget_b, fox_q_norm, fox_k_norm, sc_w, odd_w_out,
           moe_norm, router_group_w, router_group_b, router_expert_w, router_expert_b,
           expert_w_gate, expert_w_up, expert_w_down):
    b, s, d = x.shape
    n = b * s
    h = x.reshape(n, d)
    sb_width = SB_HEADS * HEAD_DIM
    fox_width = FOX_HEADS * HEAD_DIM
    moe_tiles = dict(router_tm=512, dispatch_tm=512, combine_tm=256)

    def moe(h, layer):
        return hierarchical_moe(
            h, moe_norm[layer], router_group_w[layer], router_group_b[layer],
            router_expert_w[layer], router_expert_b[layer],
            expert_w_gate, expert_w_up, expert_w_down, layer=layer, **moe_tiles)

    u = norm_matmul(h, even_norm[0], even_w_in[0].astype(BF16), tm=1024, tn=1024)
    u3 = u.reshape(b, s, -1)
    a = conformer_conv(u3, conv_w[0], conv_b[0], conv_norm_g[0], conv_norm_b[0], ts=256)
    o = stick_breaking_attention(u3, sb_q_norm[0], sb_k_norm[0], col0=2 * CONV_CH, heads=SB_HEADS)
    h = out_proj_residual(a.reshape(n, CONV_CH), o.reshape(n, sb_width),
                          even_w_out[0].astype(BF16), h, tm=512)
    h = moe(h, 0)

    n_qkv = 3 * fox_width
    w_in = odd_w_in[0]
    w_in = jnp.concatenate(
        [w_in[:, :n_qkv], w_in[:, n_qkv + FOX_HEADS:], w_in[:, n_qkv:n_qkv + FOX_HEADS],
         jnp.zeros((d, MXU_WIDTH - FOX_HEADS), F32)], axis=1).astype(BF16)
    u = norm_matmul(h, odd_norm[0], w_in, tm=1024, tn=5 * MXU_WIDTH)
    u3 = u.reshape(b, s, -1)
    c_col = forget_cumsum(u3, fox_forget_b[0], col0=n_qkv + 3 * SC_CH, tc=512)
    o = forgetting_attention(u3, c_col, fox_q_norm[0], fox_k_norm[0], heads=FOX_HEADS)
    y = short_conv(u3, sc_w[0], col0=n_qkv, ts=256)
    h = out_proj_residual(o.reshape(n, fox_width), y.reshape(n, SC_CH),
                          odd_w_out[0].astype(BF16), h, tm=512)
    h = moe(h, 1)
    return h.reshape(b, s, d)
```
